```python
import jax, jax.numpy as jnp
from jax import lax
import numpy as np

D_MODEL = 1024
BATCH = 32
SEQ = 256
DEPTH = 4
DEC_BATCH = 4
DEC_SEQ = 2048
PAST_LEN = 512

GRID_W = 64
N_MIXERS = 2
N_ATTN_LAYERS = (DEPTH + 1) // 2
N_POOL_LAYERS = DEPTH // 2
HEAD_DIM = 128
N_HEADS = D_MODEL // HEAD_DIM
N_KV_HEADS = 2
Q_W = N_HEADS * HEAD_DIM
KV_W = N_KV_HEADS * HEAD_DIM
QKV_W = Q_W + 2 * KV_W
ROPE_PAIRS = HEAD_DIM // 4
ROPE_BASE = 10000.0
Q_BLOCK = 128
POOL_WINDOWS = (2, 4, 8, 16)
N_POOL_GROUPS = len(POOL_WINDOWS)
POOL_DG = D_MODEL // N_POOL_GROUPS
D_FF = ((8 * D_MODEL // 3 + 255) // 256) * 256
N_MOD = 6
EPS = 1e-6

kernel_name = "hybrid_attn_pool_diffusion_step"


def rmsnorm(x, g):
    xf = x.astype(jnp.float32)
    y = xf * lax.rsqrt(jnp.mean(xf * xf, axis=-1, keepdims=True) + EPS)
    return (y * g.astype(jnp.float32)).astype(x.dtype)


def modulation(cond, w, b):
    ada = jax.nn.silu(cond) @ w + b
    return ada.reshape(cond.shape[0], 1, N_MOD, D_MODEL)


def modulate(h, shift, scale):
    return h * (1.0 + scale) + shift


def project_qkv(h, w_qkv, qk_g):
    B, T, _ = h.shape
    qkv = h @ w_qkv
    q = qkv[..., :Q_W].reshape(B, T, N_HEADS, HEAD_DIM)
    k = qkv[..., Q_W:Q_W + KV_W].reshape(B, T, N_KV_HEADS, HEAD_DIM)
    v = qkv[..., Q_W + KV_W:].reshape(B, T, N_KV_HEADS, HEAD_DIM)
    return rmsnorm(q, qk_g[0]), rmsnorm(k, qk_g[1]), v


def axial_rope(x):
    B, n, H, _ = x.shape
    rows = n // GRID_W
    row = jnp.broadcast_to(jnp.arange(rows)[:, None], (rows, GRID_W)).reshape(n).astype(jnp.float32)
    col = jnp.broadcast_to(jnp.arange(GRID_W)[None, :], (rows, GRID_W)).reshape(n).astype(jnp.float32)
    inv = ROPE_BASE ** (-jnp.arange(ROPE_PAIRS, dtype=jnp.float32) / ROPE_PAIRS)
    ang = jnp.stack([row[:, None] * inv, col[:, None] * inv], axis=1)
    cos = jnp.cos(ang)[None, :, None]
    sin = jnp.sin(ang)[None, :, None]
    xf = x.astype(jnp.float32).reshape(B, n, H, 2, 2, ROPE_PAIRS)
    x1, x2 = xf[..., 0, :], xf[..., 1, :]
    out = jnp.stack([x1 * cos - x2 * sin, x2 * cos + x1 * sin], axis=-2)
    return out.reshape(x.shape).astype(x.dtype)


def block_attention(q, k, v):
    B, Tq, H, Dh = q.shape
    kvh = k.shape[2]
    G = H // kvh
    nb = Tq // Q_BLOCK
    scale = 1.0 / np.sqrt(Dh)
    kf = k.astype(jnp.float32)
    qb = q.reshape(B, nb, Q_BLOCK, kvh, G, Dh).transpose(1, 0, 2, 3, 4, 5)

    def one_block(qblk):
        s = jnp.einsum('bqkgd,bskd->bkgqs', qblk.astype(jnp.float32), kf) * scale
        p = jax.nn.softmax(s, axis=-1).astype(v.dtype)
        return jnp.einsum('bkgqs,bskd->bqkgd', p, v)

    o = lax.map(one_block, qb)
    return o.transpose(1, 0, 2, 3, 4, 5).reshape(B, Tq, H * Dh)


def multiscale_pool(h, w_pool, pool_scale):
    B, T, D = h.shape
    hf = h.astype(jnp.float32)
    cs = jnp.concatenate([jnp.zeros((B, 1, D), jnp.float32), jnp.cumsum(hf, axis=1)], axis=1)
    t = jnp.arange(T)
    outs = []
    for g, w in enumerate(POOL_WINDOWS):
        lo = jnp.clip(t - w // 2, 0, T)
        hi = jnp.clip(t + w - w // 2, 0, T)
        cnt = (hi - lo).astype(jnp.float32)[None, :, None]
        csg = cs[..., g * POOL_DG:(g + 1) * POOL_DG]
        pooled = (csg[:, hi] - csg[:, lo]) / cnt - hf[..., g * POOL_DG:(g + 1) * POOL_DG]
        outs.append(pooled.astype(h.dtype) @ w_pool[g])
    return jnp.concatenate(outs, axis=-1) * pool_scale


def swiglu(h, w_up, w_down):
    gu = h @ w_up
    return (jax.nn.silu(gu[..., :D_FF]) * gu[..., D_FF:]) @ w_down


def setup_inputs(seed: int = 0) -> dict:
    key = jax.random.key(seed)
    ks = jax.random.split(key, 16)
    f32 = jnp.float32
    nrm = lambda k, s: jax.random.normal(k, s, f32)
    return {
        "x_prompt": nrm(ks[0], (BATCH, SEQ, D_MODEL)),
        "x_sample": nrm(ks[1], (DEC_BATCH, DEC_SEQ, D_MODEL)),
        "c": nrm(ks[2], (DEC_BATCH, D_MODEL)),
        "cache_k": nrm(ks[3], (DEC_BATCH, N_ATTN_LAYERS, PAST_LEN, N_KV_HEADS, HEAD_DIM)),
        "cache_v": nrm(ks[4], (DEC_BATCH, N_ATTN_LAYERS, PAST_LEN, N_KV_HEADS, HEAD_DIM)),
        "c_ctx": nrm(ks[5], (D_MODEL,)),
        "w_ada": nrm(ks[6], (DEPTH, D_MODEL, N_MOD * D_MODEL)) * (0.5 * D_MODEL ** -0.5),
        "b_ada": nrm(ks[7], (DEPTH, N_MOD * D_MODEL)) * 0.02,
        "norm_gains": 1.0 + 0.05 * nrm(ks[8], (DEPTH, 4, D_MODEL)),
        "w_qkv": nrm(ks[9], (N_ATTN_LAYERS, D_MODEL, QKV_W)) * D_MODEL ** -0.5,
        "qk_gains": 1.0 + 0.05 * nrm(ks[10], (N_ATTN_LAYERS, 2, HEAD_DIM)),
        "w_o": nrm(ks[11], (N_ATTN_LAYERS, Q_W, D_MODEL)) * Q_W ** -0.5,
        "w_pool": nrm(ks[12], (N_POOL_LAYERS, N_POOL_GROUPS, POOL_DG, POOL_DG)) * POOL_DG ** -0.5,
        "pool_scale": 0.5 + 0.1 * nrm(ks[13], (N_POOL_LAYERS, D_MODEL)),
        "w_up": nrm(ks[14], (DEPTH, D_MODEL, 2 * D_FF)) * D_MODEL ** -0.5,
        "w_down": nrm(ks[15], (DEPTH, D_FF, D_MODEL)) * D_FF ** -0.5,
    }


def reference(x_prompt, x_sample, c, cache_k, cache_v, c_ctx, w_ada, b_ada, norm_gains,
              w_qkv, qk_gains, w_o, w_pool, pool_scale, w_up, w_down):
    xp = x_prompt
    xs = x_sample
    new_ks = []
    new_vs = []
    for layer in range(DEPTH):
        g = norm_gains[layer]
        mp = modulation(c_ctx[None, :], w_ada[layer], b_ada[layer])
        ms = modulation(c, w_ada[layer], b_ada[layer])
        hp = modulate(rmsnorm(xp, g[0]), mp[:, :, 0], mp[:, :, 1])
        hs = modulate(rmsnorm(xs, g[0]), ms[:, :, 0], ms[:, :, 1])
        if layer % N_MIXERS == 0:
            a = layer // N_MIXERS
            qp, kp, vp = project_qkv(hp, w_qkv[a], qk_gains[a])
            op = block_attention(qp, kp, vp) @ w_o[a]
            new_ks.append(kp)
            new_vs.append(vp)
            qs, ks_, vs_ = project_qkv(hs, w_qkv[a], qk_gains[a])
            qs = axial_rope(qs)
            ks_ = axial_rope(ks_)
            k_all = jnp.concatenate([ks_, cache_k[:, a]], axis=1)
            v_all = jnp.concatenate([vs_, cache_v[:, a]], axis=1)
            os_ = block_attention(qs, k_all, v_all) @ w_o[a]
        else:
            p = layer // N_MIXERS
            op = multiscale_pool(hp, w_pool[p], pool_scale[p])
            os_ = multiscale_pool(hs, w_pool[p], pool_scale[p])
        xp = xp + mp[:, :, 2] * rmsnorm(op, g[1])
        xs = xs + ms[:, :, 2] * rmsnorm(os_, g[1])
        hp = modulate(rmsnorm(xp, g[2]), mp[:, :, 3], mp[:, :, 4])
        hs = modulate(rmsnorm(xs, g[2]), ms[:, :, 3], ms[:, :, 4])
        xp = xp + mp[:, :, 5] * rmsnorm(swiglu(hp, w_up[layer], w_down[layer]), g[3])
        xs = xs + ms[:, :, 5] * rmsnorm(swiglu(hs, w_up[layer], w_down[layer]), g[3])
    new_k = jnp.stack(new_ks, axis=1)
    new_v = jnp.stack(new_vs, axis=1)
    return (xp, xs, new_k, new_v)
```

```python
import functools
import math

import jax
import jax.numpy as jnp
from jax import lax
from jax.experimental import pallas as pl
from jax.experimental.pallas import tpu as pltpu

F32 = jnp.float32
BF16 = jnp.bfloat16

D = 1024
N_CTX_SEQ = 32
CTX_T = 256
N_LAT_SEQ = 4
LAT_T = 2048
PAST = 512
DEPTH = 4
GRID_W = 64
HD = 128
N_HEADS = 8
N_KV = 2
GQA = N_HEADS // N_KV
Q_W = N_HEADS * HD
KV_W = N_KV * HD
QKV_W = Q_W + 2 * KV_W
ROPE_PAIRS = HD // 4
ROPE_BASE = 10000.0
POOL_WINDOWS = (2, 4, 8, 16)
POOL_DG = D // len(POOL_WINDOWS)
D_FF = 2816
N_MOD = 6
EPS = 1e-6

N_CTX_TOK = N_CTX_SEQ * CTX_T
N_LAT_TOK = N_LAT_SEQ * LAT_T
N_TOK = N_CTX_TOK + N_LAT_TOK
N_COND = 8
Q_SCALE = math.log2(math.e) / math.sqrt(HD)

VMEM_LIMIT = 56 * 1024 * 1024

BM = 512
POOL_BM = 2048
ATT_TQ = 256
CTX_NB = 2


def _cond_row(i, bm):
  n_ctx = N_CTX_TOK // bm
  per_seq = LAT_T // bm
  return jnp.where(i < n_ctx, 0, 1 + (i - n_ctx) // per_seq)


def _rms_scale(x):
  return x * lax.rsqrt(jnp.mean(x * x, axis=-1, keepdims=True) + EPS)


def _silu(x):
  return x / (1.0 + jnp.exp(-x))


def _mod_kernel(cond_ref, w_ref, b_ref, o_ref):
  s = _silu(cond_ref[...]).astype(BF16)
  o_ref[0] = jnp.dot(s, w_ref[0].astype(BF16), preferred_element_type=F32) + b_ref[0]


def _modulation(cond, w_ada, b_ada):
  tn = 1536
  return pl.pallas_call(
      _mod_kernel,
      grid=(DEPTH, N_MOD * D // tn),
      in_specs=[
          pl.BlockSpec((N_COND, D), lambda l, j: (0, 0)),
          pl.BlockSpec((1, D, tn), lambda l, j: (l, 0, j)),
          pl.BlockSpec((1, 1, tn), lambda l, j: (l, 0, j)),
      ],
      out_specs=pl.BlockSpec((1, N_COND, tn), lambda l, j: (l, 0, j)),
      out_shape=jax.ShapeDtypeStruct((DEPTH, N_COND, N_MOD * D), F32),
      compiler_params=pltpu.CompilerParams(
          dimension_semantics=("arbitrary", "arbitrary"), vmem_limit_bytes=VMEM_LIMIT),
      name="modulation",
  )(cond, w_ada, b_ada.reshape(DEPTH, 1, N_MOD * D))


def _swap_halves(x):
  lane = lax.broadcasted_iota(jnp.int32, x.shape, 1)
  first = (lane % (2 * ROPE_PAIRS)) < ROPE_PAIRS
  return jnp.where(first, pltpu.roll(x, HD - ROPE_PAIRS, 1), pltpu.roll(x, ROPE_PAIRS, 1))


def _qkv_kernel(x_ref, ada_ref, g_ref, w_ref, qkg_ref, cs_ref,
                q_ref, k_ref, v_ref, nk_ref, nv_ref):
  i = pl.program_id(0)
  ada = ada_ref[0]
  h = (_rms_scale(x_ref[...]) * g_ref[0:1]) * (1.0 + ada[1:2]) + ada[0:1]
  qkv = jnp.dot(h.astype(BF16), w_ref[...], preferred_element_type=F32)
  gq = qkg_ref[0:1] * Q_SCALE
  gk = qkg_ref[1:2]
  qn = [_rms_scale(qkv[:, hd * HD:(hd + 1) * HD]) * gq for hd in range(N_HEADS)]
  kn = [_rms_scale(qkv[:, Q_W + hd * HD:Q_W + (hd + 1) * HD]) * gk for hd in range(N_KV)]
  v = qkv[:, Q_W + KV_W:]
  v_ref[...] = v.astype(BF16)
  nv_ref[...] = v
  for hd in range(N_KV):
    nk_ref[:, hd * HD:(hd + 1) * HD] = kn[hd]

  is_ctx = i < N_CTX_TOK // BM

  @pl.when(is_ctx)
  def _():
    for hd in range(N_HEADS):
      q_ref[:, hd * HD:(hd + 1) * HD] = qn[hd].astype(BF16)
    for hd in range(N_KV):
      k_ref[:, hd * HD:(hd + 1) * HD] = kn[hd].astype(BF16)

  @pl.when(jnp.logical_not(is_ctx))
  def _():
    cos = cs_ref[:, :HD]
    sin = cs_ref[:, HD:]
    for hd in range(N_HEADS):
      q_ref[:, hd * HD:(hd + 1) * HD] = (qn[hd] * cos + _swap_halves(qn[hd]) * sin).astype(BF16)
    for hd in range(N_KV):
      k_ref[:, hd * HD:(hd + 1) * HD] = (kn[hd] * cos + _swap_halves(kn[hd]) * sin).astype(BF16)


def _qkv(x, ada_l, gains_l, w_qkv, qk_gains, rope_cs):
  n_ctx = N_CTX_TOK // BM
  per_seq = LAT_T // BM
  row = lambda i: (i, 0)
  const = lambda i: (0, 0)
  return pl.pallas_call(
      _qkv_kernel,
      grid=(N_TOK // BM,),
      in_specs=[
          pl.BlockSpec((BM, D), row),
          pl.BlockSpec((1, N_MOD, D), lambda i: (_cond_row(i, BM), 0, 0)),
          pl.BlockSpec((4, D), const),
          pl.BlockSpec((D, QKV_W), const, pipeline_mode=pl.Buffered(1)),
          pl.BlockSpec((2, HD), const),
          pl.BlockSpec((BM, 2 * HD), lambda i: (jnp.maximum(i - n_ctx, 0) % per_seq, 0)),
      ],
      out_specs=[
          pl.BlockSpec((BM, Q_W), row),
          pl.BlockSpec((BM, KV_W), row),
          pl.BlockSpec((BM, KV_W), row),
          pl.BlockSpec((BM, KV_W), row),
          pl.BlockSpec((BM, KV_W), row),
      ],
      out_shape=[
          jax.ShapeDtypeStruct((N_TOK, Q_W), BF16),
          jax.ShapeDtypeStruct((N_TOK, KV_W), BF16),
          jax.ShapeDtypeStruct((N_TOK, KV_W), BF16),
          jax.ShapeDtypeStruct((N_TOK, KV_W), F32),
          jax.ShapeDtypeStruct((N_TOK, KV_W), F32),
      ],
      compiler_params=pltpu.CompilerParams(
          dimension_semantics=("arbitrary",), vmem_limit_bytes=VMEM_LIMIT),
      name="qkv",
  )(x, ada_l, gains_l, w_qkv, qk_gains, rope_cs)


def _nt_dot(a, b):
  return lax.dot_general(a, b, (((1,), (1,)), ((), ())), preferred_element_type=F32)


def _with_ones(v):
  return jnp.concatenate([v, jnp.ones_like(v)], axis=1)


def _stack_heads(q, tq):
  return jnp.concatenate([q[:, g * HD:(g + 1) * HD] for g in range(GQA)], axis=0)


def _unstack_heads(o, tq):
  return jnp.concatenate([o[g * tq:(g + 1) * tq] for g in range(GQA)], axis=1)


def _attn_lat_kernel(q_ref, k_ref, v_ref, ck_ref, cv_ref, o_ref):
  qs = _stack_heads(q_ref[...], ATT_TQ)
  s1 = _nt_dot(qs, k_ref[...])
  s2 = _nt_dot(qs, ck_ref[0, 0].astype(BF16))
  m = jnp.maximum(jnp.max(s1, axis=-1, keepdims=True), jnp.max(s2, axis=-1, keepdims=True))
  p1 = jnp.exp2(s1 - m).astype(BF16)
  p2 = jnp.exp2(s2 - m).astype(BF16)
  oa = (jnp.dot(p1, _with_ones(v_ref[...]), preferred_element_type=F32)
        + jnp.dot(p2, _with_ones(cv_ref[0, 0].astype(BF16)), preferred_element_type=F32))
  o = oa[:, :HD] / oa[:, HD:]
  o_ref[...] = _unstack_heads(o, ATT_TQ).astype(BF16)


def _attn_latent(q, k, v, cache_k, cache_v, layer_a):
  nq = LAT_T // ATT_TQ
  q_off = N_CTX_TOK // ATT_TQ
  kv_off = N_CTX_TOK // LAT_T
  ck = cache_k.reshape(N_LAT_SEQ, -1, PAST, KV_W)
  cv = cache_v.reshape(N_LAT_SEQ, -1, PAST, KV_W)
  return pl.pallas_call(
      _attn_lat_kernel,
      grid=(N_LAT_SEQ, N_KV, nq),
      in_specs=[
          pl.BlockSpec((ATT_TQ, GQA * HD), lambda b, h, j: (q_off + b * nq + j, h)),
          pl.BlockSpec((LAT_T, HD), lambda b, h, j: (kv_off + b, h)),
          pl.BlockSpec((LAT_T, HD), lambda b, h, j: (kv_off + b, h)),
          pl.BlockSpec((1, 1, PAST, HD), lambda b, h, j: (b, layer_a, 0, h)),
          pl.BlockSpec((1, 1, PAST, HD), lambda b, h, j: (b, layer_a, 0, h)),
      ],
      out_specs=pl.BlockSpec((ATT_TQ, GQA * HD), lambda b, h, j: (b * nq + j, h)),
      out_shape=jax.ShapeDtypeStruct((N_LAT_TOK, Q_W), BF16),
      compiler_params=pltpu.CompilerParams(
          dimension_semantics=("arbitrary", "arbitrary", "arbitrary"),
          vmem_limit_bytes=VMEM_LIMIT),
      name="attn_latent",
  )(q, k, v, ck, cv)


def _attn_ctx_kernel(q_ref, k_ref, v_ref, o_ref):
  for b in range(CTX_NB):
    rows = slice(b * CTX_T, (b + 1) * CTX_T)
    for h in range(N_KV):
      qs = _stack_heads(q_ref[rows, h * GQA * HD:(h + 1) * GQA * HD], CTX_T)
      s = _nt_dot(qs, k_ref[rows, h * HD:(h + 1) * HD])
      p = jnp.exp2(s - jnp.max(s, axis=-1, keepdims=True)).astype(BF16)
      oa = jnp.dot(p, _with_ones(v_ref[rows, h * HD:(h + 1) * HD]), preferred_element_type=F32)
      o = oa[:, :HD] / oa[:, HD:]
      o_ref[rows, h * GQA * HD:(h + 1) * GQA * HD] = _unstack_heads(o, CTX_T).astype(BF16)


def _attn_context(q, k, v):
  bm = CTX_NB * CTX_T
  row = lambda i: (i, 0)
  return pl.pallas_call(
      _attn_ctx_kernel,
      grid=(N_CTX_TOK // bm,),
      in_specs=[
          pl.BlockSpec((bm, Q_W), row),
          pl.BlockSpec((bm, KV_W), row),
          pl.BlockSpec((bm, KV_W), row),
      ],
      out_specs=pl.BlockSpec((bm, Q_W), row),
      out_shape=jax.ShapeDtypeStruct((N_CTX_TOK, Q_W), BF16),
      compiler_params=pltpu.CompilerParams(
          dimension_semantics=("arbitrary",), vmem_limit_bytes=VMEM_LIMIT),
      name="attn_context",
  )(q, k, v)


def _pool_kernel(x_ref, ada_ref, g_ref, w_ref, ps_ref, o_ref):
  i = pl.program_id(0)
  seq_t = jnp.where(i < N_CTX_TOK // POOL_BM, CTX_T, LAT_T)
  ada = ada_ref[0]
  h = (_rms_scale(x_ref[...]) * g_ref[0:1]) * (1.0 + ada[1:2]) + ada[0:1]
  t = lax.broadcasted_iota(jnp.int32, (POOL_BM, POOL_DG), 0) & (seq_t - 1)

  def shifted(a, d):
    tt = t + d
    valid = jnp.logical_and(tt >= 0, tt < seq_t)
    return jnp.where(valid, pltpu.roll(a, (-d) % POOL_BM, 0), 0.0)

  for g, w in enumerate(POOL_WINDOWS):
    hg = h[:, g * POOL_DG:(g + 1) * POOL_DG]
    back, fwd, m = hg, hg, 1
    while m < w // 2:
      back = back + shifted(back, -m)
      fwd = fwd + shifted(fwd, m)
      m *= 2
    win = shifted(back, -1) + fwd
    lo = jnp.maximum(t - w // 2, 0)
    hi = jnp.minimum(t + w - w // 2, seq_t)
    pooled = win / (hi - lo).astype(F32) - hg
    og = jnp.dot(pooled.astype(BF16), w_ref[g], preferred_element_type=F32)
    o_ref[:, g * POOL_DG:(g + 1) * POOL_DG] = (
        og * ps_ref[:, g * POOL_DG:(g + 1) * POOL_DG]).astype(BF16)


def _pool(x, ada_l, gains_l, w_pool, pool_scale):
  return pl.pallas_call(
      _pool_kernel,
      grid=(N_TOK // POOL_BM,),
      in_specs=[
          pl.BlockSpec((POOL_BM, D), lambda i: (i, 0)),
          pl.BlockSpec((1, N_MOD, D), lambda i: (_cond_row(i, POOL_BM), 0, 0)),
          pl.BlockSpec((4, D), lambda i: (0, 0)),
          pl.BlockSpec((len(POOL_WINDOWS), POOL_DG, POOL_DG), lambda i: (0, 0, 0)),
          pl.BlockSpec((1, D), lambda i: (0, 0)),
      ],
      out_specs=pl.BlockSpec((POOL_BM, D), lambda i: (i, 0)),
      out_shape=jax.ShapeDtypeStruct((N_TOK, D), BF16),
      compiler_params=pltpu.CompilerParams(
          dimension_semantics=("arbitrary",), vmem_limit_bytes=VMEM_LIMIT),
      name="pool",
  )(x, ada_l, gains_l, w_pool, pool_scale.reshape(1, D))


def _ffn_kernel(*refs, has_wo):
  if has_wo:
    x_ref, mix_ref, ada_ref, g_ref, wo_ref, wup_ref, wdn_ref, o_ref = refs
    a = jnp.dot(mix_ref[...], wo_ref[...], preferred_element_type=F32)
  else:
    x_ref, mix_ref, ada_ref, g_ref, wup_ref, wdn_ref, o_ref = refs
    a = mix_ref[...].astype(F32)
  ada = ada_ref[0]
  x1 = x_ref[...] + ada[2:3] * (_rms_scale(a) * g_ref[1:2])
  h = (_rms_scale(x1) * g_ref[2:3]) * (1.0 + ada[4:5]) + ada[3:4]
  gu = jnp.dot(h.astype(BF16), wup_ref[...], preferred_element_type=F32)
  act = (_silu(gu[:, :D_FF]) * gu[:, D_FF:]).astype(BF16)
  y = jnp.dot(act, wdn_ref[...], preferred_element_type=F32)
  o_ref[...] = x1 + ada[5:6] * (_rms_scale(y) * g_ref[3:4])


def _ffn(x, mix, ada_l, gains_l, w_o, w_up, w_down):
  row = lambda i: (i, 0)
  const = lambda i: (0, 0)
  resident = lambda shape: pl.BlockSpec(shape, const, pipeline_mode=pl.Buffered(1))
  in_specs = [
      pl.BlockSpec((BM, D), row),
      pl.BlockSpec((BM, D), row),
      pl.BlockSpec((1, N_MOD, D), lambda i: (_cond_row(i, BM), 0, 0)),
      pl.BlockSpec((4, D), const),
  ]
  args = [x, mix, ada_l, gains_l]
  if w_o is not None:
    in_specs.append(resident((Q_W, D)))
    args.append(w_o)
  in_specs += [resident((D, 2 * D_FF)), resident((D_FF, D))]
  args += [w_up, w_down]
  return pl.pallas_call(
      functools.partial(_ffn_kernel, has_wo=w_o is not None),
      grid=(N_TOK // BM,),
      in_specs=in_specs,
      out_specs=pl.BlockSpec((BM, D), row),
      out_shape=jax.ShapeDtypeStruct((N_TOK, D), F32),
      compiler_params=pltpu.CompilerParams(
          dimension_semantics=("arbitrary",), vmem_limit_bytes=VMEM_LIMIT),
      name="ffn_attn" if w_o is not None else "ffn_pool",
  )(*args)


def _rope_tables():
  pos = jnp.arange(LAT_T)
  row = (pos // GRID_W).astype(F32)
  col = (pos % GRID_W).astype(F32)
  inv = ROPE_BASE ** (-jnp.arange(ROPE_PAIRS, dtype=F32) / ROPE_PAIRS)
  ar = row[:, None] * inv
  ac = col[:, None] * inv
  cos = jnp.concatenate([jnp.cos(ar), jnp.cos(ar), jnp.cos(ac), jnp.cos(ac)], axis=1)
  sin = jnp.concatenate([-jnp.sin(ar), jnp.sin(ar), -jnp.sin(ac), jnp.sin(ac)], axis=1)
  return jnp.concatenate([cos, sin], axis=1)


def kernel(x_prompt, x_sample, c, cache_k, cache_v, c_ctx, w_ada, b_ada, norm_gains,
           w_qkv, qk_gains, w_o, w_pool, pool_scale, w_up, w_down):
  x = jnp.concatenate([x_prompt.reshape(N_CTX_TOK, D), x_sample.reshape(N_LAT_TOK, D)], axis=0)
  cond = jnp.concatenate(
      [c_ctx[None, :], c, jnp.zeros((N_COND - 1 - N_LAT_SEQ, D), F32)], axis=0)
  ada = _modulation(cond, w_ada, b_ada).reshape(DEPTH, N_COND, N_MOD, D)
  rope_cs = _rope_tables()
  w_qkv_b = w_qkv.astype(BF16)
  w_o_b = w_o.astype(BF16)
  w_pool_b = w_pool.astype(BF16)
  w_up_b = w_up.astype(BF16)
  w_down_b = w_down.astype(BF16)

  new_ks, new_vs = [], []
  for layer in range(DEPTH):
    idx = layer // 2
    if layer % 2 == 0:
      q, k, v, nk, nv = _qkv(x, ada[layer], norm_gains[layer], w_qkv_b[idx], qk_gains[idx], rope_cs)
      new_ks.append(nk[:N_CTX_TOK].reshape(N_CTX_SEQ, CTX_T, N_KV, HD))
      new_vs.append(nv[:N_CTX_TOK].reshape(N_CTX_SEQ, CTX_T, N_KV, HD))
      mix = jnp.concatenate(
          [_attn_context(q, k, v), _attn_latent(q, k, v, cache_k, cache_v, idx)], axis=0)
      x = _ffn(x, mix, ada[layer], norm_gains[layer], w_o_b[idx], w_up_b[layer], w_down_b[layer])
    else:
      mix = _pool(x, ada[layer], norm_gains[layer], w_pool_b[idx], pool_scale[idx])
      x = _ffn(x, mix, ada[layer], norm_gains[layer], None, w_up_b[layer], w_down_b[layer])

  y_prompt = x[:N_CTX_TOK].reshape(N_CTX_SEQ, CTX_T, D)
  y_sample = x[N_CTX_TOK:].reshape(N_LAT_SEQ, LAT_T, D)
  return (y_prompt, y_sample, jnp.stack(new_ks, axis=1), jnp.stack(new_vs, axis=1))
```

```python
import functools
import math

import jax
import jax.numpy as jnp
from jax import lax
from jax.experimental import pallas as pl
from jax.experimental.pallas import tpu as pltpu

F32 = jnp.float32
BF16 = jnp.bfloat16

D = 1024
N_CTX_SEQ = 32
CTX_T = 256
N_LAT_SEQ = 4
LAT_T = 2048
PAST = 512
DEPTH = 4
N_ATTN = 2
GRID_W = 64
HD = 128
N_HEADS = 8
N_KV = 2
GQA = N_HEADS // N_KV
Q_W = N_HEADS * HD
KV_W = N_KV * HD
QKV_W = Q_W + 2 * KV_W
ROPE_PAIRS = HD // 4
ROPE_BASE = 10000.0
POOL_WINDOWS = (2, 4, 8, 16)
N_POOL_G = len(POOL_WINDOWS)
POOL_DG = D // N_POOL_G
D_FF = 2816
N_MOD = 6
EPS = 1e-6

N_CTX_TOK = N_CTX_SEQ * CTX_T
N_LAT_TOK = N_LAT_SEQ * LAT_T
N_TOK = N_CTX_TOK + N_LAT_TOK
N_COND = 8
Q_SCALE = math.log2(math.e) / math.sqrt(HD)

VMEM_LIMIT = 56 * 1024 * 1024

BM = 512
N_CTX_BLK = N_CTX_TOK // BM
N_LAT_BLK = N_LAT_TOK // BM
N_BLK = N_CTX_BLK + N_LAT_BLK
BLK_PER_LAT_SEQ = LAT_T // BM
POOL_TILE = CTX_T
POOL_HALO = max(POOL_WINDOWS) // 2
ATT_TQ = 256
CTX_NB = 2


def _token_specs(split, width):
  if not split:
    return [pl.BlockSpec((BM, width), lambda i: (i, 0))]
  return [pl.BlockSpec((BM, width), lambda i: (jnp.minimum(i, N_CTX_BLK - 1), 0)),
          pl.BlockSpec((BM, width), lambda i: (jnp.maximum(i - N_CTX_BLK, 0), 0))]


def _token_shapes(split, width, dtype):
  if not split:
    return [jax.ShapeDtypeStruct((N_TOK, width), dtype)]
  return [jax.ShapeDtypeStruct((N_CTX_TOK, width), dtype),
          jax.ShapeDtypeStruct((N_LAT_TOK, width), dtype)]


def _token_load(is_ctx, refs):
  if len(refs) == 1:
    return refs[0][...]
  return jnp.where(is_ctx, refs[0][...], refs[1][...])


def _token_store(is_ctx, refs, value):
  if len(refs) == 1:
    refs[0][...] = value
    return

  @pl.when(is_ctx)
  def _():
    refs[0][...] = value

  @pl.when(jnp.logical_not(is_ctx))
  def _():
    refs[1][...] = value


def _cond_row(i):
  return jnp.where(i < N_CTX_BLK, 0, 1 + (i - N_CTX_BLK) // BLK_PER_LAT_SEQ)


def _mod_rows(ada_ref, row):
  return [ada_ref[0, m, pl.ds(row, 1), :] for m in range(N_MOD)]


def _layer_spec(shape, layer):
  zeros = (0,) * len(shape)
  return pl.BlockSpec((1,) + shape, lambda *_: (layer,) + zeros, pipeline_mode=pl.Buffered(1))


def _rms_scale(x):
  return x * lax.rsqrt(jnp.mean(x * x, axis=-1, keepdims=True) + EPS)


def _silu(x):
  return x / (1.0 + jnp.exp(-x))


def _params(n_grid_dims):
  return pltpu.CompilerParams(
      dimension_semantics=("arbitrary",) * n_grid_dims, vmem_limit_bytes=VMEM_LIMIT)


def _mod_kernel(cond_ref, w_ref, b_ref, o_ref):
  s = _silu(cond_ref[...]).astype(BF16)
  o_ref[0, 0] = jnp.dot(s, w_ref[0].astype(BF16), preferred_element_type=F32) + b_ref[0, 0]


def _modulation(cond, w_ada, b_ada):
  return pl.pallas_call(
      _mod_kernel,
      grid=(DEPTH, N_MOD),
      in_specs=[
          pl.BlockSpec((N_COND, D), lambda l, m: (0, 0)),
          pl.BlockSpec((1, D, D), lambda l, m: (l, 0, m)),
          pl.BlockSpec((1, 1, 1, D), lambda l, m: (l, m, 0, 0)),
      ],
      out_specs=pl.BlockSpec((1, 1, N_COND, D), lambda l, m: (l, m, 0, 0)),
      out_shape=jax.ShapeDtypeStruct((DEPTH, N_MOD, N_COND, D), F32),
      compiler_params=_params(2),
      name="modulation",
  )(cond, w_ada, b_ada.reshape(DEPTH, N_MOD, 1, D))


def _swap_halves(x):
  lane = lax.broadcasted_iota(jnp.int32, x.shape, 1)
  first = (lane % (2 * ROPE_PAIRS)) < ROPE_PAIRS
  return jnp.where(first, pltpu.roll(x, HD - ROPE_PAIRS, 1), pltpu.roll(x, ROPE_PAIRS, 1))


def _qkv_kernel(*refs, n_x):
  x_refs, refs = refs[:n_x], refs[n_x:]
  ada_ref, g_ref, w_ref, qkg_ref, cs_ref = refs[:5]
  q_ref, k_ref, v_ref, nk_ref, nv_ref = refs[-5:]
  i = pl.program_id(0)
  is_ctx = i < N_CTX_BLK
  shift, scale = _mod_rows(ada_ref, _cond_row(i))[:2]
  h = (_rms_scale(_token_load(is_ctx, x_refs)) * g_ref[0, 0:1]) * (1.0 + scale) + shift
  qkv = jnp.dot(h.astype(BF16), w_ref[0], preferred_element_type=F32)
  gq = qkg_ref[0, 0:1] * Q_SCALE
  gk = qkg_ref[0, 1:2]
  qn = [_rms_scale(qkv[:, hd * HD:(hd + 1) * HD]) * gq for hd in range(N_HEADS)]
  kn = [_rms_scale(qkv[:, Q_W + hd * HD:Q_W + (hd + 1) * HD]) * gk for hd in range(N_KV)]
  v = qkv[:, Q_W + KV_W:]
  v_ref[...] = v.astype(BF16)

  @pl.when(is_ctx)
  def _():
    for hd in range(N_HEADS):
      q_ref[:, hd * HD:(hd + 1) * HD] = qn[hd].astype(BF16)
    for hd in range(N_KV):
      k_ref[:, hd * HD:(hd + 1) * HD] = kn[hd].astype(BF16)
    for s in range(BM // CTX_T):
      rows = slice(s * CTX_T, (s + 1) * CTX_T)
      nv_ref[s, 0] = v[rows]
      for hd in range(N_KV):
        nk_ref[s, 0, :, hd * HD:(hd + 1) * HD] = kn[hd][rows]

  @pl.when(jnp.logical_not(is_ctx))
  def _():
    cos = cs_ref[:, :HD]
    sin = cs_ref[:, HD:]
    for hd in range(N_HEADS):
      q_ref[:, hd * HD:(hd + 1) * HD] = (qn[hd] * cos + _swap_halves(qn[hd]) * sin).astype(BF16)
    for hd in range(N_KV):
      k_ref[:, hd * HD:(hd + 1) * HD] = (kn[hd] * cos + _swap_halves(kn[hd]) * sin).astype(BF16)


def _qkv(xs, ada, gains, w_qkv, qk_gains, rope_cs, new_kv, layer):
  a = layer // 2
  seqs = BM // CTX_T
  new_spec = pl.BlockSpec(
      (seqs, 1, CTX_T, KV_W), lambda i: (jnp.minimum(i, N_CTX_BLK - 1), a, 0, 0))
  new_shape = jax.ShapeDtypeStruct((N_CTX_SEQ, N_ATTN, CTX_T, KV_W), F32)
  args = list(xs) + [ada, gains, w_qkv, qk_gains, rope_cs]
  in_specs = _token_specs(len(xs) == 2, D) + [
      _layer_spec((N_MOD, N_COND, D), layer),
      _layer_spec((4, D), layer),
      _layer_spec((D, QKV_W), a),
      _layer_spec((2, HD), a),
      pl.BlockSpec(
          (BM, 2 * HD), lambda i: (jnp.maximum(i - N_CTX_BLK, 0) % BLK_PER_LAT_SEQ, 0)),
  ]
  aliases = {}
  if new_kv is not None:
    aliases = {len(args): 3, len(args) + 1: 4}
    args += list(new_kv)
    in_specs += [pl.BlockSpec(memory_space=pl.ANY)] * 2
  return pl.pallas_call(
      functools.partial(_qkv_kernel, n_x=len(xs)),
      grid=(N_BLK,),
      in_specs=in_specs,
      out_specs=(_token_specs(False, Q_W) + _token_specs(False, KV_W) * 2 + [new_spec] * 2),
      out_shape=(_token_shapes(False, Q_W, BF16) + _token_shapes(False, KV_W, BF16) * 2
                 + [new_shape] * 2),
      input_output_aliases=aliases,
      compiler_params=_params(1),
      name="qkv",
  )(*args)


def _nt_dot(a, b):
  return lax.dot_general(a, b, (((1,), (1,)), ((), ())), preferred_element_type=F32)


def _with_ones(v):
  return jnp.concatenate([v, jnp.ones_like(v)], axis=1)


def _softmax_pv(s, v1):
  p = jnp.exp2(s - jnp.max(s, axis=-1, keepdims=True)).astype(BF16)
  oa = jnp.dot(p, v1, preferred_element_type=F32)
  return oa[:, :HD] / oa[:, HD:]


def _attn_ctx_kernel(q_ref, k_ref, v_ref, o_ref):
  for b in range(CTX_NB):
    rows = slice(b * CTX_T, (b + 1) * CTX_T)
    for h in range(N_KV):
      kv_cols = slice(h * HD, (h + 1) * HD)
      v1 = _with_ones(v_ref[rows, kv_cols])
      for g in range(GQA):
        cols = slice((h * GQA + g) * HD, (h * GQA + g + 1) * HD)
        s = _nt_dot(q_ref[rows, cols], k_ref[rows, kv_cols])
        o_ref[rows, cols] = _softmax_pv(s, v1).astype(BF16)


def _attn_context(q, k, v):
  bm = CTX_NB * CTX_T
  row = lambda i: (i, 0)
  return pl.pallas_call(
      _attn_ctx_kernel,
      grid=(N_CTX_TOK // bm,),
      in_specs=[
          pl.BlockSpec((bm, Q_W), row),
          pl.BlockSpec((bm, KV_W), row),
          pl.BlockSpec((bm, KV_W), row),
      ],
      out_specs=pl.BlockSpec((bm, Q_W), row),
      out_shape=jax.ShapeDtypeStruct((N_TOK, Q_W), BF16),
      compiler_params=_params(1),
      name="attn_context",
  )(q, k, v)


def _attn_lat_kernel(q_ref, k_ref, v_ref, ck_ref, cv_ref, mix_in, o_ref, kcat, vcat):
  del mix_in

  @pl.when(pl.program_id(2) == 0)
  def _():
    kcat[:LAT_T] = k_ref[...]
    kcat[LAT_T:] = ck_ref[0, 0].astype(BF16)
    vcat[:LAT_T] = _with_ones(v_ref[...])
    vcat[LAT_T:] = _with_ones(cv_ref[0, 0].astype(BF16))

  for g in range(GQA):
    cols = slice(g * HD, (g + 1) * HD)
    o_ref[:, cols] = _softmax_pv(_nt_dot(q_ref[:, cols], kcat[...]), vcat[...]).astype(BF16)


def _attn_latent(q, k, v, cache_k, cache_v, mix, a):
  nq = LAT_T // ATT_TQ
  q0 = N_CTX_TOK // ATT_TQ
  kv0 = N_CTX_TOK // LAT_T
  ck = cache_k.reshape(N_LAT_SEQ, N_ATTN, PAST, KV_W)
  cv = cache_v.reshape(N_LAT_SEQ, N_ATTN, PAST, KV_W)
  return pl.pallas_call(
      _attn_lat_kernel,
      grid=(N_LAT_SEQ, N_KV, nq),
      in_specs=[
          pl.BlockSpec((ATT_TQ, GQA * HD), lambda b, h, j: (q0 + b * nq + j, h)),
          pl.BlockSpec((LAT_T, HD), lambda b, h, j: (kv0 + b, h)),
          pl.BlockSpec((LAT_T, HD), lambda b, h, j: (kv0 + b, h)),
          pl.BlockSpec((1, 1, PAST, HD), lambda b, h, j: (b, a, 0, h)),
          pl.BlockSpec((1, 1, PAST, HD), lambda b, h, j: (b, a, 0, h)),
          pl.BlockSpec(memory_space=pl.ANY),
      ],
      out_specs=pl.BlockSpec((ATT_TQ, GQA * HD), lambda b, h, j: (q0 + b * nq + j, h)),
      out_shape=jax.ShapeDtypeStruct((N_TOK, Q_W), BF16),
      scratch_shapes=[pltpu.VMEM((LAT_T + PAST, HD), BF16),
                      pltpu.VMEM((LAT_T + PAST, 2 * HD), BF16)],
      input_output_aliases={5: 0},
      compiler_params=_params(3),
      name="attn_latent",
  )(q, k, v, ck, cv, mix)


def _pool_kernel(x_ref, xprev_ref, xnext_ref, ada_ref, g_ref, w_ref, ps_ref, o_ref):
  i = pl.program_id(0)
  is_ctx = i < N_CTX_BLK
  shift, scale = _mod_rows(ada_ref, _cond_row(i))[:2]
  pre = lambda x: (_rms_scale(x) * g_ref[0, 0:1]) * (1.0 + scale) + shift
  h = pre(x_ref[...])
  seq_pos = jnp.maximum(i - N_CTX_BLK, 0) % BLK_PER_LAT_SEQ
  is_first = jnp.logical_or(is_ctx, seq_pos == 0)
  is_last = jnp.logical_or(is_ctx, seq_pos == BLK_PER_LAT_SEQ - 1)
  ctx_edge = jnp.where(is_ctx, 1.0, 0.0)
  first_edge = jnp.where(is_first, 1.0, 0.0)
  last_edge = jnp.where(is_last, 1.0, 0.0)
  h_prev = jnp.where(is_first, 0.0, pre(xprev_ref[...]))
  h_next = jnp.where(is_last, 0.0, pre(xnext_ref[...]))
  n_tiles = BM // POOL_TILE
  ext_rows = POOL_TILE + 2 * POOL_HALO
  row = lax.broadcasted_iota(jnp.int32, (POOL_TILE, POOL_DG), 0)

  for g, w in enumerate(POOL_WINDOWS):
    cols = slice(g * POOL_DG, (g + 1) * POOL_DG)
    lo_cut = jnp.maximum(w // 2 - row, 0).astype(F32)
    hi_cut = jnp.maximum(row + (w - w // 2) - POOL_TILE, 0).astype(F32)
    for j in range(n_tiles):
      r0 = j * POOL_TILE
      starts = first_edge if j == 0 else ctx_edge
      ends = last_edge if j == n_tiles - 1 else ctx_edge
      hg = h[r0:r0 + POOL_TILE, cols]
      before = (h_prev[:, cols] if j == 0
                else jnp.where(is_ctx, 0.0, h[r0 - POOL_HALO:r0, cols]))
      after = (h_next[:, cols] if j == n_tiles - 1
               else jnp.where(is_ctx, 0.0, h[r0 + POOL_TILE:r0 + POOL_TILE + POOL_HALO, cols]))
      ext = jnp.concatenate([before, hg, after], axis=0)
      back, fwd, m = ext, ext, 1
      while m < w // 2:
        back = back + pltpu.roll(back, m, 0)
        fwd = fwd + pltpu.roll(fwd, ext_rows - m, 0)
        m *= 2
      win = (pltpu.roll(back, 1, 0) + fwd)[POOL_HALO:POOL_HALO + POOL_TILE]
      cnt = w - lo_cut * starts - hi_cut * ends
      pooled = win / cnt - hg
      og = jnp.dot(pooled.astype(BF16), w_ref[0, g], preferred_element_type=F32)
      o_ref[r0:r0 + POOL_TILE, cols] = (og * ps_ref[0, :, cols]).astype(BF16)


def _pool(x, ada, gains, w_pool, pool_scale, layer):
  p = layer // 2
  halo_per_blk = BM // POOL_HALO
  n_halo = N_TOK // POOL_HALO
  prev_spec = pl.BlockSpec((POOL_HALO, D), lambda i: (jnp.maximum(i * halo_per_blk - 1, 0), 0))
  next_spec = pl.BlockSpec(
      (POOL_HALO, D), lambda i: (jnp.minimum((i + 1) * halo_per_blk, n_halo - 1), 0))
  return pl.pallas_call(
      _pool_kernel,
      grid=(N_BLK,),
      in_specs=_token_specs(False, D) + [
          prev_spec, next_spec,
          _layer_spec((N_MOD, N_COND, D), layer),
          _layer_spec((4, D), layer),
          _layer_spec((N_POOL_G, POOL_DG, POOL_DG), p),
          _layer_spec((1, D), p),
      ],
      out_specs=_token_specs(False, D)[0],
      out_shape=_token_shapes(False, D, BF16)[0],
      compiler_params=_params(1),
      name="pool",
  )(x, x, x, ada, gains, w_pool, pool_scale.reshape(-1, 1, D))


def _ffn_kernel(*refs, n_x, n_out, has_wo):
  x_refs, refs = refs[:n_x], refs[n_x:]
  o_refs, refs = refs[-n_out:], refs[:-n_out]
  if has_wo:
    mix_ref, ada_ref, g_ref, wo_ref, wup_ref, wdn_ref = refs
  else:
    mix_ref, ada_ref, g_ref, wup_ref, wdn_ref = refs
  i = pl.program_id(0)
  is_ctx = i < N_CTX_BLK
  _, _, gate_m, shift_f, scale_f, gate_f = _mod_rows(ada_ref, _cond_row(i))
  if has_wo:
    a = jnp.dot(mix_ref[...], wo_ref[0], preferred_element_type=F32)
  else:
    a = mix_ref[...].astype(F32)
  x1 = _token_load(is_ctx, x_refs) + gate_m * (_rms_scale(a) * g_ref[0, 1:2])
  h = (_rms_scale(x1) * g_ref[0, 2:3]) * (1.0 + scale_f) + shift_f
  gu = jnp.dot(h.astype(BF16), wup_ref[0], preferred_element_type=F32)
  act = (_silu(gu[:, :D_FF]) * gu[:, D_FF:]).astype(BF16)
  y = jnp.dot(act, wdn_ref[0], preferred_element_type=F32)
  _token_store(is_ctx, o_refs, x1 + gate_f * (_rms_scale(y) * g_ref[0, 3:4]))


def _ffn(xs, mix, ada, gains, w_o, w_up, w_down, layer, split_out):
  has_wo = w_o is not None
  in_specs = _token_specs(len(xs) == 2, D) + _token_specs(False, D) + [
      _layer_spec((N_MOD, N_COND, D), layer),
      _layer_spec((4, D), layer),
  ]
  args = list(xs) + [mix, ada, gains]
  if has_wo:
    in_specs.append(_layer_spec((Q_W, D), layer // 2))
    args.append(w_o)
  in_specs += [_layer_spec((D, 2 * D_FF), layer), _layer_spec((D_FF, D), layer)]
  args += [w_up, w_down]
  return pl.pallas_call(
      functools.partial(_ffn_kernel, n_x=len(xs), n_out=2 if split_out else 1, has_wo=has_wo),
      grid=(N_BLK,),
      in_specs=in_specs,
      out_specs=_token_specs(split_out, D),
      out_shape=_token_shapes(split_out, D, F32),
      compiler_params=_params(1),
      name="ffn_attn" if has_wo else "ffn_pool",
  )(*args)


def _rope_tables():
  pos = jnp.arange(LAT_T)
  row = (pos // GRID_W).astype(F32)
  col = (pos % GRID_W).astype(F32)
  inv = ROPE_BASE ** (-jnp.arange(ROPE_PAIRS, dtype=F32) / ROPE_PAIRS)
  ar = row[:, None] * inv
  ac = col[:, None] * inv
  cos = jnp.concatenate([jnp.cos(ar), jnp.cos(ar), jnp.cos(ac), jnp.cos(ac)], axis=1)
  sin = jnp.concatenate([-jnp.sin(ar), jnp.sin(ar), -jnp.sin(ac), jnp.sin(ac)], axis=1)
  return jnp.concatenate([cos, sin], axis=1)


def kernel(x_prompt, x_sample, c, cache_k, cache_v, c_ctx, w_ada, b_ada, norm_gains,
           w_qkv, qk_gains, w_o, w_pool, pool_scale, w_up, w_down):
  xs = [x_prompt.reshape(N_CTX_TOK, D), x_sample.reshape(N_LAT_TOK, D)]
  cond = jnp.concatenate(
      [c_ctx[None, :], c, jnp.zeros((N_COND - 1 - N_LAT_SEQ, D), F32)], axis=0)
  ada = _modulation(cond, w_ada, b_ada)
  rope_cs = _rope_tables()
  w_qkv_b = w_qkv.astype(BF16)
  w_o_b = w_o.astype(BF16)
  w_pool_b = w_pool.astype(BF16)
  w_up_b = w_up.astype(BF16)
  w_down_b = w_down.astype(BF16)

  new_kv = None
  for layer in range(DEPTH):
    last = layer == DEPTH - 1
    if layer % 2 == 0:
      q, k, v, *new_kv = _qkv(xs, ada, norm_gains, w_qkv_b, qk_gains, rope_cs, new_kv, layer)
      mix = _attn_latent(q, k, v, cache_k, cache_v, _attn_context(q, k, v), layer // 2)
      xs = _ffn(xs, mix, ada, norm_gains, w_o_b, w_up_b, w_down_b, layer, last)
    else:
      mix = _pool(xs[0], ada, norm_gains, w_pool_b, pool_scale, layer)
      xs = _ffn(xs, mix, ada, norm_gains, None, w_up_b, w_down_b, layer, last)

  out_kv = (N_CTX_SEQ, N_ATTN, CTX_T, N_KV, HD)
  return (xs[0].reshape(N_CTX_SEQ, CTX_T, D), xs[1].reshape(N_LAT_SEQ, LAT_T, D),
          new_kv[0].reshape(out_kv), new_kv[1].reshape(out_kv))
```

```python
import functools
import math

import jax
import jax.numpy as jnp
from jax import lax
from jax.experimental import pallas as pl
from jax.experimental.pallas import tpu as pltpu

F32 = jnp.float32
BF16 = jnp.bfloat16

D = 1024
N_CTX_SEQ = 32
CTX_T = 256
N_LAT_SEQ = 4
LAT_T = 2048
PAST = 512
DEPTH = 4
N_ATTN = 2
GRID_W = 64
HD = 128
N_HEADS = 8
N_KV = 2
GQA = N_HEADS // N_KV
Q_W = N_HEADS * HD
KV_W = N_KV * HD
QKV_W = Q_W + 2 * KV_W
ROPE_PAIRS = HD // 4
ROPE_BASE = 10000.0
POOL_WINDOWS = (2, 4, 8, 16)
N_POOL_G = len(POOL_WINDOWS)
POOL_DG = D // N_POOL_G
D_FF = 2816
N_MOD = 6
EPS = 1e-6

N_CTX_TOK = N_CTX_SEQ * CTX_T
N_LAT_TOK = N_LAT_SEQ * LAT_T
N_TOK = N_CTX_TOK + N_LAT_TOK
N_COND = 8
Q_SCALE = math.log2(math.e) / math.sqrt(HD)

VMEM_LIMIT = 56 * 1024 * 1024

BM = 512
N_CTX_BLK = N_CTX_TOK // BM
N_LAT_BLK = N_LAT_TOK // BM
N_BLK = N_CTX_BLK + N_LAT_BLK
BLK_PER_LAT_SEQ = LAT_T // BM
FFN_GROUPS = 2
POOL_TILE = CTX_T
POOL_HALO = max(POOL_WINDOWS) // 2
ATT_TQ = 512
ATT_CHUNK = 256
CTX_NB = 2


def _token_specs(split, width):
  if not split:
    return [pl.BlockSpec((BM, width), lambda i: (i, 0))]
  return [pl.BlockSpec((BM, width), lambda i: (jnp.minimum(i, N_CTX_BLK - 1), 0)),
          pl.BlockSpec((BM, width), lambda i: (jnp.maximum(i - N_CTX_BLK, 0), 0))]


def _token_shapes(split, width, dtype):
  if not split:
    return [jax.ShapeDtypeStruct((N_TOK, width), dtype)]
  return [jax.ShapeDtypeStruct((N_CTX_TOK, width), dtype),
          jax.ShapeDtypeStruct((N_LAT_TOK, width), dtype)]


def _token_load(is_ctx, refs, rows=slice(None)):
  if len(refs) == 1:
    return refs[0][rows, :]
  return jnp.where(is_ctx, refs[0][rows, :], refs[1][rows, :])


def _token_store(is_ctx, refs, value, rows=slice(None)):
  if len(refs) == 1:
    refs[0][rows, :] = value
    return

  @pl.when(is_ctx)
  def _():
    refs[0][rows, :] = value

  @pl.when(jnp.logical_not(is_ctx))
  def _():
    refs[1][rows, :] = value


def _cond_row(i):
  return jnp.where(i < N_CTX_BLK, 0, 1 + (i - N_CTX_BLK) // BLK_PER_LAT_SEQ)


def _mod_rows(ada_ref, row):
  return [ada_ref[0, m, pl.ds(row, 1), :] for m in range(N_MOD)]


def _layer_spec(shape, layer):
  zeros = (0,) * len(shape)
  return pl.BlockSpec((1,) + shape, lambda *_: (layer,) + zeros, pipeline_mode=pl.Buffered(1))


def _rms_scale(x):
  return x * lax.rsqrt(jnp.mean(x * x, axis=-1, keepdims=True) + EPS)


def _silu(x):
  return x / (1.0 + jnp.exp(-x))


def _params(n_grid_dims):
  return pltpu.CompilerParams(
      dimension_semantics=("arbitrary",) * n_grid_dims, vmem_limit_bytes=VMEM_LIMIT)


def _mod_kernel(cond_ref, w_ref, b_ref, o_ref):
  s = _silu(cond_ref[...]).astype(BF16)
  o_ref[0, 0] = jnp.dot(s, w_ref[0].astype(BF16), preferred_element_type=F32) + b_ref[0, 0]


def _modulation(cond, w_ada, b_ada):
  return pl.pallas_call(
      _mod_kernel,
      grid=(DEPTH, N_MOD),
      in_specs=[
          pl.BlockSpec((N_COND, D), lambda l, m: (0, 0)),
          pl.BlockSpec((1, D, D), lambda l, m: (l, 0, m)),
          pl.BlockSpec((1, 1, 1, D), lambda l, m: (l, m, 0, 0)),
      ],
      out_specs=pl.BlockSpec((1, 1, N_COND, D), lambda l, m: (l, m, 0, 0)),
      out_shape=jax.ShapeDtypeStruct((DEPTH, N_MOD, N_COND, D), F32),
      compiler_params=_params(2),
      name="modulation",
  )(cond, w_ada, b_ada.reshape(DEPTH, N_MOD, 1, D))


def _swap_halves(x):
  lane = lax.broadcasted_iota(jnp.int32, x.shape, 1)
  first = (lane % (2 * ROPE_PAIRS)) < ROPE_PAIRS
  return jnp.where(first, pltpu.roll(x, HD - ROPE_PAIRS, 1), pltpu.roll(x, ROPE_PAIRS, 1))


def _qkv_kernel(*refs, n_x):
  x_refs, refs = refs[:n_x], refs[n_x:]
  ada_ref, g_ref, w_ref, qkg_ref, cs_ref = refs[:5]
  q_ref, k_ref, v_ref, nk_ref, nv_ref = refs[-5:]
  i = pl.program_id(0)
  is_ctx = i < N_CTX_BLK
  shift, scale = _mod_rows(ada_ref, _cond_row(i))[:2]
  gq = qkg_ref[0, 0:1] * Q_SCALE
  gk = qkg_ref[0, 1:2]
  groups = [slice(s * CTX_T, (s + 1) * CTX_T) for s in range(BM // CTX_T)]
  h = [((_rms_scale(_token_load(is_ctx, x_refs, rows)) * g_ref[0, 0:1]) * (1.0 + scale)
        + shift).astype(BF16) for rows in groups]
  qkv = [jnp.dot(h_s, w_ref[0], preferred_element_type=F32) for h_s in h]
  new = []
  for rows, qkv_s in zip(groups, qkv):
    cos = cs_ref[rows, :HD]
    sin = cs_ref[rows, HD:]
    rope = lambda x: x * cos + _swap_halves(x) * sin
    for hd in range(N_HEADS):
      cols = slice(hd * HD, (hd + 1) * HD)
      q_ref[rows, cols] = rope(_rms_scale(qkv_s[:, cols]) * gq).astype(BF16)
    k = [rope(_rms_scale(qkv_s[:, Q_W + hd * HD:Q_W + (hd + 1) * HD]) * gk)
         for hd in range(N_KV)]
    for hd in range(N_KV):
      k_ref[rows, hd * HD:(hd + 1) * HD] = k[hd].astype(BF16)
    v = qkv_s[:, Q_W + KV_W:]
    v_ref[rows, :] = v.astype(BF16)
    new.append((k, v))

  @pl.when(is_ctx)
  def _():
    for s, (k, v) in enumerate(new):
      nv_ref[s, 0] = v
      for hd in range(N_KV):
        nk_ref[s, 0, :, hd * HD:(hd + 1) * HD] = k[hd]


def _qkv(xs, ada, gains, w_qkv, qk_gains, rope_cs, new_kv, layer):
  a = layer // 2
  seqs = BM // CTX_T
  new_spec = pl.BlockSpec(
      (seqs, 1, CTX_T, KV_W), lambda i: (jnp.minimum(i, N_CTX_BLK - 1), a, 0, 0))
  new_shape = jax.ShapeDtypeStruct((N_CTX_SEQ, N_ATTN, CTX_T, KV_W), F32)
  args = list(xs) + [ada, gains, w_qkv, qk_gains, rope_cs]
  in_specs = _token_specs(len(xs) == 2, D) + [
      _layer_spec((N_MOD, N_COND, D), layer),
      _layer_spec((4, D), layer),
      _layer_spec((D, QKV_W), a),
      _layer_spec((2, HD), a),
      pl.BlockSpec((BM, 2 * HD), lambda i: (
          jnp.where(i < N_CTX_BLK, 0, 1 + (i - N_CTX_BLK) % BLK_PER_LAT_SEQ), 0)),
  ]
  aliases = {}
  if new_kv is not None:
    aliases = {len(args): 3, len(args) + 1: 4}
    args += list(new_kv)
    in_specs += [pl.BlockSpec(memory_space=pl.ANY)] * 2
  return pl.pallas_call(
      functools.partial(_qkv_kernel, n_x=len(xs)),
      grid=(N_BLK,),
      in_specs=in_specs,
      out_specs=(_token_specs(False, Q_W) + _token_specs(False, KV_W) * 2 + [new_spec] * 2),
      out_shape=(_token_shapes(False, Q_W, BF16) + _token_shapes(False, KV_W, BF16) * 2
                 + [new_shape] * 2),
      input_output_aliases=aliases,
      compiler_params=_params(1),
      name="qkv",
  )(*args)


def _nt_dot(a, b):
  return lax.dot_general(a, b, (((1,), (1,)), ((), ())), preferred_element_type=F32)


def _with_ones(v):
  return jnp.concatenate([v, jnp.ones_like(v)], axis=1)


def _softmax_pv(s, v1):
  p = jnp.exp2(s - jnp.max(s, axis=-1, keepdims=True)).astype(BF16)
  oa = jnp.dot(p, v1, preferred_element_type=F32)
  return oa[:, :HD] / oa[:, HD:]


def _attn_ctx_kernel(q_ref, k_ref, v_ref, o_ref):
  for b in range(CTX_NB):
    rows = slice(b * CTX_T, (b + 1) * CTX_T)
    for h in range(N_KV):
      kv_cols = slice(h * HD, (h + 1) * HD)
      v1 = _with_ones(v_ref[rows, kv_cols])
      for g in range(GQA):
        cols = slice((h * GQA + g) * HD, (h * GQA + g + 1) * HD)
        s = _nt_dot(q_ref[rows, cols], k_ref[rows, kv_cols])
        o_ref[rows, cols] = _softmax_pv(s, v1).astype(BF16)


def _attn_context(q, k, v):
  bm = CTX_NB * CTX_T
  row = lambda i: (i, 0)
  return pl.pallas_call(
      _attn_ctx_kernel,
      grid=(N_CTX_TOK // bm,),
      in_specs=[
          pl.BlockSpec((bm, Q_W), row),
          pl.BlockSpec((bm, KV_W), row),
          pl.BlockSpec((bm, KV_W), row),
      ],
      out_specs=pl.BlockSpec((bm, Q_W), row),
      out_shape=jax.ShapeDtypeStruct((N_TOK, Q_W), BF16),
      compiler_params=_params(1),
      name="attn_context",
  )(q, k, v)


def _attn_lat_kernel(q_ref, k_ref, v_ref, ck_ref, cv_ref, mix_in, o_ref, kcat, vcat):
  del mix_in

  @pl.when(pl.program_id(2) == 0)
  def _():
    kcat[:LAT_T] = k_ref[...]
    kcat[LAT_T:] = ck_ref[0, 0].astype(BF16)
    vcat[:LAT_T] = _with_ones(v_ref[...])
    vcat[LAT_T:] = _with_ones(cv_ref[0, 0].astype(BF16))

  for g in range(GQA):
    cols = slice(g * HD, (g + 1) * HD)
    for r in range(ATT_TQ // ATT_CHUNK):
      rows = slice(r * ATT_CHUNK, (r + 1) * ATT_CHUNK)
      s = _nt_dot(q_ref[rows, cols], kcat[...])
      o_ref[rows, cols] = _softmax_pv(s, vcat[...]).astype(BF16)


def _attn_latent(q, k, v, cache_k, cache_v, mix, a):
  nq = LAT_T // ATT_TQ
  q0 = N_CTX_TOK // ATT_TQ
  kv0 = N_CTX_TOK // LAT_T
  ck = cache_k.reshape(N_LAT_SEQ, N_ATTN, PAST, KV_W)
  cv = cache_v.reshape(N_LAT_SEQ, N_ATTN, PAST, KV_W)
  return pl.pallas_call(
      _attn_lat_kernel,
      grid=(N_LAT_SEQ, N_KV, nq),
      in_specs=[
          pl.BlockSpec((ATT_TQ, GQA * HD), lambda b, h, j: (q0 + b * nq + j, h)),
          pl.BlockSpec((LAT_T, HD), lambda b, h, j: (kv0 + b, h)),
          pl.BlockSpec((LAT_T, HD), lambda b, h, j: (kv0 + b, h)),
          pl.BlockSpec((1, 1, PAST, HD), lambda b, h, j: (b, a, 0, h)),
          pl.BlockSpec((1, 1, PAST, HD), lambda b, h, j: (b, a, 0, h)),
          pl.BlockSpec(memory_space=pl.ANY),
      ],
      out_specs=pl.BlockSpec((ATT_TQ, GQA * HD), lambda b, h, j: (q0 + b * nq + j, h)),
      out_shape=jax.ShapeDtypeStruct((N_TOK, Q_W), BF16),
      scratch_shapes=[pltpu.VMEM((LAT_T + PAST, HD), BF16),
                      pltpu.VMEM((LAT_T + PAST, 2 * HD), BF16)],
      input_output_aliases={5: 0},
      compiler_params=_params(3),
      name="attn_latent",
  )(q, k, v, ck, cv, mix)


def _pool_kernel(x_ref, xprev_ref, xnext_ref, ada_ref, g_ref, w_ref, ps_ref, o_ref):
  i = pl.program_id(0)
  is_ctx = i < N_CTX_BLK
  shift, scale = _mod_rows(ada_ref, _cond_row(i))[:2]
  pre = lambda x: (_rms_scale(x) * g_ref[0, 0:1]) * (1.0 + scale) + shift
  h = pre(x_ref[...])
  seq_pos = jnp.maximum(i - N_CTX_BLK, 0) % BLK_PER_LAT_SEQ
  is_first = jnp.logical_or(is_ctx, seq_pos == 0)
  is_last = jnp.logical_or(is_ctx, seq_pos == BLK_PER_LAT_SEQ - 1)
  ctx_edge = jnp.where(is_ctx, 1.0, 0.0)
  first_edge = jnp.where(is_first, 1.0, 0.0)
  last_edge = jnp.where(is_last, 1.0, 0.0)
  h_prev = jnp.where(is_first, 0.0, pre(xprev_ref[...]))
  h_next = jnp.where(is_last, 0.0, pre(xnext_ref[...]))
  n_tiles = BM // POOL_TILE
  ext_rows = POOL_TILE + 2 * POOL_HALO
  row = lax.broadcasted_iota(jnp.int32, (POOL_TILE, POOL_DG), 0)

  for g, w in enumerate(POOL_WINDOWS):
    cols = slice(g * POOL_DG, (g + 1) * POOL_DG)
    lo_cut = jnp.maximum(w // 2 - row, 0).astype(F32)
    hi_cut = jnp.maximum(row + (w - w // 2) - POOL_TILE, 0).astype(F32)
    for j in range(n_tiles):
      r0 = j * POOL_TILE
      starts = first_edge if j == 0 else ctx_edge
      ends = last_edge if j == n_tiles - 1 else ctx_edge
      hg = h[r0:r0 + POOL_TILE, cols]
      before = (h_prev[:, cols] if j == 0
                else jnp.where(is_ctx, 0.0, h[r0 - POOL_HALO:r0, cols]))
      after = (h_next[:, cols] if j == n_tiles - 1
               else jnp.where(is_ctx, 0.0, h[r0 + POOL_TILE:r0 + POOL_TILE + POOL_HALO, cols]))
      ext = jnp.concatenate([before, hg, after], axis=0)
      back, fwd, m = ext, ext, 1
      while m < w // 2:
        back = back + pltpu.roll(back, m, 0)
        fwd = fwd + pltpu.roll(fwd, ext_rows - m, 0)
        m *= 2
      win = (pltpu.roll(back, 1, 0) + fwd)[POOL_HALO:POOL_HALO + POOL_TILE]
      cnt = w - lo_cut * starts - hi_cut * ends
      pooled = win / cnt - hg
      og = jnp.dot(pooled.astype(BF16), w_ref[0, g], preferred_element_type=F32)
      o_ref[r0:r0 + POOL_TILE, cols] = (og * ps_ref[0, :, cols]).astype(BF16)


def _pool(x, ada, gains, w_pool, pool_scale, layer):
  p = layer // 2
  halo_per_blk = BM // POOL_HALO
  n_halo = N_TOK // POOL_HALO
  prev_spec = pl.BlockSpec((POOL_HALO, D), lambda i: (jnp.maximum(i * halo_per_blk - 1, 0), 0))
  next_spec = pl.BlockSpec(
      (POOL_HALO, D), lambda i: (jnp.minimum((i + 1) * halo_per_blk, n_halo - 1), 0))
  return pl.pallas_call(
      _pool_kernel,
      grid=(N_BLK,),
      in_specs=_token_specs(False, D) + [
          prev_spec, next_spec,
          _layer_spec((N_MOD, N_COND, D), layer),
          _layer_spec((4, D), layer),
          _layer_spec((N_POOL_G, POOL_DG, POOL_DG), p),
          _layer_spec((1, D), p),
      ],
      out_specs=_token_specs(False, D)[0],
      out_shape=_token_shapes(False, D, BF16)[0],
      compiler_params=_params(1),
      name="pool",
  )(x, x, x, ada, gains, w_pool, pool_scale.reshape(-1, 1, D))


def _ffn_kernel(*refs, n_x, n_out, has_wo):
  x_refs, refs = refs[:n_x], refs[n_x:]
  o_refs, refs = refs[-n_out:], refs[:-n_out]
  if has_wo:
    mix_ref, ada_ref, g_ref, wo_ref, wup_ref, wdn_ref = refs
  else:
    mix_ref, ada_ref, g_ref, wup_ref, wdn_ref = refs
  i = pl.program_id(0)
  is_ctx = i < N_CTX_BLK
  _, _, gate_m, shift_f, scale_f, gate_f = _mod_rows(ada_ref, _cond_row(i))
  groups = [slice(r * (BM // FFN_GROUPS), (r + 1) * (BM // FFN_GROUPS))
            for r in range(FFN_GROUPS)]
  if has_wo:
    a = [jnp.dot(mix_ref[rows, :], wo_ref[0], preferred_element_type=F32) for rows in groups]
  else:
    a = [mix_ref[rows, :].astype(F32) for rows in groups]
  x1 = [_token_load(is_ctx, x_refs, rows) + gate_m * (_rms_scale(a_r) * g_ref[0, 1:2])
        for rows, a_r in zip(groups, a)]
  h = [((_rms_scale(x1_r) * g_ref[0, 2:3]) * (1.0 + scale_f) + shift_f).astype(BF16)
       for x1_r in x1]
  gu = [jnp.dot(h_r, wup_ref[0], preferred_element_type=F32) for h_r in h]
  act = [(_silu(gu_r[:, :D_FF]) * gu_r[:, D_FF:]).astype(BF16) for gu_r in gu]
  y = [jnp.dot(act_r, wdn_ref[0], preferred_element_type=F32) for act_r in act]
  for rows, x1_r, y_r in zip(groups, x1, y):
    _token_store(is_ctx, o_refs, x1_r + gate_f * (_rms_scale(y_r) * g_ref[0, 3:4]), rows)


def _ffn(xs, mix, ada, gains, w_o, w_up, w_down, layer, split_out):
  has_wo = w_o is not None
  in_specs = _token_specs(len(xs) == 2, D) + _token_specs(False, D) + [
      _layer_spec((N_MOD, N_COND, D), layer),
      _layer_spec((4, D), layer),
  ]
  args = list(xs) + [mix, ada, gains]
  if has_wo:
    in_specs.append(_layer_spec((Q_W, D), layer // 2))
    args.append(w_o)
  in_specs += [_layer_spec((D, 2 * D_FF), layer), _layer_spec((D_FF, D), layer)]
  args += [w_up, w_down]
  return pl.pallas_call(
      functools.partial(_ffn_kernel, n_x=len(xs), n_out=2 if split_out else 1, has_wo=has_wo),
      grid=(N_BLK,),
      in_specs=in_specs,
      out_specs=_token_specs(split_out, D),
      out_shape=_token_shapes(split_out, D, F32),
      compiler_params=_params(1),
      name="ffn_attn" if has_wo else "ffn_pool",
  )(*args)


def _rope_tables():
  pos = jnp.arange(LAT_T)
  row = (pos // GRID_W).astype(F32)
  col = (pos % GRID_W).astype(F32)
  inv = ROPE_BASE ** (-jnp.arange(ROPE_PAIRS, dtype=F32) / ROPE_PAIRS)
  ar = row[:, None] * inv
  ac = col[:, None] * inv
  cos = jnp.concatenate([jnp.cos(ar), jnp.cos(ar), jnp.cos(ac), jnp.cos(ac)], axis=1)
  sin = jnp.concatenate([-jnp.sin(ar), jnp.sin(ar), -jnp.sin(ac), jnp.sin(ac)], axis=1)
  identity = jnp.concatenate([jnp.ones((BM, HD), F32), jnp.zeros((BM, HD), F32)], axis=1)
  return jnp.concatenate([identity, jnp.concatenate([cos, sin], axis=1)], axis=0)


def kernel(x_prompt, x_sample, c, cache_k, cache_v, c_ctx, w_ada, b_ada, norm_gains,
           w_qkv, qk_gains, w_o, w_pool, pool_scale, w_up, w_down):
  xs = [x_prompt.reshape(N_CTX_TOK, D), x_sample.reshape(N_LAT_TOK, D)]
  cond = jnp.concatenate(
      [c_ctx[None, :], c, jnp.zeros((N_COND - 1 - N_LAT_SEQ, D), F32)], axis=0)
  ada = _modulation(cond, w_ada, b_ada)
  rope_cs = _rope_tables()
  w_qkv_b = w_qkv.astype(BF16)
  w_o_b = w_o.astype(BF16)
  w_pool_b = w_pool.astype(BF16)
  w_up_b = w_up.astype(BF16)
  w_down_b = w_down.astype(BF16)

  new_kv = None
  for layer in range(DEPTH):
    last = layer == DEPTH - 1
    if layer % 2 == 0:
      q, k, v, *new_kv = _qkv(xs, ada, norm_gains, w_qkv_b, qk_gains, rope_cs, new_kv, layer)
      mix = _attn_latent(q, k, v, cache_k, cache_v, _attn_context(q, k, v), layer // 2)
      xs = _ffn(xs, mix, ada, norm_gains, w_o_b, w_up_b, w_down_b, layer, last)
    else:
      mix = _pool(xs[0], ada, norm_gains, w_pool_b, pool_scale, layer)
      xs = _ffn(xs, mix, ada, norm_gains, None, w_up_b, w_down_b, layer, last)

  out_kv = (N_CTX_SEQ, N_ATTN, CTX_T, N_KV, HD)
  return (xs[0].reshape(N_CTX_SEQ, CTX_T, D), xs[1].reshape(N_LAT_SEQ, LAT_T, D),
          new_kv[0].reshape(out_kv), new_kv[1].reshape(out_kv))
```

```python
import functools
import math

import jax
import jax.numpy as jnp
from jax import lax
from jax.experimental import pallas as pl
from jax.experimental.pallas import tpu as pltpu

F32 = jnp.float32
BF16 = jnp.bfloat16

D = 1024
N_CTX_SEQ = 32
CTX_T = 256
N_LAT_SEQ = 4
LAT_T = 2048
PAST = 512
DEPTH = 4
N_ATTN = 2
GRID_W = 64
HD = 128
N_HEADS = 8
N_KV = 2
GQA = N_HEADS // N_KV
Q_W = N_HEADS * HD
KV_W = N_KV * HD
QKV_W = Q_W + 2 * KV_W
ROPE_PAIRS = HD // 4
ROPE_BASE = 10000.0
POOL_WINDOWS = (2, 4, 8, 16)
N_POOL_G = len(POOL_WINDOWS)
POOL_DG = D // N_POOL_G
D_FF = 2816
N_MOD = 6
EPS = 1e-6

N_CTX_TOK = N_CTX_SEQ * CTX_T
N_LAT_TOK = N_LAT_SEQ * LAT_T
N_TOK = N_CTX_TOK + N_LAT_TOK
N_COND = 8
Q_SCALE = math.log2(math.e) / math.sqrt(HD)

VMEM_LIMIT = 56 * 1024 * 1024

BM = 512
N_CTX_BLK = N_CTX_TOK // BM
N_LAT_BLK = N_LAT_TOK // BM
N_BLK = N_CTX_BLK + N_LAT_BLK
BLK_PER_LAT_SEQ = LAT_T // BM
FFN_GROUPS = 2
POOL_TILE = CTX_T
POOL_HALO = max(POOL_WINDOWS) // 2
ATT_TQ = 512
ATT_CHUNK = 256
CTX_NB = 2


def _token_specs(split, width):
  if not split:
    return [pl.BlockSpec((BM, width), lambda i: (i, 0))]
  return [pl.BlockSpec((BM, width), lambda i: (jnp.minimum(i, N_CTX_BLK - 1), 0)),
          pl.BlockSpec((BM, width), lambda i: (jnp.maximum(i - N_CTX_BLK, 0), 0))]


def _token_shapes(split, width, dtype):
  if not split:
    return [jax.ShapeDtypeStruct((N_TOK, width), dtype)]
  return [jax.ShapeDtypeStruct((N_CTX_TOK, width), dtype),
          jax.ShapeDtypeStruct((N_LAT_TOK, width), dtype)]


def _token_load(is_ctx, refs, rows=slice(None)):
  if len(refs) == 1:
    return refs[0][rows, :]
  return jnp.where(is_ctx, refs[0][rows, :], refs[1][rows, :])


def _token_store(is_ctx, refs, value, rows=slice(None)):
  if len(refs) == 1:
    refs[0][rows, :] = value
    return

  @pl.when(is_ctx)
  def _():
    refs[0][rows, :] = value

  @pl.when(jnp.logical_not(is_ctx))
  def _():
    refs[1][rows, :] = value


def _cond_row(i):
  return jnp.where(i < N_CTX_BLK, 0, 1 + (i - N_CTX_BLK) // BLK_PER_LAT_SEQ)


def _mod_rows(ada_ref, row):
  return [ada_ref[0, m, pl.ds(row, 1), :] for m in range(N_MOD)]


def _layer_spec(shape, layer):
  zeros = (0,) * len(shape)
  return pl.BlockSpec((1,) + shape, lambda *_: (layer,) + zeros, pipeline_mode=pl.Buffered(1))


def _rms_scale(x):
  return x * lax.rsqrt(jnp.mean(x * x, axis=-1, keepdims=True) + EPS)


def _silu(x):
  return x / (1.0 + jnp.exp(-x))


def _params(n_grid_dims):
  return pltpu.CompilerParams(
      dimension_semantics=("arbitrary",) * n_grid_dims, vmem_limit_bytes=VMEM_LIMIT)


def _mod_kernel(cond_ref, w_ref, b_ref, o_ref):
  s = _silu(cond_ref[...]).astype(BF16)
  o_ref[0, 0] = jnp.dot(s, w_ref[0].astype(BF16), preferred_element_type=F32) + b_ref[0, 0]


def _modulation(cond, w_ada, b_ada):
  return pl.pallas_call(
      _mod_kernel,
      grid=(DEPTH, N_MOD),
      in_specs=[
          pl.BlockSpec((N_COND, D), lambda l, m: (0, 0)),
          pl.BlockSpec((1, D, D), lambda l, m: (l, 0, m)),
          pl.BlockSpec((1, 1, 1, D), lambda l, m: (l, m, 0, 0)),
      ],
      out_specs=pl.BlockSpec((1, 1, N_COND, D), lambda l, m: (l, m, 0, 0)),
      out_shape=jax.ShapeDtypeStruct((DEPTH, N_MOD, N_COND, D), F32),
      compiler_params=_params(2),
      name="modulation",
  )(cond, w_ada, b_ada.reshape(DEPTH, N_MOD, 1, D))


HEAD_PAIRED_ORDER = tuple(
    list(range(0, ROPE_PAIRS)) + list(range(2 * ROPE_PAIRS, 3 * ROPE_PAIRS))
    + list(range(ROPE_PAIRS, 2 * ROPE_PAIRS)) + list(range(3 * ROPE_PAIRS, HD)))


def _swap_middle(x):
  blk = lax.broadcasted_iota(jnp.int32, x.shape, 1) // ROPE_PAIRS
  return jnp.where(blk == 1, pltpu.roll(x, HD - ROPE_PAIRS, 1),
                   jnp.where(blk == 2, pltpu.roll(x, ROPE_PAIRS, 1), x))


def _qkv_kernel(*refs, n_x):
  x_refs, refs = refs[:n_x], refs[n_x:]
  ada_ref, g_ref, w_ref, qkg_ref, cs_ref = refs[:5]
  q_ref, k_ref, v_ref, nk_ref, nv_ref = refs[-5:]
  i = pl.program_id(0)
  is_ctx = i < N_CTX_BLK
  shift, scale = _mod_rows(ada_ref, _cond_row(i))[:2]
  gq = qkg_ref[0, 0:1] * Q_SCALE
  gk = qkg_ref[0, 1:2]
  groups = [slice(s * CTX_T, (s + 1) * CTX_T) for s in range(BM // CTX_T)]

  def body(x_ref, latent):
    h = [((_rms_scale(x_ref[rows, :]) * g_ref[0, 0:1]) * (1.0 + scale) + shift).astype(BF16)
         for rows in groups]
    qkv = [jnp.dot(h_s, w_ref[0], preferred_element_type=F32) for h_s in h]
    for s, (rows, qkv_s) in enumerate(zip(groups, qkv)):
      if latent:
        cos = cs_ref[rows, :HD]
        sin = cs_ref[rows, HD:]
        rope = lambda x: x * cos + pltpu.roll(x, HD // 2, 1) * sin
      else:
        rope = lambda x: x
      for hd in range(N_HEADS):
        cols = slice(hd * HD, (hd + 1) * HD)
        q_ref[rows, cols] = rope(_rms_scale(qkv_s[:, cols]) * gq).astype(BF16)
      v = qkv_s[:, Q_W + KV_W:]
      v_ref[rows, :] = v.astype(BF16)
      for hd in range(N_KV):
        cols = slice(hd * HD, (hd + 1) * HD)
        k = rope(_rms_scale(qkv_s[:, Q_W + hd * HD:Q_W + (hd + 1) * HD]) * gk)
        k_ref[rows, cols] = k.astype(BF16)
        if not latent:
          nk_ref[s, 0, :, cols] = _swap_middle(k)
      if not latent:
        nv_ref[s, 0] = v

  @pl.when(is_ctx)
  def _():
    body(x_refs[0], latent=False)

  @pl.when(jnp.logical_not(is_ctx))
  def _():
    body(x_refs[-1], latent=True)


def _qkv(xs, ada, gains, w_qkv, qk_gains, rope_cs, new_kv, layer):
  a = layer // 2
  seqs = BM // CTX_T
  new_spec = pl.BlockSpec(
      (seqs, 1, CTX_T, KV_W), lambda i: (jnp.minimum(i, N_CTX_BLK - 1), a, 0, 0))
  new_shape = jax.ShapeDtypeStruct((N_CTX_SEQ, N_ATTN, CTX_T, KV_W), F32)
  args = list(xs) + [ada, gains, w_qkv, qk_gains, rope_cs]
  in_specs = _token_specs(len(xs) == 2, D) + [
      _layer_spec((N_MOD, N_COND, D), layer),
      _layer_spec((4, D), layer),
      _layer_spec((D, QKV_W), a),
      _layer_spec((2, HD), a),
      pl.BlockSpec(
          (BM, 2 * HD), lambda i: (jnp.maximum(i - N_CTX_BLK, 0) % BLK_PER_LAT_SEQ, 0)),
  ]
  aliases = {}
  if new_kv is not None:
    aliases = {len(args): 3, len(args) + 1: 4}
    args += list(new_kv)
    in_specs += [pl.BlockSpec(memory_space=pl.ANY)] * 2
  return pl.pallas_call(
      functools.partial(_qkv_kernel, n_x=len(xs)),
      grid=(N_BLK,),
      in_specs=in_specs,
      out_specs=(_token_specs(False, Q_W) + _token_specs(False, KV_W) * 2 + [new_spec] * 2),
      out_shape=(_token_shapes(False, Q_W, BF16) + _token_shapes(False, KV_W, BF16) * 2
                 + [new_shape] * 2),
      input_output_aliases=aliases,
      compiler_params=_params(1),
      name="qkv",
  )(*args)


def _nt_dot(a, b):
  return lax.dot_general(a, b, (((1,), (1,)), ((), ())), preferred_element_type=F32)


def _with_ones(v):
  return jnp.concatenate([v, jnp.ones_like(v)], axis=1)


def _softmax_pv(s, v1):
  p = jnp.exp2(s - jnp.max(s, axis=-1, keepdims=True)).astype(BF16)
  oa = jnp.dot(p, v1, preferred_element_type=F32)
  return oa[:, :HD] / oa[:, HD:]


def _attn_ctx_kernel(q_ref, k_ref, v_ref, o_ref):
  for b in range(CTX_NB):
    rows = slice(b * CTX_T, (b + 1) * CTX_T)
    for h in range(N_KV):
      kv_cols = slice(h * HD, (h + 1) * HD)
      v1 = _with_ones(v_ref[rows, kv_cols])
      for g in range(GQA):
        cols = slice((h * GQA + g) * HD, (h * GQA + g + 1) * HD)
        s = _nt_dot(q_ref[rows, cols], k_ref[rows, kv_cols])
        o_ref[rows, cols] = _softmax_pv(s, v1).astype(BF16)


def _attn_context(q, k, v):
  bm = CTX_NB * CTX_T
  row = lambda i: (i, 0)
  return pl.pallas_call(
      _attn_ctx_kernel,
      grid=(N_CTX_TOK // bm,),
      in_specs=[
          pl.BlockSpec((bm, Q_W), row),
          pl.BlockSpec((bm, KV_W), row),
          pl.BlockSpec((bm, KV_W), row),
      ],
      out_specs=pl.BlockSpec((bm, Q_W), row),
      out_shape=jax.ShapeDtypeStruct((N_TOK, Q_W), BF16),
      compiler_params=_params(1),
      name="attn_context",
  )(q, k, v)


def _attn_lat_kernel(q_ref, k_ref, v_ref, ck_ref, cv_ref, mix_in, o_ref, kcat, vcat):
  del mix_in

  @pl.when(pl.program_id(2) == 0)
  def _():
    kcat[:LAT_T] = k_ref[...]
    kcat[LAT_T:] = _swap_middle(ck_ref[0, 0]).astype(BF16)
    vcat[:LAT_T] = _with_ones(v_ref[...])
    vcat[LAT_T:] = _with_ones(cv_ref[0, 0].astype(BF16))

  for g in range(GQA):
    cols = slice(g * HD, (g + 1) * HD)
    for r in range(ATT_TQ // ATT_CHUNK):
      rows = slice(r * ATT_CHUNK, (r + 1) * ATT_CHUNK)
      s = _nt_dot(q_ref[rows, cols], kcat[...])
      o_ref[rows, cols] = _softmax_pv(s, vcat[...]).astype(BF16)


def _attn_latent(q, k, v, cache_k, cache_v, mix, a):
  nq = LAT_T // ATT_TQ
  q0 = N_CTX_TOK // ATT_TQ
  kv0 = N_CTX_TOK // LAT_T
  ck = cache_k.reshape(N_LAT_SEQ, N_ATTN, PAST, KV_W)
  cv = cache_v.reshape(N_LAT_SEQ, N_ATTN, PAST, KV_W)
  return pl.pallas_call(
      _attn_lat_kernel,
      grid=(N_LAT_SEQ, N_KV, nq),
      in_specs=[
          pl.BlockSpec((ATT_TQ, GQA * HD), lambda b, h, j: (q0 + b * nq + j, h)),
          pl.BlockSpec((LAT_T, HD), lambda b, h, j: (kv0 + b, h)),
          pl.BlockSpec((LAT_T, HD), lambda b, h, j: (kv0 + b, h)),
          pl.BlockSpec((1, 1, PAST, HD), lambda b, h, j: (b, a, 0, h)),
          pl.BlockSpec((1, 1, PAST, HD), lambda b, h, j: (b, a, 0, h)),
          pl.BlockSpec(memory_space=pl.ANY),
      ],
      out_specs=pl.BlockSpec((ATT_TQ, GQA * HD), lambda b, h, j: (q0 + b * nq + j, h)),
      out_shape=jax.ShapeDtypeStruct((N_TOK, Q_W), BF16),
      scratch_shapes=[pltpu.VMEM((LAT_T + PAST, HD), BF16),
                      pltpu.VMEM((LAT_T + PAST, 2 * HD), BF16)],
      input_output_aliases={5: 0},
      compiler_params=_params(3),
      name="attn_latent",
  )(q, k, v, ck, cv, mix)


def _pool_kernel(x_ref, xprev_ref, xnext_ref, ada_ref, g_ref, w_ref, ps_ref, o_ref):
  i = pl.program_id(0)
  is_ctx = i < N_CTX_BLK
  shift, scale = _mod_rows(ada_ref, _cond_row(i))[:2]
  pre = lambda x: (_rms_scale(x) * g_ref[0, 0:1]) * (1.0 + scale) + shift
  h = pre(x_ref[...])
  seq_pos = jnp.maximum(i - N_CTX_BLK, 0) % BLK_PER_LAT_SEQ
  is_first = jnp.logical_or(is_ctx, seq_pos == 0)
  is_last = jnp.logical_or(is_ctx, seq_pos == BLK_PER_LAT_SEQ - 1)
  ctx_edge = jnp.where(is_ctx, 1.0, 0.0)
  first_edge = jnp.where(is_first, 1.0, 0.0)
  last_edge = jnp.where(is_last, 1.0, 0.0)
  h_prev = jnp.where(is_first, 0.0, pre(xprev_ref[...]))
  h_next = jnp.where(is_last, 0.0, pre(xnext_ref[...]))
  n_tiles = BM // POOL_TILE
  ext_rows = POOL_TILE + 2 * POOL_HALO
  row = lax.broadcasted_iota(jnp.int32, (POOL_TILE, POOL_DG), 0)

  for g, w in enumerate(POOL_WINDOWS):
    cols = slice(g * POOL_DG, (g + 1) * POOL_DG)
    lo_cut = jnp.maximum(w // 2 - row, 0).astype(F32)
    hi_cut = jnp.maximum(row + (w - w // 2) - POOL_TILE, 0).astype(F32)
    for j in range(n_tiles):
      r0 = j * POOL_TILE
      starts = first_edge if j == 0 else ctx_edge
      ends = last_edge if j == n_tiles - 1 else ctx_edge
      hg = h[r0:r0 + POOL_TILE, cols]
      before = (h_prev[:, cols] if j == 0
                else jnp.where(is_ctx, 0.0, h[r0 - POOL_HALO:r0, cols]))
      after = (h_next[:, cols] if j == n_tiles - 1
               else jnp.where(is_ctx, 0.0, h[r0 + POOL_TILE:r0 + POOL_TILE + POOL_HALO, cols]))
      ext = jnp.concatenate([before, hg, after], axis=0)
      back, fwd, m = ext, ext, 1
      while m < w // 2:
        back = back + pltpu.roll(back, m, 0)
        fwd = fwd + pltpu.roll(fwd, ext_rows - m, 0)
        m *= 2
      win = (pltpu.roll(back, 1, 0) + fwd)[POOL_HALO:POOL_HALO + POOL_TILE]
      cnt = w - lo_cut * starts - hi_cut * ends
      pooled = win / cnt - hg
      og = jnp.dot(pooled.astype(BF16), w_ref[0, g], preferred_element_type=F32)
      o_ref[r0:r0 + POOL_TILE, cols] = (og * ps_ref[0, :, cols]).astype(BF16)


def _pool(x, ada, gains, w_pool, pool_scale, layer):
  p = layer // 2
  halo_per_blk = BM // POOL_HALO
  n_halo = N_TOK // POOL_HALO
  prev_spec = pl.BlockSpec((POOL_HALO, D), lambda i: (jnp.maximum(i * halo_per_blk - 1, 0), 0))
  next_spec = pl.BlockSpec(
      (POOL_HALO, D), lambda i: (jnp.minimum((i + 1) * halo_per_blk, n_halo - 1), 0))
  return pl.pallas_call(
      _pool_kernel,
      grid=(N_BLK,),
      in_specs=_token_specs(False, D) + [
          prev_spec, next_spec,
          _layer_spec((N_MOD, N_COND, D), layer),
          _layer_spec((4, D), layer),
          _layer_spec((N_POOL_G, POOL_DG, POOL_DG), p),
          _layer_spec((1, D), p),
      ],
      out_specs=_token_specs(False, D)[0],
      out_shape=_token_shapes(False, D, BF16)[0],
      compiler_params=_params(1),
      name="pool",
  )(x, x, x, ada, gains, w_pool, pool_scale.reshape(-1, 1, D))


def _ffn_kernel(*refs, n_x, n_out, has_wo):
  x_refs, refs = refs[:n_x], refs[n_x:]
  o_refs, refs = refs[-n_out:], refs[:-n_out]
  if has_wo:
    mix_ref, ada_ref, g_ref, wo_ref, wup_ref, wdn_ref = refs
  else:
    mix_ref, ada_ref, g_ref, wup_ref, wdn_ref = refs
  i = pl.program_id(0)
  is_ctx = i < N_CTX_BLK
  _, _, gate_m, shift_f, scale_f, gate_f = _mod_rows(ada_ref, _cond_row(i))
  groups = [slice(r * (BM // FFN_GROUPS), (r + 1) * (BM // FFN_GROUPS))
            for r in range(FFN_GROUPS)]
  if has_wo:
    a = [jnp.dot(mix_ref[rows, :], wo_ref[0], preferred_element_type=F32) for rows in groups]
  else:
    a = [mix_ref[rows, :].astype(F32) for rows in groups]
  x1 = [_token_load(is_ctx, x_refs, rows) + gate_m * (_rms_scale(a_r) * g_ref[0, 1:2])
        for rows, a_r in zip(groups, a)]
  h = [((_rms_scale(x1_r) * g_ref[0, 2:3]) * (1.0 + scale_f) + shift_f).astype(BF16)
       for x1_r in x1]
  gu = [jnp.dot(h_r, wup_ref[0], preferred_element_type=F32) for h_r in h]
  act = [(_silu(gu_r[:, :D_FF]) * gu_r[:, D_FF:]).astype(BF16) for gu_r in gu]
  y = [jnp.dot(act_r, wdn_ref[0], preferred_element_type=F32) for act_r in act]
  for rows, x1_r, y_r in zip(groups, x1, y):
    _token_store(is_ctx, o_refs, x1_r + gate_f * (_rms_scale(y_r) * g_ref[0, 3:4]), rows)


def _ffn(xs, mix, ada, gains, w_o, w_up, w_down, layer, split_out):
  has_wo = w_o is not None
  in_specs = _token_specs(len(xs) == 2, D) + _token_specs(False, D) + [
      _layer_spec((N_MOD, N_COND, D), layer),
      _layer_spec((4, D), layer),
  ]
  args = list(xs) + [mix, ada, gains]
  if has_wo:
    in_specs.append(_layer_spec((Q_W, D), layer // 2))
    args.append(w_o)
  in_specs += [_layer_spec((D, 2 * D_FF), layer), _layer_spec((D_FF, D), layer)]
  args += [w_up, w_down]
  return pl.pallas_call(
      functools.partial(_ffn_kernel, n_x=len(xs), n_out=2 if split_out else 1, has_wo=has_wo),
      grid=(N_BLK,),
      in_specs=in_specs,
      out_specs=_token_specs(split_out, D),
      out_shape=_token_shapes(split_out, D, F32),
      compiler_params=_params(1),
      name="ffn_attn" if has_wo else "ffn_pool",
  )(*args)


def _rope_tables():
  pos = jnp.arange(LAT_T)
  row = (pos // GRID_W).astype(F32)
  col = (pos % GRID_W).astype(F32)
  inv = ROPE_BASE ** (-jnp.arange(ROPE_PAIRS, dtype=F32) / ROPE_PAIRS)
  ar = row[:, None] * inv
  ac = col[:, None] * inv
  cos = jnp.concatenate([jnp.cos(ar), jnp.cos(ac), jnp.cos(ar), jnp.cos(ac)], axis=1)
  sin = jnp.concatenate([-jnp.sin(ar), -jnp.sin(ac), jnp.sin(ar), jnp.sin(ac)], axis=1)
  return jnp.concatenate([cos, sin], axis=1)


def _paired_qk_columns(w_qkv, qk_gains):
  order = jnp.asarray(HEAD_PAIRED_ORDER)
  heads = jnp.arange(N_HEADS + N_KV)[:, None] * HD
  cols = jnp.concatenate([(heads + order[None, :]).reshape(-1), jnp.arange(Q_W + KV_W, QKV_W)])
  return jnp.take(w_qkv, cols, axis=2), jnp.take(qk_gains, order, axis=2)


def kernel(x_prompt, x_sample, c, cache_k, cache_v, c_ctx, w_ada, b_ada, norm_gains,
           w_qkv, qk_gains, w_o, w_pool, pool_scale, w_up, w_down):
  xs = [x_prompt.reshape(N_CTX_TOK, D), x_sample.reshape(N_LAT_TOK, D)]
  cond = jnp.concatenate(
      [c_ctx[None, :], c, jnp.zeros((N_COND - 1 - N_LAT_SEQ, D), F32)], axis=0)
  ada = _modulation(cond, w_ada, b_ada)
  rope_cs = _rope_tables()
  w_qkv_p, qk_gains_p = _paired_qk_columns(w_qkv, qk_gains)
  w_qkv_b = w_qkv_p.astype(BF16)
  w_o_b = w_o.astype(BF16)
  w_pool_b = w_pool.astype(BF16)
  w_up_b = w_up.astype(BF16)
  w_down_b = w_down.astype(BF16)

  new_kv = None
  for layer in range(DEPTH):
    last = layer == DEPTH - 1
    if layer % 2 == 0:
      q, k, v, *new_kv = _qkv(xs, ada, norm_gains, w_qkv_b, qk_gains_p, rope_cs, new_kv, layer)
      mix = _attn_latent(q, k, v, cache_k, cache_v, _attn_context(q, k, v), layer // 2)
      xs = _ffn(xs, mix, ada, norm_gains, w_o_b, w_up_b, w_down_b, layer, last)
    else:
      mix = _pool(xs[0], ada, norm_gains, w_pool_b, pool_scale, layer)
      xs = _ffn(xs, mix, ada, norm_gains, None, w_up_b, w_down_b, layer, last)

  out_kv = (N_CTX_SEQ, N_ATTN, CTX_T, N_KV, HD)
  return (xs[0].reshape(N_CTX_SEQ, CTX_T, D), xs[1].reshape(N_LAT_SEQ, LAT_T, D),
          new_kv[0].reshape(out_kv), new_kv[1].reshape(out_kv))
```

```python
import functools
import math

import jax
import jax.numpy as jnp
import numpy as np
from jax import lax
from jax.experimental import pallas as pl
from jax.experimental.pallas import tpu as pltpu

F32 = jnp.float32
BF16 = jnp.bfloat16

D = 1024
N_CTX_SEQ = 32
CTX_T = 256
N_LAT_SEQ = 4
LAT_T = 2048
PAST = 512
DEPTH = 4
N_ATTN = 2
GRID_W = 64
HD = 128
N_HEADS = 8
N_KV = 2
GQA = N_HEADS // N_KV
Q_W = N_HEADS * HD
KV_W = N_KV * HD
QKV_W = Q_W + 2 * KV_W
ROPE_PAIRS = HD // 4
ROPE_BASE = 10000.0
POOL_WINDOWS = (2, 4, 8, 16)
N_POOL_G = len(POOL_WINDOWS)
POOL_DG = D // N_POOL_G
D_FF = 2816
N_MOD = 6
EPS = 1e-6

N_CTX_TOK = N_CTX_SEQ * CTX_T
N_LAT_TOK = N_LAT_SEQ * LAT_T
N_TOK = N_CTX_TOK + N_LAT_TOK
N_COND = 8
Q_SCALE = math.log2(math.e) / math.sqrt(HD)

VMEM_LIMIT = 56 * 1024 * 1024

BM = 512
N_CTX_BLK = N_CTX_TOK // BM
N_LAT_BLK = N_LAT_TOK // BM
N_BLK = N_CTX_BLK + N_LAT_BLK
BLK_PER_LAT_SEQ = LAT_T // BM
FFN_GROUPS = 2
POOL_TILE = CTX_T
POOL_HALO = max(POOL_WINDOWS) // 2
ATT_TQ = 1024
ATT_CHUNK = 256
CTX_NB = 2


def _token_specs(split, width):
  if not split:
    return [pl.BlockSpec((BM, width), lambda i: (i, 0))]
  return [pl.BlockSpec((BM, width), lambda i: (jnp.minimum(i, N_CTX_BLK - 1), 0)),
          pl.BlockSpec((BM, width), lambda i: (jnp.maximum(i - N_CTX_BLK, 0), 0))]


def _token_shapes(split, width, dtype):
  if not split:
    return [jax.ShapeDtypeStruct((N_TOK, width), dtype)]
  return [jax.ShapeDtypeStruct((N_CTX_TOK, width), dtype),
          jax.ShapeDtypeStruct((N_LAT_TOK, width), dtype)]


def _token_load(is_ctx, refs, rows=slice(None)):
  if len(refs) == 1:
    return refs[0][rows, :]
  return jnp.where(is_ctx, refs[0][rows, :], refs[1][rows, :])


def _token_store(is_ctx, refs, value, rows=slice(None)):
  if len(refs) == 1:
    refs[0][rows, :] = value
    return

  @pl.when(is_ctx)
  def _():
    refs[0][rows, :] = value

  @pl.when(jnp.logical_not(is_ctx))
  def _():
    refs[1][rows, :] = value


def _cond_row(i):
  return jnp.where(i < N_CTX_BLK, 0, 1 + (i - N_CTX_BLK) // BLK_PER_LAT_SEQ)


def _mod_rows(ada_ref, row):
  return [ada_ref[0, m, pl.ds(row, 1), :] for m in range(N_MOD)]


def _layer_spec(shape, layer):
  zeros = (0,) * len(shape)
  return pl.BlockSpec((1,) + shape, lambda *_: (layer,) + zeros, pipeline_mode=pl.Buffered(1))


def _rms_scale(x):
  return x * lax.rsqrt(jnp.mean(x * x, axis=-1, keepdims=True) + EPS)


def _silu(x):
  return x / (1.0 + jnp.exp(-x))


def _params(n_grid_dims):
  return pltpu.CompilerParams(
      dimension_semantics=("arbitrary",) * n_grid_dims, vmem_limit_bytes=VMEM_LIMIT)


def _mod_kernel(cond_ref, w_ref, b_ref, o_ref):
  s = _silu(cond_ref[...]).astype(BF16)
  o_ref[0, 0] = jnp.dot(s, w_ref[0].astype(BF16), preferred_element_type=F32) + b_ref[0, 0]


def _modulation(cond, w_ada, b_ada):
  return pl.pallas_call(
      _mod_kernel,
      grid=(DEPTH, N_MOD),
      in_specs=[
          pl.BlockSpec((N_COND, D), lambda l, m: (0, 0)),
          pl.BlockSpec((1, D, D), lambda l, m: (l, 0, m)),
          pl.BlockSpec((1, 1, 1, D), lambda l, m: (l, m, 0, 0)),
      ],
      out_specs=pl.BlockSpec((1, 1, N_COND, D), lambda l, m: (l, m, 0, 0)),
      out_shape=jax.ShapeDtypeStruct((DEPTH, N_MOD, N_COND, D), F32),
      compiler_params=_params(2),
      name="modulation",
  )(cond, w_ada, b_ada.reshape(DEPTH, N_MOD, 1, D))


def _swap_middle(x):
  blk = lax.broadcasted_iota(jnp.int32, x.shape, 1) // ROPE_PAIRS
  return jnp.where(blk == 1, pltpu.roll(x, HD - ROPE_PAIRS, 1),
                   jnp.where(blk == 2, pltpu.roll(x, ROPE_PAIRS, 1), x))


def _qkv_kernel(*refs, n_x):
  x_refs, refs = refs[:n_x], refs[n_x:]
  ada_ref, g_ref, w_ref, qkg_ref, cs_ref = refs[:5]
  q_ref, k_ref, v_ref, nk_ref, nv_ref = refs[-5:]
  i = pl.program_id(0)
  is_ctx = i < N_CTX_BLK
  shift, scale = _mod_rows(ada_ref, _cond_row(i))[:2]
  gq = qkg_ref[0, 0:1] * Q_SCALE
  gk = qkg_ref[0, 1:2]
  groups = [slice(s * CTX_T, (s + 1) * CTX_T) for s in range(BM // CTX_T)]

  def body(x_ref, latent):
    h = [((_rms_scale(x_ref[rows, :]) * g_ref[0, 0:1]) * (1.0 + scale) + shift).astype(BF16)
         for rows in groups]
    qkv = [jnp.dot(h_s, w_ref[0], preferred_element_type=F32) for h_s in h]
    for s, (rows, qkv_s) in enumerate(zip(groups, qkv)):
      if latent:
        cos = cs_ref[rows, :HD]
        sin = cs_ref[rows, HD:]
        rope = lambda x: x * cos + pltpu.roll(x, HD // 2, 1) * sin
      else:
        rope = lambda x: x
      for hd in range(N_HEADS):
        cols = slice(hd * HD, (hd + 1) * HD)
        q_ref[rows, cols] = rope(_rms_scale(qkv_s[:, cols]) * gq).astype(BF16)
      v = qkv_s[:, Q_W + KV_W:]
      v_ref[rows, :] = v.astype(BF16)
      for hd in range(N_KV):
        cols = slice(hd * HD, (hd + 1) * HD)
        k = rope(_rms_scale(qkv_s[:, Q_W + hd * HD:Q_W + (hd + 1) * HD]) * gk)
        k_ref[rows, cols] = k.astype(BF16)
        if not latent:
          nk_ref[s, 0, :, cols] = _swap_middle(k)
      if not latent:
        nv_ref[s, 0] = v

  @pl.when(is_ctx)
  def _():
    body(x_refs[0], latent=False)

  @pl.when(jnp.logical_not(is_ctx))
  def _():
    body(x_refs[-1], latent=True)


def _qkv(xs, ada, gains, w_qkv, qk_gains, rope_cs, new_kv, layer):
  a = layer // 2
  seqs = BM // CTX_T
  new_spec = pl.BlockSpec(
      (seqs, 1, CTX_T, KV_W), lambda i: (jnp.minimum(i, N_CTX_BLK - 1), a, 0, 0))
  new_shape = jax.ShapeDtypeStruct((N_CTX_SEQ, N_ATTN, CTX_T, KV_W), F32)
  args = list(xs) + [ada, gains, w_qkv, qk_gains, rope_cs]
  in_specs = _token_specs(len(xs) == 2, D) + [
      _layer_spec((N_MOD, N_COND, D), layer),
      _layer_spec((4, D), layer),
      _layer_spec((D, QKV_W), a),
      _layer_spec((2, HD), a),
      pl.BlockSpec(
          (BM, 2 * HD), lambda i: (jnp.maximum(i - N_CTX_BLK, 0) % BLK_PER_LAT_SEQ, 0)),
  ]
  aliases = {}
  if new_kv is not None:
    aliases = {len(args): 3, len(args) + 1: 4}
    args += list(new_kv)
    in_specs += [pl.BlockSpec(memory_space=pl.ANY)] * 2
  return pl.pallas_call(
      functools.partial(_qkv_kernel, n_x=len(xs)),
      grid=(N_BLK,),
      in_specs=in_specs,
      out_specs=(_token_specs(False, Q_W) + _token_specs(False, KV_W) * 2 + [new_spec] * 2),
      out_shape=(_token_shapes(False, Q_W, BF16) + _token_shapes(False, KV_W, BF16) * 2
                 + [new_shape] * 2),
      input_output_aliases=aliases,
      compiler_params=_params(1),
      name="qkv",
  )(*args)


def _nt_dot(a, b):
  return lax.dot_general(a, b, (((1,), (1,)), ((), ())), preferred_element_type=F32)


def _with_ones(v):
  return jnp.concatenate([v, jnp.ones_like(v)], axis=1)


def _softmax_pv(s, v1):
  p = jnp.exp2(s - jnp.max(s, axis=-1, keepdims=True)).astype(BF16)
  oa = jnp.dot(p, v1, preferred_element_type=F32)
  return oa[:, :HD] / oa[:, HD:]


def _attn_ctx_kernel(q_ref, k_ref, v_ref, o_ref):
  for b in range(CTX_NB):
    rows = slice(b * CTX_T, (b + 1) * CTX_T)
    for h in range(N_KV):
      kv_cols = slice(h * HD, (h + 1) * HD)
      v1 = _with_ones(v_ref[rows, kv_cols])
      for g in range(GQA):
        cols = slice((h * GQA + g) * HD, (h * GQA + g + 1) * HD)
        s = _nt_dot(q_ref[rows, cols], k_ref[rows, kv_cols])
        o_ref[rows, cols] = _softmax_pv(s, v1).astype(BF16)


def _attn_context(q, k, v):
  bm = CTX_NB * CTX_T
  row = lambda i: (i, 0)
  return pl.pallas_call(
      _attn_ctx_kernel,
      grid=(N_CTX_TOK // bm,),
      in_specs=[
          pl.BlockSpec((bm, Q_W), row),
          pl.BlockSpec((bm, KV_W), row),
          pl.BlockSpec((bm, KV_W), row),
      ],
      out_specs=pl.BlockSpec((bm, Q_W), row),
      out_shape=jax.ShapeDtypeStruct((N_TOK, Q_W), BF16),
      compiler_params=_params(1),
      name="attn_context",
  )(q, k, v)


def _attn_lat_kernel(q_ref, k_ref, v_ref, ck_ref, cv_ref, mix_in, o_ref, kcat, vcat):
  del mix_in

  @pl.when(pl.program_id(2) == 0)
  def _():
    kcat[:LAT_T] = k_ref[...]
    kcat[LAT_T:] = _swap_middle(ck_ref[0, 0]).astype(BF16)
    vcat[:LAT_T] = _with_ones(v_ref[...])
    vcat[LAT_T:] = _with_ones(cv_ref[0, 0].astype(BF16))

  for g in range(GQA):
    cols = slice(g * HD, (g + 1) * HD)
    for r in range(ATT_TQ // ATT_CHUNK):
      rows = slice(r * ATT_CHUNK, (r + 1) * ATT_CHUNK)
      s = _nt_dot(q_ref[rows, cols], kcat[...])
      o_ref[rows, cols] = _softmax_pv(s, vcat[...]).astype(BF16)


def _attn_latent(q, k, v, cache_k, cache_v, mix, a):
  nq = LAT_T // ATT_TQ
  q0 = N_CTX_TOK // ATT_TQ
  kv0 = N_CTX_TOK // LAT_T
  ck = cache_k.reshape(N_LAT_SEQ, N_ATTN, PAST, KV_W)
  cv = cache_v.reshape(N_LAT_SEQ, N_ATTN, PAST, KV_W)
  return pl.pallas_call(
      _attn_lat_kernel,
      grid=(N_LAT_SEQ, N_KV, nq),
      in_specs=[
          pl.BlockSpec((ATT_TQ, GQA * HD), lambda b, h, j: (q0 + b * nq + j, h)),
          pl.BlockSpec((LAT_T, HD), lambda b, h, j: (kv0 + b, h)),
          pl.BlockSpec((LAT_T, HD), lambda b, h, j: (kv0 + b, h)),
          pl.BlockSpec((1, 1, PAST, HD), lambda b, h, j: (b, a, 0, h)),
          pl.BlockSpec((1, 1, PAST, HD), lambda b, h, j: (b, a, 0, h)),
          pl.BlockSpec(memory_space=pl.ANY),
      ],
      out_specs=pl.BlockSpec((ATT_TQ, GQA * HD), lambda b, h, j: (q0 + b * nq + j, h)),
      out_shape=jax.ShapeDtypeStruct((N_TOK, Q_W), BF16),
      scratch_shapes=[pltpu.VMEM((LAT_T + PAST, HD), BF16),
                      pltpu.VMEM((LAT_T + PAST, 2 * HD), BF16)],
      input_output_aliases={5: 0},
      compiler_params=_params(3),
      name="attn_latent",
  )(q, k, v, ck, cv, mix)


def _pool_kernel(x_ref, xprev_ref, xnext_ref, ada_ref, g_ref, w_ref, ps_ref, o_ref):
  i = pl.program_id(0)
  is_ctx = i < N_CTX_BLK
  shift, scale = _mod_rows(ada_ref, _cond_row(i))[:2]
  pre = lambda x: (_rms_scale(x) * g_ref[0, 0:1]) * (1.0 + scale) + shift
  h = pre(x_ref[...])
  seq_pos = jnp.maximum(i - N_CTX_BLK, 0) % BLK_PER_LAT_SEQ
  is_first = jnp.logical_or(is_ctx, seq_pos == 0)
  is_last = jnp.logical_or(is_ctx, seq_pos == BLK_PER_LAT_SEQ - 1)
  ctx_edge = jnp.where(is_ctx, 1.0, 0.0)
  first_edge = jnp.where(is_first, 1.0, 0.0)
  last_edge = jnp.where(is_last, 1.0, 0.0)
  h_prev = jnp.where(is_first, 0.0, pre(xprev_ref[...]))
  h_next = jnp.where(is_last, 0.0, pre(xnext_ref[...]))
  n_tiles = BM // POOL_TILE
  ext_rows = POOL_TILE + 2 * POOL_HALO
  row = lax.broadcasted_iota(jnp.int32, (POOL_TILE, POOL_DG), 0)

  for g, w in enumerate(POOL_WINDOWS):
    cols = slice(g * POOL_DG, (g + 1) * POOL_DG)
    lo_cut = jnp.maximum(w // 2 - row, 0).astype(F32)
    hi_cut = jnp.maximum(row + (w - w // 2) - POOL_TILE, 0).astype(F32)
    for j in range(n_tiles):
      r0 = j * POOL_TILE
      starts = first_edge if j == 0 else ctx_edge
      ends = last_edge if j == n_tiles - 1 else ctx_edge
      hg = h[r0:r0 + POOL_TILE, cols]
      before = (h_prev[:, cols] if j == 0
                else jnp.where(is_ctx, 0.0, h[r0 - POOL_HALO:r0, cols]))
      after = (h_next[:, cols] if j == n_tiles - 1
               else jnp.where(is_ctx, 0.0, h[r0 + POOL_TILE:r0 + POOL_TILE + POOL_HALO, cols]))
      ext = jnp.concatenate([before, hg, after], axis=0)
      back, fwd, m = ext, ext, 1
      while m < w // 2:
        back = back + pltpu.roll(back, m, 0)
        fwd = fwd + pltpu.roll(fwd, ext_rows - m, 0)
        m *= 2
      win = (pltpu.roll(back, 1, 0) + fwd)[POOL_HALO:POOL_HALO + POOL_TILE]
      cnt = w - lo_cut * starts - hi_cut * ends
      pooled = win / cnt - hg
      og = jnp.dot(pooled.astype(BF16), w_ref[0, g], preferred_element_type=F32)
      o_ref[r0:r0 + POOL_TILE, cols] = (og * ps_ref[0, :, cols]).astype(BF16)


def _pool(x, ada, gains, w_pool, pool_scale, layer):
  p = layer // 2
  halo_per_blk = BM // POOL_HALO
  n_halo = N_TOK // POOL_HALO
  prev_spec = pl.BlockSpec((POOL_HALO, D), lambda i: (jnp.maximum(i * halo_per_blk - 1, 0), 0))
  next_spec = pl.BlockSpec(
      (POOL_HALO, D), lambda i: (jnp.minimum((i + 1) * halo_per_blk, n_halo - 1), 0))
  return pl.pallas_call(
      _pool_kernel,
      grid=(N_BLK,),
      in_specs=_token_specs(False, D) + [
          prev_spec, next_spec,
          _layer_spec((N_MOD, N_COND, D), layer),
          _layer_spec((4, D), layer),
          _layer_spec((N_POOL_G, POOL_DG, POOL_DG), p),
          _layer_spec((1, D), p),
      ],
      out_specs=_token_specs(False, D)[0],
      out_shape=_token_shapes(False, D, BF16)[0],
      compiler_params=_params(1),
      name="pool",
  )(x, x, x, ada, gains, w_pool, pool_scale.reshape(-1, 1, D))


def _ffn_kernel(*refs, n_x, n_out, has_wo):
  x_refs, refs = refs[:n_x], refs[n_x:]
  o_refs, refs = refs[-n_out:], refs[:-n_out]
  if has_wo:
    mix_ref, ada_ref, g_ref, wo_ref, wup_ref, wdn_ref = refs
  else:
    mix_ref, ada_ref, g_ref, wup_ref, wdn_ref = refs
  i = pl.program_id(0)
  is_ctx = i < N_CTX_BLK
  _, _, gate_m, shift_f, scale_f, gate_f = _mod_rows(ada_ref, _cond_row(i))
  groups = [slice(r * (BM // FFN_GROUPS), (r + 1) * (BM // FFN_GROUPS))
            for r in range(FFN_GROUPS)]
  if has_wo:
    a = [jnp.dot(mix_ref[rows, :], wo_ref[0], preferred_element_type=F32) for rows in groups]
  else:
    a = [mix_ref[rows, :].astype(F32) for rows in groups]
  x1 = [_token_load(is_ctx, x_refs, rows) + gate_m * (_rms_scale(a_r) * g_ref[0, 1:2])
        for rows, a_r in zip(groups, a)]
  h = [((_rms_scale(x1_r) * g_ref[0, 2:3]) * (1.0 + scale_f) + shift_f).astype(BF16)
       for x1_r in x1]
  gu = [jnp.dot(h_r, wup_ref[0], preferred_element_type=F32) for h_r in h]
  act = [(_silu(gu_r[:, :D_FF]) * gu_r[:, D_FF:]).astype(BF16) for gu_r in gu]
  y = [jnp.dot(act_r, wdn_ref[0], preferred_element_type=F32) for act_r in act]
  for rows, x1_r, y_r in zip(groups, x1, y):
    _token_store(is_ctx, o_refs, x1_r + gate_f * (_rms_scale(y_r) * g_ref[0, 3:4]), rows)


def _ffn(xs, mix, ada, gains, w_o, w_up, w_down, layer, split_out):
  has_wo = w_o is not None
  in_specs = _token_specs(len(xs) == 2, D) + _token_specs(False, D) + [
      _layer_spec((N_MOD, N_COND, D), layer),
      _layer_spec((4, D), layer),
  ]
  args = list(xs) + [mix, ada, gains]
  if has_wo:
    in_specs.append(_layer_spec((Q_W, D), layer // 2))
    args.append(w_o)
  in_specs += [_layer_spec((D, 2 * D_FF), layer), _layer_spec((D_FF, D), layer)]
  args += [w_up, w_down]
  return pl.pallas_call(
      functools.partial(_ffn_kernel, n_x=len(xs), n_out=2 if split_out else 1, has_wo=has_wo),
      grid=(N_BLK,),
      in_specs=in_specs,
      out_specs=_token_specs(split_out, D),
      out_shape=_token_shapes(split_out, D, F32),
      compiler_params=_params(1),
      name="ffn_attn" if has_wo else "ffn_pool",
  )(*args)


def _rope_tables():
  pos = np.arange(LAT_T)
  row = (pos // GRID_W).astype(np.float32)
  col = (pos % GRID_W).astype(np.float32)
  inv = np.float32(ROPE_BASE) ** (-np.arange(ROPE_PAIRS, dtype=np.float32) / np.float32(ROPE_PAIRS))
  ar = row[:, None] * inv
  ac = col[:, None] * inv
  cos = np.concatenate([np.cos(ar), np.cos(ac), np.cos(ar), np.cos(ac)], axis=1)
  sin = np.concatenate([-np.sin(ar), -np.sin(ac), np.sin(ar), np.sin(ac)], axis=1)
  return jnp.asarray(np.concatenate([cos, sin], axis=1), dtype=F32)


def _paired_qk_columns(w_qkv, qk_gains):
  def paired(a):
    blocks = a.reshape(a.shape[:-1] + (-1, 2, 2, ROPE_PAIRS))
    return jnp.swapaxes(blocks, -3, -2).reshape(a.shape)

  w = jnp.concatenate([paired(w_qkv[..., :Q_W + KV_W]), w_qkv[..., Q_W + KV_W:]], axis=-1)
  return w, paired(qk_gains)


def kernel(x_prompt, x_sample, c, cache_k, cache_v, c_ctx, w_ada, b_ada, norm_gains,
           w_qkv, qk_gains, w_o, w_pool, pool_scale, w_up, w_down):
  xs = [x_prompt.reshape(N_CTX_TOK, D), x_sample.reshape(N_LAT_TOK, D)]
  cond = jnp.concatenate(
      [c_ctx[None, :], c, jnp.zeros((N_COND - 1 - N_LAT_SEQ, D), F32)], axis=0)
  ada = _modulation(cond, w_ada, b_ada)
  rope_cs = _rope_tables()
  w_qkv_p, qk_gains_p = _paired_qk_columns(w_qkv, qk_gains)
  w_qkv_b = w_qkv_p.astype(BF16)
  w_o_b = w_o.astype(BF16)
  w_pool_b = w_pool.astype(BF16)
  w_up_b = w_up.astype(BF16)
  w_down_b = w_down.astype(BF16)

  new_kv = None
  for layer in range(DEPTH):
    last = layer == DEPTH - 1
    if layer % 2 == 0:
      q, k, v, *new_kv = _qkv(xs, ada, norm_gains, w_qkv_b, qk_gains_p, rope_cs, new_kv, layer)
      mix = _attn_latent(q, k, v, cache_k, cache_v, _attn_context(q, k, v), layer // 2)
      xs = _ffn(xs, mix, ada, norm_gains, w_o_b, w_up_b, w_down_b, layer, last)
    else:
      mix = _pool(xs[0], ada, norm_gains, w_pool_b, pool_scale, layer)
      xs = _ffn(xs, mix, ada, norm_gains, None, w_up_b, w_down_b, layer, last)

  out_kv = (N_CTX_SEQ, N_ATTN, CTX_T, N_KV, HD)
  return (xs[0].reshape(N_CTX_SEQ, CTX_T, D), xs[1].reshape(N_LAT_SEQ, LAT_T, D),
          new_kv[0].reshape(out_kv), new_kv[1].reshape(out_kv))
```

```python
import functools
import math

import jax
import jax.numpy as jnp
import numpy as np
from jax import lax
from jax.experimental import pallas as pl
from jax.experimental.pallas import tpu as pltpu

F32 = jnp.float32
BF16 = jnp.bfloat16

D = 1024
N_CTX_SEQ = 32
CTX_T = 256
N_LAT_SEQ = 4
LAT_T = 2048
PAST = 512
DEPTH = 4
N_ATTN = 2
GRID_W = 64
HD = 128
N_HEADS = 8
N_KV = 2
GQA = N_HEADS // N_KV
Q_W = N_HEADS * HD
KV_W = N_KV * HD
QKV_W = Q_W + 2 * KV_W
ROPE_PAIRS = HD // 4
ROPE_BASE = 10000.0
POOL_WINDOWS = (2, 4, 8, 16)
N_POOL_G = len(POOL_WINDOWS)
POOL_DG = D // N_POOL_G
D_FF = 2816
N_MOD = 6
EPS = 1e-6

N_CTX_TOK = N_CTX_SEQ * CTX_T
N_LAT_TOK = N_LAT_SEQ * LAT_T
N_TOK = N_CTX_TOK + N_LAT_TOK
N_COND = 8
Q_SCALE = math.log2(math.e) / math.sqrt(HD)

VMEM_LIMIT = 56 * 1024 * 1024

BM = 512
N_CTX_BLK = N_CTX_TOK // BM
N_LAT_BLK = N_LAT_TOK // BM
N_BLK = N_CTX_BLK + N_LAT_BLK
BLK_PER_LAT_SEQ = LAT_T // BM
FFN_GROUPS = 2
POOL_TILE = CTX_T
POOL_HALO = max(POOL_WINDOWS) // 2
ATT_TQ = 1024
ATT_CHUNK = 256
CTX_NB = 2


def _token_specs(split, width):
  if not split:
    return [pl.BlockSpec((BM, width), lambda i: (i, 0))]
  return [pl.BlockSpec((BM, width), lambda i: (jnp.minimum(i, N_CTX_BLK - 1), 0)),
          pl.BlockSpec((BM, width), lambda i: (jnp.maximum(i - N_CTX_BLK, 0), 0))]


def _token_shapes(split, width, dtype):
  if not split:
    return [jax.ShapeDtypeStruct((N_TOK, width), dtype)]
  return [jax.ShapeDtypeStruct((N_CTX_TOK, width), dtype),
          jax.ShapeDtypeStruct((N_LAT_TOK, width), dtype)]


def _token_load(is_ctx, refs, rows=slice(None)):
  if len(refs) == 1:
    return refs[0][rows, :]
  return jnp.where(is_ctx, refs[0][rows, :], refs[1][rows, :])


def _token_store(is_ctx, refs, value, rows=slice(None)):
  if len(refs) == 1:
    refs[0][rows, :] = value
    return

  @pl.when(is_ctx)
  def _():
    refs[0][rows, :] = value

  @pl.when(jnp.logical_not(is_ctx))
  def _():
    refs[1][rows, :] = value


def _cond_row(i):
  return jnp.where(i < N_CTX_BLK, 0, 1 + (i - N_CTX_BLK) // BLK_PER_LAT_SEQ)


def _mod_rows(ada_ref, row):
  return [ada_ref[0, m, pl.ds(row, 1), :] for m in range(N_MOD)]


def _layer_spec(shape, layer):
  zeros = (0,) * len(shape)
  return pl.BlockSpec((1,) + shape, lambda *_: (layer,) + zeros, pipeline_mode=pl.Buffered(1))


def _rms_scale(x):
  return x * lax.rsqrt(jnp.mean(x * x, axis=-1, keepdims=True) + EPS)


def _silu(x):
  return x / (1.0 + jnp.exp(-x))


def _params(n_grid_dims):
  return pltpu.CompilerParams(
      dimension_semantics=("arbitrary",) * n_grid_dims, vmem_limit_bytes=VMEM_LIMIT)


def _mod_kernel(cond_ref, w_ref, b_ref, o_ref):
  s = _silu(cond_ref[...]).astype(BF16)
  o_ref[0, 0] = jnp.dot(s, w_ref[0].astype(BF16), preferred_element_type=F32) + b_ref[0, 0]


def _modulation(cond, w_ada, b_ada):
  return pl.pallas_call(
      _mod_kernel,
      grid=(DEPTH, N_MOD),
      in_specs=[
          pl.BlockSpec((N_COND, D), lambda l, m: (0, 0)),
          pl.BlockSpec((1, D, D), lambda l, m: (l, 0, m)),
          pl.BlockSpec((1, 1, 1, D), lambda l, m: (l, m, 0, 0)),
      ],
      out_specs=pl.BlockSpec((1, 1, N_COND, D), lambda l, m: (l, m, 0, 0)),
      out_shape=jax.ShapeDtypeStruct((DEPTH, N_MOD, N_COND, D), F32),
      compiler_params=_params(2),
      name="modulation",
  )(cond, w_ada, b_ada.reshape(DEPTH, N_MOD, 1, D))


def _swap_middle(x):
  blk = lax.broadcasted_iota(jnp.int32, x.shape, 1) // ROPE_PAIRS
  return jnp.where(blk == 1, pltpu.roll(x, HD - ROPE_PAIRS, 1),
                   jnp.where(blk == 2, pltpu.roll(x, ROPE_PAIRS, 1), x))


def _qkv_kernel(*refs, n_x):
  x_refs, refs = refs[:n_x], refs[n_x:]
  ada_ref, g_ref, w_ref, qkg_ref, cs_ref = refs[:5]
  q_ref, k_ref, v_ref, nk_ref, nv_ref = refs[-5:]
  i = pl.program_id(0)
  is_ctx = i < N_CTX_BLK
  shift, scale = _mod_rows(ada_ref, _cond_row(i))[:2]
  gq = qkg_ref[0, 0:1] * Q_SCALE
  gk = qkg_ref[0, 1:2]
  groups = [slice(s * CTX_T, (s + 1) * CTX_T) for s in range(BM // CTX_T)]

  def body(x_ref, latent):
    h = [((_rms_scale(x_ref[rows, :]) * g_ref[0, 0:1]) * (1.0 + scale) + shift).astype(BF16)
         for rows in groups]
    qkv = [jnp.dot(h_s, w_ref[0], preferred_element_type=F32) for h_s in h]
    for s, (rows, qkv_s) in enumerate(zip(groups, qkv)):
      if latent:
        cos = cs_ref[rows, :HD]
        sin = cs_ref[rows, HD:]
        rope = lambda x: x * cos + pltpu.roll(x, HD // 2, 1) * sin
      else:
        rope = lambda x: x
      for hd in range(N_HEADS):
        cols = slice(hd * HD, (hd + 1) * HD)
        q_ref[rows, cols] = rope(_rms_scale(qkv_s[:, cols]) * gq).astype(BF16)
      v = qkv_s[:, Q_W + KV_W:]
      v_ref[rows, :] = v.astype(BF16)
      for hd in range(N_KV):
        cols = slice(hd * HD, (hd + 1) * HD)
        k = rope(_rms_scale(qkv_s[:, Q_W + hd * HD:Q_W + (hd + 1) * HD]) * gk)
        k_ref[rows, cols] = k.astype(BF16)
        if not latent:
          nk_ref[s, 0, :, cols] = _swap_middle(k)
      if not latent:
        nv_ref[s, 0] = v

  @pl.when(is_ctx)
  def _():
    body(x_refs[0], latent=False)

  @pl.when(jnp.logical_not(is_ctx))
  def _():
    body(x_refs[-1], latent=True)


def _qkv(xs, ada, gains, w_qkv, qk_gains, rope_cs, new_kv, layer):
  a = layer // 2
  seqs = BM // CTX_T
  new_spec = pl.BlockSpec(
      (seqs, 1, CTX_T, KV_W), lambda i: (jnp.minimum(i, N_CTX_BLK - 1), a, 0, 0))
  new_shape = jax.ShapeDtypeStruct((N_CTX_SEQ, N_ATTN, CTX_T, KV_W), F32)
  args = list(xs) + [ada, gains, w_qkv, qk_gains, rope_cs]
  in_specs = _token_specs(len(xs) == 2, D) + [
      _layer_spec((N_MOD, N_COND, D), layer),
      _layer_spec((4, D), layer),
      _layer_spec((D, QKV_W), a),
      _layer_spec((2, HD), a),
      pl.BlockSpec(
          (BM, 2 * HD), lambda i: (jnp.maximum(i - N_CTX_BLK, 0) % BLK_PER_LAT_SEQ, 0)),
  ]
  aliases = {}
  if new_kv is not None:
    aliases = {len(args): 3, len(args) + 1: 4}
    args += list(new_kv)
    in_specs += [pl.BlockSpec(memory_space=pl.ANY)] * 2
  return pl.pallas_call(
      functools.partial(_qkv_kernel, n_x=len(xs)),
      grid=(N_BLK,),
      in_specs=in_specs,
      out_specs=(_token_specs(False, Q_W) + _token_specs(False, KV_W) * 2 + [new_spec] * 2),
      out_shape=(_token_shapes(False, Q_W, BF16) + _token_shapes(False, KV_W, BF16) * 2
                 + [new_shape] * 2),
      input_output_aliases=aliases,
      compiler_params=_params(1),
      name="qkv",
  )(*args)


def _nt_dot(a, b):
  return lax.dot_general(a, b, (((1,), (1,)), ((), ())), preferred_element_type=F32)


def _with_ones(v):
  return jnp.concatenate([v, jnp.ones_like(v)], axis=1)


def _softmax_pv(s, v1):
  p = jnp.exp2(s - jnp.max(s, axis=-1, keepdims=True)).astype(BF16)
  oa = jnp.dot(p, v1, preferred_element_type=F32)
  return oa[:, :HD] / oa[:, HD:]


def _attn_ctx_kernel(q_ref, k_ref, v_ref, o_ref):
  for b in range(CTX_NB):
    rows = slice(b * CTX_T, (b + 1) * CTX_T)
    for h in range(N_KV):
      kv_cols = slice(h * HD, (h + 1) * HD)
      v1 = _with_ones(v_ref[rows, kv_cols])
      for g in range(GQA):
        cols = slice((h * GQA + g) * HD, (h * GQA + g + 1) * HD)
        s = _nt_dot(q_ref[rows, cols], k_ref[rows, kv_cols])
        o_ref[rows, cols] = _softmax_pv(s, v1).astype(BF16)


def _attn_context(q, k, v):
  bm = CTX_NB * CTX_T
  row = lambda i: (i, 0)
  return pl.pallas_call(
      _attn_ctx_kernel,
      grid=(N_CTX_TOK // bm,),
      in_specs=[
          pl.BlockSpec((bm, Q_W), row),
          pl.BlockSpec((bm, KV_W), row),
          pl.BlockSpec((bm, KV_W), row),
      ],
      out_specs=pl.BlockSpec((bm, Q_W), row),
      out_shape=jax.ShapeDtypeStruct((N_TOK, Q_W), BF16),
      compiler_params=_params(1),
      name="attn_context",
  )(q, k, v)


def _attn_lat_kernel(q_ref, k_ref, v_ref, ck_ref, cv_ref, mix_in, o_ref, kcat, vcat):
  del mix_in

  @pl.when(pl.program_id(2) == 0)
  def _():
    kcat[:LAT_T] = k_ref[...]
    kcat[LAT_T:] = _swap_middle(ck_ref[0, 0]).astype(BF16)
    vcat[:LAT_T] = _with_ones(v_ref[...])
    vcat[LAT_T:] = _with_ones(cv_ref[0, 0].astype(BF16))

  for g in range(GQA):
    cols = slice(g * HD, (g + 1) * HD)
    for r in range(ATT_TQ // ATT_CHUNK):
      rows = slice(r * ATT_CHUNK, (r + 1) * ATT_CHUNK)
      s = _nt_dot(q_ref[rows, cols], kcat[...])
      o_ref[rows, cols] = _softmax_pv(s, vcat[...]).astype(BF16)


def _attn_latent(q, k, v, cache_k, cache_v, mix, a):
  nq = LAT_T // ATT_TQ
  q0 = N_CTX_TOK // ATT_TQ
  kv0 = N_CTX_TOK // LAT_T
  ck = cache_k.reshape(N_LAT_SEQ, N_ATTN, PAST, KV_W)
  cv = cache_v.reshape(N_LAT_SEQ, N_ATTN, PAST, KV_W)
  return pl.pallas_call(
      _attn_lat_kernel,
      grid=(N_LAT_SEQ, N_KV, nq),
      in_specs=[
          pl.BlockSpec((ATT_TQ, GQA * HD), lambda b, h, j: (q0 + b * nq + j, h)),
          pl.BlockSpec((LAT_T, HD), lambda b, h, j: (kv0 + b, h)),
          pl.BlockSpec((LAT_T, HD), lambda b, h, j: (kv0 + b, h)),
          pl.BlockSpec((1, 1, PAST, HD), lambda b, h, j: (b, a, 0, h)),
          pl.BlockSpec((1, 1, PAST, HD), lambda b, h, j: (b, a, 0, h)),
          pl.BlockSpec(memory_space=pl.ANY),
      ],
      out_specs=pl.BlockSpec((ATT_TQ, GQA * HD), lambda b, h, j: (q0 + b * nq + j, h)),
      out_shape=jax.ShapeDtypeStruct((N_TOK, Q_W), BF16),
      scratch_shapes=[pltpu.VMEM((LAT_T + PAST, HD), BF16),
                      pltpu.VMEM((LAT_T + PAST, 2 * HD), BF16)],
      input_output_aliases={5: 0},
      compiler_params=_params(3),
      name="attn_latent",
  )(q, k, v, ck, cv, mix)


def _edge_window_sums(slab, w):
  rows = slab.shape[0]
  back, fwd, m = slab, slab, 1
  while m < w // 2:
    back = back + pltpu.roll(back, m, 0)
    fwd = fwd + pltpu.roll(fwd, rows - m, 0)
    m *= 2
  return (pltpu.roll(back, 1, 0) + fwd)[POOL_HALO:2 * POOL_HALO]


def _pool_bands():
  t = np.arange(POOL_TILE)[:, None]
  j = np.arange(POOL_TILE)[None, :]
  bands = [(j >= t - w // 2) & (j < t + w - w // 2) for w in POOL_WINDOWS]
  return jnp.asarray(np.stack(bands), dtype=BF16)


def _pool_kernel(x_ref, xprev_ref, xnext_ref, ada_ref, g_ref, band_ref, w_ref, ps_ref, o_ref):
  i = pl.program_id(0)
  is_ctx = i < N_CTX_BLK
  shift, scale = _mod_rows(ada_ref, _cond_row(i))[:2]
  pre = lambda x: (_rms_scale(x) * g_ref[0, 0:1]) * (1.0 + scale) + shift
  h = pre(x_ref[...])
  seq_pos = jnp.maximum(i - N_CTX_BLK, 0) % BLK_PER_LAT_SEQ
  is_first = jnp.logical_or(is_ctx, seq_pos == 0)
  is_last = jnp.logical_or(is_ctx, seq_pos == BLK_PER_LAT_SEQ - 1)
  ctx_edge = jnp.where(is_ctx, 1.0, 0.0)
  first_edge = jnp.where(is_first, 1.0, 0.0)
  last_edge = jnp.where(is_last, 1.0, 0.0)
  h_prev = jnp.where(is_first, 0.0, pre(xprev_ref[...]))
  h_next = jnp.where(is_last, 0.0, pre(xnext_ref[...]))
  h_hi = h.astype(BF16)
  h_lo = (h - h_hi.astype(F32)).astype(BF16)
  n_tiles = BM // POOL_TILE
  row = lax.broadcasted_iota(jnp.int32, (POOL_TILE, POOL_DG), 0)
  units = [(g, w, j) for g, w in enumerate(POOL_WINDOWS) for j in range(n_tiles)]
  col_of = lambda g: slice(g * POOL_DG, (g + 1) * POOL_DG)
  row_of = lambda j: slice(j * POOL_TILE, (j + 1) * POOL_TILE)

  parts = [jnp.dot(band_ref[g],
                   jnp.concatenate([h_hi[row_of(j), col_of(g)], h_lo[row_of(j), col_of(g)]], axis=1),
                   preferred_element_type=F32) for g, w, j in units]

  pooled = []
  for (g, w, j), part in zip(units, parts):
    r0, cols = j * POOL_TILE, col_of(g)
    hg = h[row_of(j), cols]
    inner = part[:, :POOL_DG] + part[:, POOL_DG:]
    before = (h_prev[:, cols] if j == 0
              else jnp.where(is_ctx, 0.0, h[r0 - POOL_HALO:r0, cols]))
    after = (h_next[:, cols] if j == n_tiles - 1
             else jnp.where(is_ctx, 0.0, h[r0 + POOL_TILE:r0 + POOL_TILE + POOL_HALO, cols]))
    top = _edge_window_sums(jnp.concatenate([before, hg[:2 * POOL_HALO]], axis=0), w)
    bot = _edge_window_sums(jnp.concatenate([hg[-2 * POOL_HALO:], after], axis=0), w)
    win = jnp.concatenate([top, inner[POOL_HALO:-POOL_HALO], bot], axis=0)
    starts = first_edge if j == 0 else ctx_edge
    ends = last_edge if j == n_tiles - 1 else ctx_edge
    lo_cut = jnp.maximum(w // 2 - row, 0).astype(F32)
    hi_cut = jnp.maximum(row + (w - w // 2) - POOL_TILE, 0).astype(F32)
    cnt = w - lo_cut * starts - hi_cut * ends
    pooled.append((win / cnt - hg).astype(BF16))

  for (g, w, j), p in zip(units, pooled):
    og = jnp.dot(p, w_ref[0, g], preferred_element_type=F32)
    o_ref[row_of(j), col_of(g)] = (og * ps_ref[0, :, col_of(g)]).astype(BF16)


def _pool(x, ada, gains, w_pool, pool_scale, layer):
  p = layer // 2
  halo_per_blk = BM // POOL_HALO
  n_halo = N_TOK // POOL_HALO
  prev_spec = pl.BlockSpec((POOL_HALO, D), lambda i: (jnp.maximum(i * halo_per_blk - 1, 0), 0))
  next_spec = pl.BlockSpec(
      (POOL_HALO, D), lambda i: (jnp.minimum((i + 1) * halo_per_blk, n_halo - 1), 0))
  return pl.pallas_call(
      _pool_kernel,
      grid=(N_BLK,),
      in_specs=_token_specs(False, D) + [
          prev_spec, next_spec,
          _layer_spec((N_MOD, N_COND, D), layer),
          _layer_spec((4, D), layer),
          pl.BlockSpec((N_POOL_G, POOL_TILE, POOL_TILE), lambda i: (0, 0, 0),
                       pipeline_mode=pl.Buffered(1)),
          _layer_spec((N_POOL_G, POOL_DG, POOL_DG), p),
          _layer_spec((1, D), p),
      ],
      out_specs=_token_specs(False, D)[0],
      out_shape=_token_shapes(False, D, BF16)[0],
      compiler_params=_params(1),
      name="pool",
  )(x, x, x, ada, gains, _pool_bands(), w_pool, pool_scale.reshape(-1, 1, D))


def _ffn_kernel(*refs, n_x, n_out, has_wo, cast_next):
  x_refs, refs = refs[:n_x], refs[n_x:]
  if cast_next:
    (wup_dst, wdn_dst), refs = refs[-2:], refs[:-2]
  o_refs, refs = refs[-n_out:], refs[:-n_out]
  if cast_next:
    (wup_src, wdn_src), refs = refs[-2:], refs[:-2]
  if has_wo:
    mix_ref, ada_ref, g_ref, wo_ref, wup_ref, wdn_ref = refs
  else:
    mix_ref, ada_ref, g_ref, wup_ref, wdn_ref = refs
  i = pl.program_id(0)
  is_ctx = i < N_CTX_BLK
  _, _, gate_m, shift_f, scale_f, gate_f = _mod_rows(ada_ref, _cond_row(i))
  if cast_next:
    wup_dst[...] = wup_src[0].astype(BF16)
    wdn_dst[...] = wdn_src[0].astype(BF16)
  groups = [slice(r * (BM // FFN_GROUPS), (r + 1) * (BM // FFN_GROUPS))
            for r in range(FFN_GROUPS)]
  if has_wo:
    a = [jnp.dot(mix_ref[rows, :], wo_ref[0], preferred_element_type=F32) for rows in groups]
  else:
    a = [mix_ref[rows, :].astype(F32) for rows in groups]
  x1 = [_token_load(is_ctx, x_refs, rows) + gate_m * (_rms_scale(a_r) * g_ref[0, 1:2])
        for rows, a_r in zip(groups, a)]
  h = [((_rms_scale(x1_r) * g_ref[0, 2:3]) * (1.0 + scale_f) + shift_f).astype(BF16)
       for x1_r in x1]
  gu = [jnp.dot(h_r, wup_ref[...], preferred_element_type=F32) for h_r in h]
  act = [(_silu(gu_r[:, :D_FF]) * gu_r[:, D_FF:]).astype(BF16) for gu_r in gu]
  y = [jnp.dot(act_r, wdn_ref[...], preferred_element_type=F32) for act_r in act]
  for rows, x1_r, y_r in zip(groups, x1, y):
    _token_store(is_ctx, o_refs, x1_r + gate_f * (_rms_scale(y_r) * g_ref[0, 3:4]), rows)


def _ffn(xs, mix, ada, gains, w_o, w_up, w_down, next_w, layer, split_out):
  has_wo = w_o is not None
  cast_next = next_w is not None
  resident = lambda shape: pl.BlockSpec(shape, lambda i: (0, 0), pipeline_mode=pl.Buffered(1))
  in_specs = _token_specs(len(xs) == 2, D) + _token_specs(False, D) + [
      _layer_spec((N_MOD, N_COND, D), layer),
      _layer_spec((4, D), layer),
  ]
  args = list(xs) + [mix, ada, gains]
  if has_wo:
    in_specs.append(_layer_spec((Q_W, D), layer // 2))
    args.append(w_o)
  in_specs += [resident((D, 2 * D_FF)), resident((D_FF, D))]
  args += [w_up, w_down]
  n_out = 2 if split_out else 1
  out_specs = _token_specs(split_out, D)
  out_shape = _token_shapes(split_out, D, F32)
  if cast_next:
    up_rows = D // N_BLK
    dn_blks = N_BLK // 2
    dn_rows = D_FF // dn_blks
    in_specs += [
        pl.BlockSpec((1, up_rows, 2 * D_FF), lambda i: (layer + 1, i, 0)),
        pl.BlockSpec((1, dn_rows, D), lambda i: (layer + 1, jnp.minimum(i, dn_blks - 1), 0))]
    args += list(next_w)
    out_specs += [pl.BlockSpec((up_rows, 2 * D_FF), lambda i: (i, 0)),
                  pl.BlockSpec((dn_rows, D), lambda i: (jnp.minimum(i, dn_blks - 1), 0))]
    out_shape += [jax.ShapeDtypeStruct((D, 2 * D_FF), BF16),
                  jax.ShapeDtypeStruct((D_FF, D), BF16)]
  outs = pl.pallas_call(
      functools.partial(_ffn_kernel, n_x=len(xs), n_out=n_out, has_wo=has_wo,
                        cast_next=cast_next),
      grid=(N_BLK,),
      in_specs=in_specs,
      out_specs=out_specs,
      out_shape=out_shape,
      compiler_params=_params(1),
      name="ffn_attn" if has_wo else "ffn_pool",
  )(*args)
  return outs[:n_out], outs[n_out:]


def _rope_tables():
  pos = np.arange(LAT_T)
  row = (pos // GRID_W).astype(np.float32)
  col = (pos % GRID_W).astype(np.float32)
  inv = np.float32(ROPE_BASE) ** (-np.arange(ROPE_PAIRS, dtype=np.float32) / np.float32(ROPE_PAIRS))
  ar = row[:, None] * inv
  ac = col[:, None] * inv
  cos = np.concatenate([np.cos(ar), np.cos(ac), np.cos(ar), np.cos(ac)], axis=1)
  sin = np.concatenate([-np.sin(ar), -np.sin(ac), np.sin(ar), np.sin(ac)], axis=1)
  return jnp.asarray(np.concatenate([cos, sin], axis=1), dtype=F32)


def _paired_qk_columns(w_qkv, qk_gains):
  def paired(a):
    blocks = a.reshape(a.shape[:-1] + (-1, 2, 2, ROPE_PAIRS))
    return jnp.swapaxes(blocks, -3, -2).reshape(a.shape)

  w = jnp.concatenate([paired(w_qkv[..., :Q_W + KV_W]), w_qkv[..., Q_W + KV_W:]], axis=-1)
  return w, paired(qk_gains)


def kernel(x_prompt, x_sample, c, cache_k, cache_v, c_ctx, w_ada, b_ada, norm_gains,
           w_qkv, qk_gains, w_o, w_pool, pool_scale, w_up, w_down):
  xs = [x_prompt.reshape(N_CTX_TOK, D), x_sample.reshape(N_LAT_TOK, D)]
  cond = jnp.concatenate(
      [c_ctx[None, :], c, jnp.zeros((N_COND - 1 - N_LAT_SEQ, D), F32)], axis=0)
  ada = _modulation(cond, w_ada, b_ada)
  rope_cs = _rope_tables()
  w_qkv_p, qk_gains_p = _paired_qk_columns(w_qkv, qk_gains)
  w_qkv_b = w_qkv_p.astype(BF16)
  w_o_b = w_o.astype(BF16)
  w_pool_b = w_pool.astype(BF16)
  ffn_w = [w_up[0].astype(BF16), w_down[0].astype(BF16)]

  new_kv = None
  for layer in range(DEPTH):
    last = layer == DEPTH - 1
    next_w = None if last else (w_up, w_down)
    if layer % 2 == 0:
      q, k, v, *new_kv = _qkv(xs, ada, norm_gains, w_qkv_b, qk_gains_p, rope_cs, new_kv, layer)
      mix = _attn_latent(q, k, v, cache_k, cache_v, _attn_context(q, k, v), layer // 2)
      xs, ffn_w = _ffn(xs, mix, ada, norm_gains, w_o_b, *ffn_w, next_w, layer, last)
    else:
      mix = _pool(xs[0], ada, norm_gains, w_pool_b, pool_scale, layer)
      xs, ffn_w = _ffn(xs, mix, ada, norm_gains, None, *ffn_w, next_w, layer, last)

  out_kv = (N_CTX_SEQ, N_ATTN, CTX_T, N_KV, HD)
  return (xs[0].reshape(N_CTX_SEQ, CTX_T, D), xs[1].reshape(N_LAT_SEQ, LAT_T, D),
          new_kv[0].reshape(out_kv), new_kv[1].reshape(out_kv))
```

```python
import functools
import math

import jax
import jax.numpy as jnp
import numpy as np
from jax import lax
from jax.experimental import pallas as pl
from jax.experimental.pallas import tpu as pltpu

F32 = jnp.float32
BF16 = jnp.bfloat16

D = 1024
N_CTX_SEQ = 32
CTX_T = 256
N_LAT_SEQ = 4
LAT_T = 2048
PAST = 512
DEPTH = 4
N_ATTN = 2
GRID_W = 64
HD = 128
N_HEADS = 8
N_KV = 2
GQA = N_HEADS // N_KV
Q_W = N_HEADS * HD
KV_W = N_KV * HD
QKV_W = Q_W + 2 * KV_W
ROPE_PAIRS = HD // 4
ROPE_BASE = 10000.0
POOL_WINDOWS = (2, 4, 8, 16)
N_POOL_G = len(POOL_WINDOWS)
POOL_DG = D // N_POOL_G
D_FF = 2816
N_MOD = 6
EPS = 1e-6

N_CTX_TOK = N_CTX_SEQ * CTX_T
N_LAT_TOK = N_LAT_SEQ * LAT_T
N_TOK = N_CTX_TOK + N_LAT_TOK
N_COND = 8
Q_SCALE = math.log2(math.e) / math.sqrt(HD)

VMEM_LIMIT = 56 * 1024 * 1024

FFN_BM = 512
QKV_BM = 512
POOL_BM = 1024
FFN_GROUPS = 2
POOL_TILE = CTX_T
POOL_HALO = max(POOL_WINDOWS) // 2
ATT_TQ = 1024
ATT_CHUNK = 256
CTX_NB = 4


def _n_ctx_blk(bm):
  return N_CTX_TOK // bm


def _token_specs(split, width, bm):
  n_ctx = _n_ctx_blk(bm)
  if not split:
    return [pl.BlockSpec((bm, width), lambda i: (i, 0))]
  return [pl.BlockSpec((bm, width), lambda i: (jnp.minimum(i, n_ctx - 1), 0)),
          pl.BlockSpec((bm, width), lambda i: (jnp.maximum(i - n_ctx, 0), 0))]


def _token_shapes(split, width, dtype):
  if not split:
    return [jax.ShapeDtypeStruct((N_TOK, width), dtype)]
  return [jax.ShapeDtypeStruct((N_CTX_TOK, width), dtype),
          jax.ShapeDtypeStruct((N_LAT_TOK, width), dtype)]


def _token_load(is_ctx, refs, rows=slice(None)):
  if len(refs) == 1:
    return refs[0][rows, :]
  return jnp.where(is_ctx, refs[0][rows, :], refs[1][rows, :])


def _token_store(is_ctx, refs, value, rows=slice(None)):
  if len(refs) == 1:
    refs[0][rows, :] = value
    return

  @pl.when(is_ctx)
  def _():
    refs[0][rows, :] = value

  @pl.when(jnp.logical_not(is_ctx))
  def _():
    refs[1][rows, :] = value


def _lat_seq_pos(i, bm):
  lat_blk = jnp.maximum(i - _n_ctx_blk(bm), 0)
  return lat_blk // (LAT_T // bm), lat_blk % (LAT_T // bm)


def _cond_row(i, bm):
  return jnp.where(i < _n_ctx_blk(bm), 0, 1 + _lat_seq_pos(i, bm)[0])


def _mod_rows(ada_ref, row):
  return [ada_ref[0, m, pl.ds(row, 1), :] for m in range(N_MOD)]


def _layer_spec(shape, layer):
  zeros = (0,) * len(shape)
  return pl.BlockSpec((1,) + shape, lambda *_: (layer,) + zeros, pipeline_mode=pl.Buffered(1))


def _rms_scale(x):
  return x * lax.rsqrt(jnp.mean(x * x, axis=-1, keepdims=True) + EPS)


def _silu(x):
  return x / (1.0 + jnp.exp(-x))


def _params(n_grid_dims):
  return pltpu.CompilerParams(
      dimension_semantics=("arbitrary",) * n_grid_dims, vmem_limit_bytes=VMEM_LIMIT)


def _mod_kernel(cond_ref, w_ref, b_ref, o_ref):
  s = _silu(cond_ref[...]).astype(BF16)
  o_ref[0, 0] = jnp.dot(s, w_ref[0].astype(BF16), preferred_element_type=F32) + b_ref[0, 0]


def _modulation(cond, w_ada, b_ada):
  return pl.pallas_call(
      _mod_kernel,
      grid=(DEPTH, N_MOD),
      in_specs=[
          pl.BlockSpec((N_COND, D), lambda l, m: (0, 0)),
          pl.BlockSpec((1, D, D), lambda l, m: (l, 0, m)),
          pl.BlockSpec((1, 1, 1, D), lambda l, m: (l, m, 0, 0)),
      ],
      out_specs=pl.BlockSpec((1, 1, N_COND, D), lambda l, m: (l, m, 0, 0)),
      out_shape=jax.ShapeDtypeStruct((DEPTH, N_MOD, N_COND, D), F32),
      compiler_params=_params(2),
      name="modulation",
  )(cond, w_ada, b_ada.reshape(DEPTH, N_MOD, 1, D))


def _swap_middle(x):
  blk = lax.broadcasted_iota(jnp.int32, x.shape, 1) // ROPE_PAIRS
  return jnp.where(blk == 1, pltpu.roll(x, HD - ROPE_PAIRS, 1),
                   jnp.where(blk == 2, pltpu.roll(x, ROPE_PAIRS, 1), x))


def _qkv_kernel(*refs, n_x):
  x_refs, refs = refs[:n_x], refs[n_x:]
  ada_ref, g_ref, w_ref, qkg_ref, cs_ref = refs[:5]
  q_ref, k_ref, v_ref, nk_ref, nv_ref = refs[-5:]
  i = pl.program_id(0)
  is_ctx = i < _n_ctx_blk(QKV_BM)
  shift, scale = _mod_rows(ada_ref, _cond_row(i, QKV_BM))[:2]
  gq = qkg_ref[0, 0:1] * Q_SCALE
  gk = qkg_ref[0, 1:2]
  groups = [slice(s * CTX_T, (s + 1) * CTX_T) for s in range(QKV_BM // CTX_T)]

  def body(x_ref, latent):
    h = [((_rms_scale(x_ref[rows, :]) * g_ref[0, 0:1]) * (1.0 + scale) + shift).astype(BF16)
         for rows in groups]
    qkv = [jnp.dot(h_s, w_ref[0], preferred_element_type=F32) for h_s in h]
    for s, (rows, qkv_s) in enumerate(zip(groups, qkv)):
      if latent:
        cos = cs_ref[rows, :HD]
        sin = cs_ref[rows, HD:]
        rope = lambda x: x * cos + pltpu.roll(x, HD // 2, 1) * sin
      else:
        rope = lambda x: x
      for hd in range(N_HEADS):
        cols = slice(hd * HD, (hd + 1) * HD)
        q_ref[rows, cols] = rope(_rms_scale(qkv_s[:, cols]) * gq).astype(BF16)
      v = qkv_s[:, Q_W + KV_W:]
      v_ref[rows, :] = v.astype(BF16)
      for hd in range(N_KV):
        cols = slice(hd * HD, (hd + 1) * HD)
        k = rope(_rms_scale(qkv_s[:, Q_W + hd * HD:Q_W + (hd + 1) * HD]) * gk)
        k_ref[rows, cols] = k.astype(BF16)
        if not latent:
          nk_ref[s, 0, pl.ds(hd, CTX_T, stride=N_KV), :] = _swap_middle(k)
          nv_ref[s, 0, pl.ds(hd, CTX_T, stride=N_KV), :] = v[:, cols]

  @pl.when(is_ctx)
  def _():
    body(x_refs[0], latent=False)

  @pl.when(jnp.logical_not(is_ctx))
  def _():
    body(x_refs[-1], latent=True)


def _qkv(xs, ada, gains, w_qkv, qk_gains, rope_cs, new_kv, layer):
  a = layer // 2
  bm = QKV_BM
  new_spec = pl.BlockSpec(
      (bm // CTX_T, 1, CTX_T * N_KV, HD),
      lambda i: (jnp.minimum(i, _n_ctx_blk(bm) - 1), a, 0, 0))
  new_shape = jax.ShapeDtypeStruct((N_CTX_SEQ, N_ATTN, CTX_T * N_KV, HD), F32)
  args = list(xs) + [ada, gains, w_qkv, qk_gains, rope_cs]
  in_specs = _token_specs(len(xs) == 2, D, bm) + [
      _layer_spec((N_MOD, N_COND, D), layer),
      _layer_spec((4, D), layer),
      _layer_spec((D, QKV_W), a),
      _layer_spec((2, HD), a),
      pl.BlockSpec((bm, 2 * HD), lambda i: (_lat_seq_pos(i, bm)[1], 0)),
  ]
  aliases = {}
  if new_kv is not None:
    aliases = {len(args): 3, len(args) + 1: 4}
    args += list(new_kv)
    in_specs += [pl.BlockSpec(memory_space=pl.ANY)] * 2
  return pl.pallas_call(
      functools.partial(_qkv_kernel, n_x=len(xs)),
      grid=(N_TOK // bm,),
      in_specs=in_specs,
      out_specs=(_token_specs(False, Q_W, bm) + _token_specs(False, KV_W, bm) * 2
                 + [new_spec] * 2),
      out_shape=(_token_shapes(False, Q_W, BF16) + _token_shapes(False, KV_W, BF16) * 2
                 + [new_shape] * 2),
      input_output_aliases=aliases,
      compiler_params=_params(1),
      name="qkv",
  )(*args)


def _nt_dot(a, b):
  return lax.dot_general(a, b, (((1,), (1,)), ((), ())), preferred_element_type=F32)


def _with_ones(v):
  return jnp.concatenate([v, jnp.ones_like(v)], axis=1)


def _softmax_pv(s, v1):
  p = jnp.exp2(s - jnp.max(s, axis=-1, keepdims=True)).astype(BF16)
  oa = jnp.dot(p, v1, preferred_element_type=F32)
  return oa[:, :HD] / oa[:, HD:]


def _attn_ctx_kernel(q_ref, k_ref, v_ref, o_ref):
  for b in range(CTX_NB):
    rows = slice(b * CTX_T, (b + 1) * CTX_T)
    for h in range(N_KV):
      kv_cols = slice(h * HD, (h + 1) * HD)
      v1 = _with_ones(v_ref[rows, kv_cols])
      for g in range(GQA):
        cols = slice((h * GQA + g) * HD, (h * GQA + g + 1) * HD)
        s = _nt_dot(q_ref[rows, cols], k_ref[rows, kv_cols])
        o_ref[rows, cols] = _softmax_pv(s, v1).astype(BF16)


def _attn_context(q, k, v):
  bm = CTX_NB * CTX_T
  row = lambda i: (i, 0)
  return pl.pallas_call(
      _attn_ctx_kernel,
      grid=(N_CTX_TOK // bm,),
      in_specs=[
          pl.BlockSpec((bm, Q_W), row),
          pl.BlockSpec((bm, KV_W), row),
          pl.BlockSpec((bm, KV_W), row),
      ],
      out_specs=pl.BlockSpec((bm, Q_W), row),
      out_shape=jax.ShapeDtypeStruct((N_TOK, Q_W), BF16),
      compiler_params=_params(1),
      name="attn_context",
  )(q, k, v)


def _attn_lat_kernel(q_ref, k_ref, v_ref, ck_ref, cv_ref, mix_in, o_ref, kcat, vcat):
  del mix_in

  @pl.when(pl.program_id(2) == 0)
  def _():
    kcat[:LAT_T] = k_ref[...]
    kcat[LAT_T:] = _swap_middle(ck_ref[0, 0]).astype(BF16)
    vcat[:LAT_T] = _with_ones(v_ref[...])
    vcat[LAT_T:] = _with_ones(cv_ref[0, 0].astype(BF16))

  for g in range(GQA):
    cols = slice(g * HD, (g + 1) * HD)
    for r in range(ATT_TQ // ATT_CHUNK):
      rows = slice(r * ATT_CHUNK, (r + 1) * ATT_CHUNK)
      s = _nt_dot(q_ref[rows, cols], kcat[...])
      o_ref[rows, cols] = _softmax_pv(s, vcat[...]).astype(BF16)


def _attn_latent(q, k, v, cache_k, cache_v, mix, a):
  nq = LAT_T // ATT_TQ
  q0 = N_CTX_TOK // ATT_TQ
  kv0 = N_CTX_TOK // LAT_T
  ck = cache_k.reshape(N_LAT_SEQ, N_ATTN, PAST, KV_W)
  cv = cache_v.reshape(N_LAT_SEQ, N_ATTN, PAST, KV_W)
  return pl.pallas_call(
      _attn_lat_kernel,
      grid=(N_LAT_SEQ, N_KV, nq),
      in_specs=[
          pl.BlockSpec((ATT_TQ, GQA * HD), lambda b, h, j: (q0 + b * nq + j, h)),
          pl.BlockSpec((LAT_T, HD), lambda b, h, j: (kv0 + b, h)),
          pl.BlockSpec((LAT_T, HD), lambda b, h, j: (kv0 + b, h)),
          pl.BlockSpec((1, 1, PAST, HD), lambda b, h, j: (b, a, 0, h)),
          pl.BlockSpec((1, 1, PAST, HD), lambda b, h, j: (b, a, 0, h)),
          pl.BlockSpec(memory_space=pl.ANY),
      ],
      out_specs=pl.BlockSpec((ATT_TQ, GQA * HD), lambda b, h, j: (q0 + b * nq + j, h)),
      out_shape=jax.ShapeDtypeStruct((N_TOK, Q_W), BF16),
      scratch_shapes=[pltpu.VMEM((LAT_T + PAST, HD), BF16),
                      pltpu.VMEM((LAT_T + PAST, 2 * HD), BF16)],
      input_output_aliases={5: 0},
      compiler_params=_params(3),
      name="attn_latent",
  )(q, k, v, ck, cv, mix)


def _edge_window_sums(slab, w):
  rows = slab.shape[0]
  back, fwd, m = slab, slab, 1
  while m < w // 2:
    back = back + pltpu.roll(back, m, 0)
    fwd = fwd + pltpu.roll(fwd, rows - m, 0)
    m *= 2
  return (pltpu.roll(back, 1, 0) + fwd)[POOL_HALO:2 * POOL_HALO]


def _pool_bands():
  t = np.arange(POOL_TILE)[:, None]
  j = np.arange(POOL_TILE)[None, :]
  bands = [(j >= t - w // 2) & (j < t + w - w // 2) for w in POOL_WINDOWS]
  return jnp.asarray(np.stack(bands), dtype=BF16)


def _pool_kernel(x_ref, xprev_ref, xnext_ref, ada_ref, g_ref, band_ref, w_ref, ps_ref, o_ref):
  i = pl.program_id(0)
  is_ctx = i < _n_ctx_blk(POOL_BM)
  shift, scale = _mod_rows(ada_ref, _cond_row(i, POOL_BM))[:2]
  pre = lambda x: (_rms_scale(x) * g_ref[0, 0:1]) * (1.0 + scale) + shift
  h = pre(x_ref[...])
  seq_pos = _lat_seq_pos(i, POOL_BM)[1]
  is_first = jnp.logical_or(is_ctx, seq_pos == 0)
  is_last = jnp.logical_or(is_ctx, seq_pos == LAT_T // POOL_BM - 1)
  ctx_edge = jnp.where(is_ctx, 1.0, 0.0)
  first_edge = jnp.where(is_first, 1.0, 0.0)
  last_edge = jnp.where(is_last, 1.0, 0.0)
  h_prev = jnp.where(is_first, 0.0, pre(xprev_ref[...]))
  h_next = jnp.where(is_last, 0.0, pre(xnext_ref[...]))
  h_hi = h.astype(BF16)
  h_lo = (h - h_hi.astype(F32)).astype(BF16)
  n_tiles = POOL_BM // POOL_TILE
  row = lax.broadcasted_iota(jnp.int32, (POOL_TILE, POOL_DG), 0)
  units = [(g, w, j) for g, w in enumerate(POOL_WINDOWS) for j in range(n_tiles)]
  col_of = lambda g: slice(g * POOL_DG, (g + 1) * POOL_DG)
  row_of = lambda j: slice(j * POOL_TILE, (j + 1) * POOL_TILE)

  parts = [jnp.dot(band_ref[g],
                   jnp.concatenate([h_hi[row_of(j), col_of(g)], h_lo[row_of(j), col_of(g)]], axis=1),
                   preferred_element_type=F32) for g, w, j in units]

  pooled = []
  for (g, w, j), part in zip(units, parts):
    r0, cols = j * POOL_TILE, col_of(g)
    hg = h[row_of(j), cols]
    inner = part[:, :POOL_DG] + part[:, POOL_DG:]
    before = (h_prev[:, cols] if j == 0
              else jnp.where(is_ctx, 0.0, h[r0 - POOL_HALO:r0, cols]))
    after = (h_next[:, cols] if j == n_tiles - 1
             else jnp.where(is_ctx, 0.0, h[r0 + POOL_TILE:r0 + POOL_TILE + POOL_HALO, cols]))
    top = _edge_window_sums(jnp.concatenate([before, hg[:2 * POOL_HALO]], axis=0), w)
    bot = _edge_window_sums(jnp.concatenate([hg[-2 * POOL_HALO:], after], axis=0), w)
    win = jnp.concatenate([top, inner[POOL_HALO:-POOL_HALO], bot], axis=0)
    starts = first_edge if j == 0 else ctx_edge
    ends = last_edge if j == n_tiles - 1 else ctx_edge
    lo_cut = jnp.maximum(w // 2 - row, 0).astype(F32)
    hi_cut = jnp.maximum(row + (w - w // 2) - POOL_TILE, 0).astype(F32)
    cnt = w - lo_cut * starts - hi_cut * ends
    pooled.append((win / cnt - hg).astype(BF16))

  for (g, w, j), p in zip(units, pooled):
    og = jnp.dot(p, w_ref[0, g], preferred_element_type=F32)
    o_ref[row_of(j), col_of(g)] = (og * ps_ref[0, :, col_of(g)]).astype(BF16)


def _pool(x, ada, gains, w_pool, pool_scale, layer):
  p = layer // 2
  bm = POOL_BM
  halo_per_blk = bm // POOL_HALO
  n_halo = N_TOK // POOL_HALO
  prev_spec = pl.BlockSpec((POOL_HALO, D), lambda i: (jnp.maximum(i * halo_per_blk - 1, 0), 0))
  next_spec = pl.BlockSpec(
      (POOL_HALO, D), lambda i: (jnp.minimum((i + 1) * halo_per_blk, n_halo - 1), 0))
  return pl.pallas_call(
      _pool_kernel,
      grid=(N_TOK // bm,),
      in_specs=_token_specs(False, D, bm) + [
          prev_spec, next_spec,
          _layer_spec((N_MOD, N_COND, D), layer),
          _layer_spec((4, D), layer),
          pl.BlockSpec((N_POOL_G, POOL_TILE, POOL_TILE), lambda i: (0, 0, 0),
                       pipeline_mode=pl.Buffered(1)),
          _layer_spec((N_POOL_G, POOL_DG, POOL_DG), p),
          _layer_spec((1, D), p),
      ],
      out_specs=_token_specs(False, D, bm)[0],
      out_shape=_token_shapes(False, D, BF16)[0],
      compiler_params=_params(1),
      name="pool",
  )(x, x, x, ada, gains, _pool_bands(), w_pool, pool_scale.reshape(-1, 1, D))


def _ffn_kernel(*refs, n_x, n_out, has_wo, cast_next):
  x_refs, refs = refs[:n_x], refs[n_x:]
  if cast_next:
    (wup_dst, wdn_dst), refs = refs[-2:], refs[:-2]
  o_refs, refs = refs[-n_out:], refs[:-n_out]
  if cast_next:
    (wup_src, wdn_src), refs = refs[-2:], refs[:-2]
  if has_wo:
    mix_ref, ada_ref, g_ref, wo_ref, wup_ref, wdn_ref = refs
  else:
    mix_ref, ada_ref, g_ref, wup_ref, wdn_ref = refs
  i = pl.program_id(0)
  is_ctx = i < _n_ctx_blk(FFN_BM)
  _, _, gate_m, shift_f, scale_f, gate_f = _mod_rows(ada_ref, _cond_row(i, FFN_BM))
  if cast_next:
    wup_dst[...] = wup_src[0].astype(BF16)
    wdn_dst[...] = wdn_src[0].astype(BF16)
  groups = [slice(r * (FFN_BM // FFN_GROUPS), (r + 1) * (FFN_BM // FFN_GROUPS))
            for r in range(FFN_GROUPS)]
  if has_wo:
    a = [jnp.dot(mix_ref[rows, :], wo_ref[0], preferred_element_type=F32) for rows in groups]
  else:
    a = [mix_ref[rows, :].astype(F32) for rows in groups]
  x1 = [_token_load(is_ctx, x_refs, rows) + gate_m * (_rms_scale(a_r) * g_ref[0, 1:2])
        for rows, a_r in zip(groups, a)]
  h = [((_rms_scale(x1_r) * g_ref[0, 2:3]) * (1.0 + scale_f) + shift_f).astype(BF16)
       for x1_r in x1]
  gu, y = [], []
  for r in range(FFN_GROUPS + 1):
    if r < FFN_GROUPS:
      gu.append(jnp.dot(h[r], wup_ref[...], preferred_element_type=F32))
    if r > 0:
      act = (_silu(gu[r - 1][:, :D_FF]) * gu[r - 1][:, D_FF:]).astype(BF16)
      y.append(jnp.dot(act, wdn_ref[...], preferred_element_type=F32))
  for rows, x1_r, y_r in zip(groups, x1, y):
    _token_store(is_ctx, o_refs, x1_r + gate_f * (_rms_scale(y_r) * g_ref[0, 3:4]), rows)


def _ffn(xs, mix, ada, gains, w_o, w_up, w_down, next_w, layer, split_out):
  has_wo = w_o is not None
  cast_next = next_w is not None
  resident = lambda shape: pl.BlockSpec(shape, lambda i: (0, 0), pipeline_mode=pl.Buffered(1))
  bm = FFN_BM
  n_blk = N_TOK // bm
  in_specs = _token_specs(len(xs) == 2, D, bm) + _token_specs(False, D, bm) + [
      _layer_spec((N_MOD, N_COND, D), layer),
      _layer_spec((4, D), layer),
  ]
  args = list(xs) + [mix, ada, gains]
  if has_wo:
    in_specs.append(_layer_spec((Q_W, D), layer // 2))
    args.append(w_o)
  in_specs += [resident((D, 2 * D_FF)), resident((D_FF, D))]
  args += [w_up, w_down]
  n_out = 2 if split_out else 1
  out_specs = _token_specs(split_out, D, bm)
  out_shape = _token_shapes(split_out, D, F32)
  if cast_next:
    up_rows = D // n_blk
    dn_blks = n_blk // 2
    dn_rows = D_FF // dn_blks
    in_specs += [
        pl.BlockSpec((1, up_rows, 2 * D_FF), lambda i: (layer + 1, i, 0)),
        pl.BlockSpec((1, dn_rows, D), lambda i: (layer + 1, jnp.minimum(i, dn_blks - 1), 0))]
    args += list(next_w)
    out_specs += [pl.BlockSpec((up_rows, 2 * D_FF), lambda i: (i, 0)),
                  pl.BlockSpec((dn_rows, D), lambda i: (jnp.minimum(i, dn_blks - 1), 0))]
    out_shape += [jax.ShapeDtypeStruct((D, 2 * D_FF), BF16),
                  jax.ShapeDtypeStruct((D_FF, D), BF16)]
  outs = pl.pallas_call(
      functools.partial(_ffn_kernel, n_x=len(xs), n_out=n_out, has_wo=has_wo,
                        cast_next=cast_next),
      grid=(n_blk,),
      in_specs=in_specs,
      out_specs=out_specs,
      out_shape=out_shape,
      compiler_params=_params(1),
      name="ffn_attn" if has_wo else "ffn_pool",
  )(*args)
  return outs[:n_out], outs[n_out:]


def _rope_tables():
  pos = np.arange(LAT_T)
  row = (pos // GRID_W).astype(np.float32)
  col = (pos % GRID_W).astype(np.float32)
  inv = np.float32(ROPE_BASE) ** (-np.arange(ROPE_PAIRS, dtype=np.float32) / np.float32(ROPE_PAIRS))
  ar = row[:, None] * inv
  ac = col[:, None] * inv
  cos = np.concatenate([np.cos(ar), np.cos(ac), np.cos(ar), np.cos(ac)], axis=1)
  sin = np.concatenate([-np.sin(ar), -np.sin(ac), np.sin(ar), np.sin(ac)], axis=1)
  return jnp.asarray(np.concatenate([cos, sin], axis=1), dtype=F32)


def _paired_qk_columns(w_qkv, qk_gains):
  def paired(a):
    blocks = a.reshape(a.shape[:-1] + (-1, 2, 2, ROPE_PAIRS))
    return jnp.swapaxes(blocks, -3, -2).reshape(a.shape)

  w = jnp.concatenate([paired(w_qkv[..., :Q_W + KV_W]), w_qkv[..., Q_W + KV_W:]], axis=-1)
  return w, paired(qk_gains)


def kernel(x_prompt, x_sample, c, cache_k, cache_v, c_ctx, w_ada, b_ada, norm_gains,
           w_qkv, qk_gains, w_o, w_pool, pool_scale, w_up, w_down):
  xs = [x_prompt.reshape(N_CTX_TOK, D), x_sample.reshape(N_LAT_TOK, D)]
  cond = jnp.concatenate(
      [c_ctx[None, :], c, jnp.zeros((N_COND - 1 - N_LAT_SEQ, D), F32)], axis=0)
  ada = _modulation(cond, w_ada, b_ada)
  rope_cs = _rope_tables()
  w_qkv_p, qk_gains_p = _paired_qk_columns(w_qkv, qk_gains)
  w_qkv_b = w_qkv_p.astype(BF16)
  w_o_b = w_o.astype(BF16)
  w_pool_b = w_pool.astype(BF16)
  ffn_w = [w_up[0].astype(BF16), w_down[0].astype(BF16)]

  new_kv = None
  for layer in range(DEPTH):
    last = layer == DEPTH - 1
    next_w = None if last else (w_up, w_down)
    if layer % 2 == 0:
      q, k, v, *new_kv = _qkv(xs, ada, norm_gains, w_qkv_b, qk_gains_p, rope_cs, new_kv, layer)
      mix = _attn_latent(q, k, v, cache_k, cache_v, _attn_context(q, k, v), layer // 2)
      xs, ffn_w = _ffn(xs, mix, ada, norm_gains, w_o_b, *ffn_w, next_w, layer, last)
    else:
      mix = _pool(xs[0], ada, norm_gains, w_pool_b, pool_scale, layer)
      xs, ffn_w = _ffn(xs, mix, ada, norm_gains, None, *ffn_w, next_w, layer, last)

  out_kv = (N_CTX_SEQ, N_ATTN, CTX_T, N_KV, HD)
  return (xs[0].reshape(N_CTX_SEQ, CTX_T, D), xs[1].reshape(N_LAT_SEQ, LAT_T, D),
          new_kv[0].reshape(out_kv), new_kv[1].reshape(out_kv))
```

```python
import functools
import math

import jax
import jax.numpy as jnp
import numpy as np
from jax import lax
from jax.experimental import pallas as pl
from jax.experimental.pallas import tpu as pltpu

F32 = jnp.float32
BF16 = jnp.bfloat16

D = 1024
N_CTX_SEQ = 32
CTX_T = 256
N_LAT_SEQ = 4
LAT_T = 2048
PAST = 512
DEPTH = 4
N_ATTN = 2
GRID_W = 64
HD = 128
N_HEADS = 8
N_KV = 2
GQA = N_HEADS // N_KV
Q_W = N_HEADS * HD
KV_W = N_KV * HD
QKV_W = Q_W + 2 * KV_W
ROPE_PAIRS = HD // 4
ROPE_BASE = 10000.0
POOL_WINDOWS = (2, 4, 8, 16)
N_POOL_G = len(POOL_WINDOWS)
POOL_DG = D // N_POOL_G
D_FF = 2816
N_MOD = 6
EPS = 1e-6

N_CTX_TOK = N_CTX_SEQ * CTX_T
N_LAT_TOK = N_LAT_SEQ * LAT_T
N_TOK = N_CTX_TOK + N_LAT_TOK
N_COND = 8
Q_SCALE = math.log2(math.e) / math.sqrt(HD)

VMEM_LIMIT = 56 * 1024 * 1024

FFN_BM = 512
QKV_BM = 512
POOL_BM = 1024
FFN_GROUPS = 2
POOL_TILE = CTX_T
POOL_HALO = max(POOL_WINDOWS) // 2
ATT_TQ = 1024
ATT_CHUNK = 256
CTX_NB = 4


def _n_ctx_blk(bm):
  return N_CTX_TOK // bm


def _token_specs(split, width, bm):
  n_ctx = _n_ctx_blk(bm)
  if not split:
    return [pl.BlockSpec((bm, width), lambda i: (i, 0))]
  return [pl.BlockSpec((bm, width), lambda i: (jnp.minimum(i, n_ctx - 1), 0)),
          pl.BlockSpec((bm, width), lambda i: (jnp.maximum(i - n_ctx, 0), 0))]


def _token_shapes(split, width, dtype):
  if not split:
    return [jax.ShapeDtypeStruct((N_TOK, width), dtype)]
  return [jax.ShapeDtypeStruct((N_CTX_TOK, width), dtype),
          jax.ShapeDtypeStruct((N_LAT_TOK, width), dtype)]


def _token_load(is_ctx, refs, rows=slice(None)):
  if len(refs) == 1:
    return refs[0][rows, :]
  return jnp.where(is_ctx, refs[0][rows, :], refs[1][rows, :])


def _token_store(is_ctx, refs, value, rows=slice(None)):
  if len(refs) == 1:
    refs[0][rows, :] = value
    return

  @pl.when(is_ctx)
  def _():
    refs[0][rows, :] = value

  @pl.when(jnp.logical_not(is_ctx))
  def _():
    refs[1][rows, :] = value


def _lat_seq_pos(i, bm):
  lat_blk = jnp.maximum(i - _n_ctx_blk(bm), 0)
  return lat_blk // (LAT_T // bm), lat_blk % (LAT_T // bm)


def _cond_row(i, bm):
  return jnp.where(i < _n_ctx_blk(bm), 0, 1 + _lat_seq_pos(i, bm)[0])


def _mod_rows(ada_ref, row):
  return [ada_ref[0, m, pl.ds(row, 1), :] for m in range(N_MOD)]


def _layer_spec(shape, layer):
  zeros = (0,) * len(shape)
  return pl.BlockSpec((1,) + shape, lambda *_: (layer,) + zeros, pipeline_mode=pl.Buffered(1))


def _rms_scale(x):
  return x * lax.rsqrt(jnp.mean(x * x, axis=-1, keepdims=True) + EPS)


def _silu(x):
  return x / (1.0 + jnp.exp(-x))


BF16_SUBLANES = 16


def _ffn_weight_cast_specs(layer, n_steps, step_of):
  up_rows = D // n_steps
  dn_blks = n_steps
  while (D_FF // dn_blks) % BF16_SUBLANES:
    dn_blks //= 2
  dn_rows = D_FF // dn_blks
  up_blk = lambda *g: step_of(*g)
  dn_blk = lambda *g: jnp.minimum(step_of(*g), dn_blks - 1)
  in_specs = [pl.BlockSpec((1, up_rows, 2 * D_FF), lambda *g: (layer, up_blk(*g), 0)),
              pl.BlockSpec((1, dn_rows, D), lambda *g: (layer, dn_blk(*g), 0))]
  out_specs = [pl.BlockSpec((up_rows, 2 * D_FF), lambda *g: (up_blk(*g), 0)),
               pl.BlockSpec((dn_rows, D), lambda *g: (dn_blk(*g), 0))]
  out_shape = [jax.ShapeDtypeStruct((D, 2 * D_FF), BF16),
               jax.ShapeDtypeStruct((D_FF, D), BF16)]
  return in_specs, out_specs, out_shape


def _cast_ffn_weight_slab(src_refs, dst_refs):
  for src, dst in zip(src_refs, dst_refs):
    dst[...] = src[0].astype(BF16)


def _params(n_grid_dims):
  return pltpu.CompilerParams(
      dimension_semantics=("arbitrary",) * n_grid_dims, vmem_limit_bytes=VMEM_LIMIT)


def _mod_kernel(cond_ref, w_ref, b_ref, o_ref):
  s = _silu(cond_ref[...]).astype(BF16)
  o_ref[0, 0] = jnp.dot(s, w_ref[0].astype(BF16), preferred_element_type=F32) + b_ref[0, 0]


def _modulation(cond, w_ada, b_ada):
  return pl.pallas_call(
      _mod_kernel,
      grid=(DEPTH, N_MOD),
      in_specs=[
          pl.BlockSpec((N_COND, D), lambda l, m: (0, 0)),
          pl.BlockSpec((1, D, D), lambda l, m: (l, 0, m)),
          pl.BlockSpec((1, 1, 1, D), lambda l, m: (l, m, 0, 0)),
      ],
      out_specs=pl.BlockSpec((1, 1, N_COND, D), lambda l, m: (l, m, 0, 0)),
      out_shape=jax.ShapeDtypeStruct((DEPTH, N_MOD, N_COND, D), F32),
      compiler_params=_params(2),
      name="modulation",
  )(cond, w_ada, b_ada.reshape(DEPTH, N_MOD, 1, D))


def _swap_middle(x):
  blk = lax.broadcasted_iota(jnp.int32, x.shape, 1) // ROPE_PAIRS
  return jnp.where(blk == 1, pltpu.roll(x, HD - ROPE_PAIRS, 1),
                   jnp.where(blk == 2, pltpu.roll(x, ROPE_PAIRS, 1), x))


def _qkv_kernel(*refs, n_x):
  x_refs, refs = refs[:n_x], refs[n_x:]
  ada_ref, g_ref, w_ref, qkg_ref, cs_ref = refs[:5]
  q_ref, k_ref, v_ref, nk_ref, nv_ref = refs[-5:]
  i = pl.program_id(0)
  is_ctx = i < _n_ctx_blk(QKV_BM)
  shift, scale = _mod_rows(ada_ref, _cond_row(i, QKV_BM))[:2]
  gq = qkg_ref[0, 0:1] * Q_SCALE
  gk = qkg_ref[0, 1:2]
  groups = [slice(s * CTX_T, (s + 1) * CTX_T) for s in range(QKV_BM // CTX_T)]

  def body(x_ref, latent):
    h = [((_rms_scale(x_ref[rows, :]) * g_ref[0, 0:1]) * (1.0 + scale) + shift).astype(BF16)
         for rows in groups]
    qkv = [jnp.dot(h_s, w_ref[0], preferred_element_type=F32) for h_s in h]
    for s, (rows, qkv_s) in enumerate(zip(groups, qkv)):
      if latent:
        cos = cs_ref[rows, :HD]
        sin = cs_ref[rows, HD:]
        rope = lambda x: x * cos + pltpu.roll(x, HD // 2, 1) * sin
      else:
        rope = lambda x: x
      for hd in range(N_HEADS):
        cols = slice(hd * HD, (hd + 1) * HD)
        q_ref[rows, cols] = rope(_rms_scale(qkv_s[:, cols]) * gq).astype(BF16)
      v = qkv_s[:, Q_W + KV_W:]
      v_ref[rows, :] = v.astype(BF16)
      for hd in range(N_KV):
        cols = slice(hd * HD, (hd + 1) * HD)
        k = rope(_rms_scale(qkv_s[:, Q_W + hd * HD:Q_W + (hd + 1) * HD]) * gk)
        k_ref[rows, cols] = k.astype(BF16)
        if not latent:
          nk_ref[s, 0, pl.ds(hd, CTX_T, stride=N_KV), :] = _swap_middle(k)
          nv_ref[s, 0, pl.ds(hd, CTX_T, stride=N_KV), :] = v[:, cols]

  @pl.when(is_ctx)
  def _():
    body(x_refs[0], latent=False)

  @pl.when(jnp.logical_not(is_ctx))
  def _():
    body(x_refs[-1], latent=True)


def _qkv(xs, ada, gains, w_qkv, qk_gains, rope_cs, new_kv, layer):
  a = layer // 2
  bm = QKV_BM
  new_spec = pl.BlockSpec(
      (bm // CTX_T, 1, CTX_T * N_KV, HD),
      lambda i: (jnp.minimum(i, _n_ctx_blk(bm) - 1), a, 0, 0))
  new_shape = jax.ShapeDtypeStruct((N_CTX_SEQ, N_ATTN, CTX_T * N_KV, HD), F32)
  args = list(xs) + [ada, gains, w_qkv, qk_gains, rope_cs]
  in_specs = _token_specs(len(xs) == 2, D, bm) + [
      _layer_spec((N_MOD, N_COND, D), layer),
      _layer_spec((4, D), layer),
      _layer_spec((D, QKV_W), a),
      _layer_spec((2, HD), a),
      pl.BlockSpec((bm, 2 * HD), lambda i: (_lat_seq_pos(i, bm)[1], 0)),
  ]
  aliases = {}
  if new_kv is not None:
    aliases = {len(args): 3, len(args) + 1: 4}
    args += list(new_kv)
    in_specs += [pl.BlockSpec(memory_space=pl.ANY)] * 2
  return pl.pallas_call(
      functools.partial(_qkv_kernel, n_x=len(xs)),
      grid=(N_TOK // bm,),
      in_specs=in_specs,
      out_specs=(_token_specs(False, Q_W, bm) + _token_specs(False, KV_W, bm) * 2
                 + [new_spec] * 2),
      out_shape=(_token_shapes(False, Q_W, BF16) + _token_shapes(False, KV_W, BF16) * 2
                 + [new_shape] * 2),
      input_output_aliases=aliases,
      compiler_params=_params(1),
      name="qkv",
  )(*args)


def _nt_dot(a, b):
  return lax.dot_general(a, b, (((1,), (1,)), ((), ())), preferred_element_type=F32)


def _with_ones(v):
  return jnp.concatenate([v, jnp.ones_like(v)], axis=1)


def _softmax_pv(s, v1):
  p = jnp.exp2(s - jnp.max(s, axis=-1, keepdims=True)).astype(BF16)
  oa = jnp.dot(p, v1, preferred_element_type=F32)
  return oa[:, :HD] / oa[:, HD:]


def _attn_ctx_kernel(q_ref, k_ref, v_ref, o_ref):
  for b in range(CTX_NB):
    rows = slice(b * CTX_T, (b + 1) * CTX_T)
    for h in range(N_KV):
      kv_cols = slice(h * HD, (h + 1) * HD)
      v1 = _with_ones(v_ref[rows, kv_cols])
      for g in range(GQA):
        cols = slice((h * GQA + g) * HD, (h * GQA + g + 1) * HD)
        s = _nt_dot(q_ref[rows, cols], k_ref[rows, kv_cols])
        o_ref[rows, cols] = _softmax_pv(s, v1).astype(BF16)


def _attn_context(q, k, v):
  bm = CTX_NB * CTX_T
  row = lambda i: (i, 0)
  return pl.pallas_call(
      _attn_ctx_kernel,
      grid=(N_CTX_TOK // bm,),
      in_specs=[
          pl.BlockSpec((bm, Q_W), row),
          pl.BlockSpec((bm, KV_W), row),
          pl.BlockSpec((bm, KV_W), row),
      ],
      out_specs=pl.BlockSpec((bm, Q_W), row),
      out_shape=jax.ShapeDtypeStruct((N_TOK, Q_W), BF16),
      compiler_params=_params(1),
      name="attn_context",
  )(q, k, v)


def _attn_lat_kernel(*refs, cast_weights):
  q_ref, k_ref, v_ref, ck_ref, cv_ref, _ = refs[:6]
  kcat, vcat = refs[-2:]
  if cast_weights:
    _cast_ffn_weight_slab(refs[6:8], refs[-4:-2])
    o_ref = refs[8]
  else:
    o_ref = refs[6]

  @pl.when(pl.program_id(2) == 0)
  def _():
    head_rows = pl.ds(pl.program_id(1), PAST, stride=N_KV)
    kcat[:LAT_T] = k_ref[...]
    kcat[LAT_T:] = _swap_middle(ck_ref[0, 0, head_rows, :]).astype(BF16)
    vcat[:LAT_T] = _with_ones(v_ref[...])
    vcat[LAT_T:] = _with_ones(cv_ref[0, 0, head_rows, :].astype(BF16))

  for g in range(GQA):
    cols = slice(g * HD, (g + 1) * HD)
    for r in range(ATT_TQ // ATT_CHUNK):
      rows = slice(r * ATT_CHUNK, (r + 1) * ATT_CHUNK)
      s = _nt_dot(q_ref[rows, cols], kcat[...])
      o_ref[rows, cols] = _softmax_pv(s, vcat[...]).astype(BF16)


def _attn_latent(q, k, v, cache_k, cache_v, mix, a, cast_w):
  nq = LAT_T // ATT_TQ
  q0 = N_CTX_TOK // ATT_TQ
  kv0 = N_CTX_TOK // LAT_T
  cache_rows = (N_LAT_SEQ, N_ATTN, PAST * N_KV, HD)
  cache_spec = pl.BlockSpec((1, 1, PAST * N_KV, HD), lambda b, h, j: (b, a, 0, 0))
  in_specs = [
      pl.BlockSpec((ATT_TQ, GQA * HD), lambda b, h, j: (q0 + b * nq + j, h)),
      pl.BlockSpec((LAT_T, HD), lambda b, h, j: (kv0 + b, h)),
      pl.BlockSpec((LAT_T, HD), lambda b, h, j: (kv0 + b, h)),
      cache_spec,
      cache_spec,
      pl.BlockSpec(memory_space=pl.ANY),
  ]
  args = [q, k, v, cache_k.reshape(cache_rows), cache_v.reshape(cache_rows), mix]
  out_specs = [pl.BlockSpec((ATT_TQ, GQA * HD), lambda b, h, j: (q0 + b * nq + j, h))]
  out_shape = [jax.ShapeDtypeStruct((N_TOK, Q_W), BF16)]
  if cast_w is not None:
    cast_in, cast_out, cast_shape = _ffn_weight_cast_specs(
        0, N_LAT_SEQ * N_KV * nq, lambda b, h, j: (b * N_KV + h) * nq + j)
    in_specs += cast_in
    args += list(cast_w)
    out_specs += cast_out
    out_shape += cast_shape
  outs = pl.pallas_call(
      functools.partial(_attn_lat_kernel, cast_weights=cast_w is not None),
      grid=(N_LAT_SEQ, N_KV, nq),
      in_specs=in_specs,
      out_specs=out_specs,
      out_shape=out_shape,
      scratch_shapes=[pltpu.VMEM((LAT_T + PAST, HD), BF16),
                      pltpu.VMEM((LAT_T + PAST, 2 * HD), BF16)],
      input_output_aliases={5: 0},
      compiler_params=_params(3),
      name="attn_latent",
  )(*args)
  return outs[0], outs[1:]


def _edge_window_sums(slab, w):
  rows = slab.shape[0]
  back, fwd, m = slab, slab, 1
  while m < w // 2:
    back = back + pltpu.roll(back, m, 0)
    fwd = fwd + pltpu.roll(fwd, rows - m, 0)
    m *= 2
  return (pltpu.roll(back, 1, 0) + fwd)[POOL_HALO:2 * POOL_HALO]


def _pool_bands():
  t = np.arange(POOL_TILE)[:, None]
  j = np.arange(POOL_TILE)[None, :]
  bands = [(j >= t - w // 2) & (j < t + w - w // 2) for w in POOL_WINDOWS]
  return jnp.asarray(np.stack(bands), dtype=BF16)


def _pool_kernel(x_ref, xprev_ref, xnext_ref, ada_ref, g_ref, band_ref, w_ref, ps_ref, o_ref):
  i = pl.program_id(0)
  is_ctx = i < _n_ctx_blk(POOL_BM)
  shift, scale = _mod_rows(ada_ref, _cond_row(i, POOL_BM))[:2]
  pre = lambda x: (_rms_scale(x) * g_ref[0, 0:1]) * (1.0 + scale) + shift
  h = pre(x_ref[...])
  seq_pos = _lat_seq_pos(i, POOL_BM)[1]
  is_first = jnp.logical_or(is_ctx, seq_pos == 0)
  is_last = jnp.logical_or(is_ctx, seq_pos == LAT_T // POOL_BM - 1)
  ctx_edge = jnp.where(is_ctx, 1.0, 0.0)
  first_edge = jnp.where(is_first, 1.0, 0.0)
  last_edge = jnp.where(is_last, 1.0, 0.0)
  h_prev = jnp.where(is_first, 0.0, pre(xprev_ref[...]))
  h_next = jnp.where(is_last, 0.0, pre(xnext_ref[...]))
  h_hi = h.astype(BF16)
  h_lo = (h - h_hi.astype(F32)).astype(BF16)
  n_tiles = POOL_BM // POOL_TILE
  row = lax.broadcasted_iota(jnp.int32, (POOL_TILE, POOL_DG), 0)
  units = [(g, w, j) for g, w in enumerate(POOL_WINDOWS) for j in range(n_tiles)]
  col_of = lambda g: slice(g * POOL_DG, (g + 1) * POOL_DG)
  row_of = lambda j: slice(j * POOL_TILE, (j + 1) * POOL_TILE)

  parts = [jnp.dot(band_ref[g],
                   jnp.concatenate([h_hi[row_of(j), col_of(g)], h_lo[row_of(j), col_of(g)]], axis=1),
                   preferred_element_type=F32) for g, w, j in units]

  pooled = []
  for (g, w, j), part in zip(units, parts):
    r0, cols = j * POOL_TILE, col_of(g)
    hg = h[row_of(j), cols]
    inner = part[:, :POOL_DG] + part[:, POOL_DG:]
    before = (h_prev[:, cols] if j == 0
              else jnp.where(is_ctx, 0.0, h[r0 - POOL_HALO:r0, cols]))
    after = (h_next[:, cols] if j == n_tiles - 1
             else jnp.where(is_ctx, 0.0, h[r0 + POOL_TILE:r0 + POOL_TILE + POOL_HALO, cols]))
    top = _edge_window_sums(jnp.concatenate([before, hg[:2 * POOL_HALO]], axis=0), w)
    bot = _edge_window_sums(jnp.concatenate([hg[-2 * POOL_HALO:], after], axis=0), w)
    win = jnp.concatenate([top, inner[POOL_HALO:-POOL_HALO], bot], axis=0)
    starts = first_edge if j == 0 else ctx_edge
    ends = last_edge if j == n_tiles - 1 else ctx_edge
    lo_cut = jnp.maximum(w // 2 - row, 0).astype(F32)
    hi_cut = jnp.maximum(row + (w - w // 2) - POOL_TILE, 0).astype(F32)
    cnt = w - lo_cut * starts - hi_cut * ends
    pooled.append((win / cnt - hg).astype(BF16))

  for (g, w, j), p in zip(units, pooled):
    og = jnp.dot(p, w_ref[0, g], preferred_element_type=F32)
    o_ref[row_of(j), col_of(g)] = (og * ps_ref[0, :, col_of(g)]).astype(BF16)


def _pool(x, ada, gains, w_pool, pool_scale, layer):
  p = layer // 2
  bm = POOL_BM
  halo_per_blk = bm // POOL_HALO
  n_halo = N_TOK // POOL_HALO
  prev_spec = pl.BlockSpec((POOL_HALO, D), lambda i: (jnp.maximum(i * halo_per_blk - 1, 0), 0))
  next_spec = pl.BlockSpec(
      (POOL_HALO, D), lambda i: (jnp.minimum((i + 1) * halo_per_blk, n_halo - 1), 0))
  return pl.pallas_call(
      _pool_kernel,
      grid=(N_TOK // bm,),
      in_specs=_token_specs(False, D, bm) + [
          prev_spec, next_spec,
          _layer_spec((N_MOD, N_COND, D), layer),
          _layer_spec((4, D), layer),
          pl.BlockSpec((N_POOL_G, POOL_TILE, POOL_TILE), lambda i: (0, 0, 0),
                       pipeline_mode=pl.Buffered(1)),
          _layer_spec((N_POOL_G, POOL_DG, POOL_DG), p),
          _layer_spec((1, D), p),
      ],
      out_specs=_token_specs(False, D, bm)[0],
      out_shape=_token_shapes(False, D, BF16)[0],
      compiler_params=_params(1),
      name="pool",
  )(x, x, x, ada, gains, _pool_bands(), w_pool, pool_scale.reshape(-1, 1, D))


def _ffn_kernel(*refs, n_x, n_out, has_wo, cast_next):
  x_refs, xn_refs, refs = refs[:n_x], refs[n_x:2 * n_x], refs[2 * n_x:]
  (x1_scr, h_scr), refs = refs[-2:], refs[:-2]
  if cast_next:
    cast_dst, refs = refs[-2:], refs[:-2]
  o_refs, refs = refs[-n_out:], refs[:-n_out]
  if cast_next:
    cast_src, refs = refs[-2:], refs[:-2]
  if has_wo:
    mix_ref, mixn_ref, ada_ref, g_ref, wo_ref, wup_ref, wdn_ref = refs
  else:
    mix_ref, mixn_ref, ada_ref, g_ref, wup_ref, wdn_ref = refs
  i = pl.program_id(0)
  nxt = jnp.minimum(i + 1, pl.num_programs(0) - 1)
  n_ctx = _n_ctx_blk(FFN_BM)
  mod = _mod_rows(ada_ref, _cond_row(i, FFN_BM))
  mod_nxt = _mod_rows(ada_ref, _cond_row(nxt, FFN_BM))
  gate_f = mod[5]
  if cast_next:
    _cast_ffn_weight_slab(cast_src, cast_dst)

  def pre_ffn(x, mix, mod):
    _, _, gate_m, shift_f, scale_f, _ = mod
    if has_wo:
      a = jnp.dot(mix, wo_ref[0], preferred_element_type=F32)
    else:
      a = mix.astype(F32)
    x1 = x + gate_m * (_rms_scale(a) * g_ref[0, 1:2])
    return x1, ((_rms_scale(x1) * g_ref[0, 2:3]) * (1.0 + scale_f) + shift_f).astype(BF16)

  group = FFN_BM // FFN_GROUPS
  groups = [slice(r * group, (r + 1) * group) for r in range(FFN_GROUPS)]

  @pl.when(i == 0)
  def _():
    x1_scr[...], h_scr[...] = pre_ffn(
        _token_load(i < n_ctx, x_refs, groups[0]), mix_ref[groups[0], :], mod)

  x1, h = [x1_scr[...]], [h_scr[...]]
  for rows in groups[1:]:
    x1_r, h_r = pre_ffn(_token_load(i < n_ctx, x_refs, rows), mix_ref[rows, :], mod)
    x1.append(x1_r)
    h.append(h_r)
  gu, y = [], []
  for r in range(FFN_GROUPS + 1):
    if r < FFN_GROUPS:
      gu.append(jnp.dot(h[r], wup_ref[...], preferred_element_type=F32))
    if r == FFN_GROUPS - 1:
      x1_nxt, h_nxt = pre_ffn(_token_load(nxt < n_ctx, xn_refs), mixn_ref[...], mod_nxt)
    if r > 0:
      act = (_silu(gu[r - 1][:, :D_FF]) * gu[r - 1][:, D_FF:]).astype(BF16)
      y.append(jnp.dot(act, wdn_ref[...], preferred_element_type=F32))
  for rows, x1_r, y_r in zip(groups, x1, y):
    _token_store(i < n_ctx, o_refs, x1_r + gate_f * (_rms_scale(y_r) * g_ref[0, 3:4]), rows)
  x1_scr[...], h_scr[...] = x1_nxt, h_nxt


def _next_group_specs(split, width, bm, group):
  n_blk, n_ctx, per_blk = N_TOK // bm, _n_ctx_blk(bm), bm // group
  nxt = lambda i: jnp.minimum(i + 1, n_blk - 1)
  if not split:
    return [pl.BlockSpec((group, width), lambda i: (nxt(i) * per_blk, 0))]
  return [
      pl.BlockSpec((group, width), lambda i: (jnp.minimum(nxt(i), n_ctx - 1) * per_blk, 0)),
      pl.BlockSpec((group, width), lambda i: (jnp.maximum(nxt(i) - n_ctx, 0) * per_blk, 0))]


def _ffn(xs, mix, ada, gains, w_o, w_up, w_down, next_w, layer, split_out):
  has_wo = w_o is not None
  cast_next = next_w is not None
  resident = lambda shape: pl.BlockSpec(shape, lambda i: (0, 0), pipeline_mode=pl.Buffered(1))
  bm = FFN_BM
  n_blk = N_TOK // bm
  group = bm // FFN_GROUPS
  split_in = len(xs) == 2
  in_specs = (_token_specs(split_in, D, bm) + _next_group_specs(split_in, D, bm, group)
              + _token_specs(False, D, bm) + _next_group_specs(False, D, bm, group) + [
                  _layer_spec((N_MOD, N_COND, D), layer),
                  _layer_spec((4, D), layer),
              ])
  args = list(xs) + list(xs) + [mix, mix, ada, gains]
  if has_wo:
    in_specs.append(_layer_spec((Q_W, D), layer // 2))
    args.append(w_o)
  in_specs += [resident((D, 2 * D_FF)), resident((D_FF, D))]
  args += [w_up, w_down]
  n_out = 2 if split_out else 1
  out_specs = _token_specs(split_out, D, bm)
  out_shape = _token_shapes(split_out, D, F32)
  if cast_next:
    cast_in, cast_out, cast_shape = _ffn_weight_cast_specs(layer + 1, n_blk, lambda i: i)
    in_specs += cast_in
    args += list(next_w)
    out_specs += cast_out
    out_shape += cast_shape
  outs = pl.pallas_call(
      functools.partial(_ffn_kernel, n_x=len(xs), n_out=n_out, has_wo=has_wo,
                        cast_next=cast_next),
      grid=(n_blk,),
      in_specs=in_specs,
      out_specs=out_specs,
      out_shape=out_shape,
      scratch_shapes=[pltpu.VMEM((group, D), F32), pltpu.VMEM((group, D), BF16)],
      compiler_params=_params(1),
      name="ffn_attn" if has_wo else "ffn_pool",
  )(*args)
  return outs[:n_out], outs[n_out:]


def _rope_tables():
  pos = np.arange(LAT_T)
  row = (pos // GRID_W).astype(np.float32)
  col = (pos % GRID_W).astype(np.float32)
  inv = np.float32(ROPE_BASE) ** (-np.arange(ROPE_PAIRS, dtype=np.float32) / np.float32(ROPE_PAIRS))
  ar = row[:, None] * inv
  ac = col[:, None] * inv
  cos = np.concatenate([np.cos(ar), np.cos(ac), np.cos(ar), np.cos(ac)], axis=1)
  sin = np.concatenate([-np.sin(ar), -np.sin(ac), np.sin(ar), np.sin(ac)], axis=1)
  return jnp.asarray(np.concatenate([cos, sin], axis=1), dtype=F32)


def _paired_qk_columns(w_qkv, qk_gains):
  def paired(a):
    blocks = a.reshape(a.shape[:-1] + (-1, 2, 2, ROPE_PAIRS))
    return jnp.swapaxes(blocks, -3, -2).reshape(a.shape)

  w = jnp.concatenate([paired(w_qkv[..., :Q_W + KV_W]), w_qkv[..., Q_W + KV_W:]], axis=-1)
  return w, paired(qk_gains)


def kernel(x_prompt, x_sample, c, cache_k, cache_v, c_ctx, w_ada, b_ada, norm_gains,
           w_qkv, qk_gains, w_o, w_pool, pool_scale, w_up, w_down):
  xs = [x_prompt.reshape(N_CTX_TOK, D), x_sample.reshape(N_LAT_TOK, D)]
  cond = jnp.concatenate(
      [c_ctx[None, :], c, jnp.zeros((N_COND - 1 - N_LAT_SEQ, D), F32)], axis=0)
  ada = _modulation(cond, w_ada, b_ada)
  rope_cs = _rope_tables()
  w_qkv_b, qk_gains_p = _paired_qk_columns(w_qkv.astype(BF16), qk_gains)
  w_o_b = w_o.astype(BF16)
  w_pool_b = w_pool.astype(BF16)

  new_kv = ffn_w = None
  for layer in range(DEPTH):
    last = layer == DEPTH - 1
    next_w = None if last else (w_up, w_down)
    if layer % 2 == 0:
      q, k, v, *new_kv = _qkv(xs, ada, norm_gains, w_qkv_b, qk_gains_p, rope_cs, new_kv, layer)
      mix, cast_w = _attn_latent(q, k, v, cache_k, cache_v, _attn_context(q, k, v), layer // 2,
                                 (w_up, w_down) if ffn_w is None else None)
      ffn_w = ffn_w or cast_w
      xs, ffn_w = _ffn(xs, mix, ada, norm_gains, w_o_b, *ffn_w, next_w, layer, last)
    else:
      mix = _pool(xs[0], ada, norm_gains, w_pool_b, pool_scale, layer)
      xs, ffn_w = _ffn(xs, mix, ada, norm_gains, None, *ffn_w, next_w, layer, last)

  out_kv = (N_CTX_SEQ, N_ATTN, CTX_T, N_KV, HD)
  return (xs[0].reshape(N_CTX_SEQ, CTX_T, D), xs[1].reshape(N_LAT_SEQ, LAT_T, D),
          new_kv[0].reshape(out_kv), new_kv[1].reshape(out_kv))
```

```python
import functools
import math

import jax
import jax.numpy as jnp
import numpy as np
from jax import lax
from jax.experimental import pallas as pl
from jax.experimental.pallas import tpu as pltpu

F32 = jnp.float32
BF16 = jnp.bfloat16

D = 1024
N_CTX_SEQ = 32
CTX_T = 256
N_LAT_SEQ = 4
LAT_T = 2048
PAST = 512
DEPTH = 4
N_ATTN = 2
GRID_W = 64
HD = 128
N_HEADS = 8
N_KV = 2
GQA = N_HEADS // N_KV
Q_W = N_HEADS * HD
KV_W = N_KV * HD
QKV_W = Q_W + 2 * KV_W
ROPE_PAIRS = HD // 4
ROPE_BASE = 10000.0
POOL_WINDOWS = (2, 4, 8, 16)
N_POOL_G = len(POOL_WINDOWS)
POOL_DG = D // N_POOL_G
D_FF = 2816
N_MOD = 6
EPS = 1e-6

N_CTX_TOK = N_CTX_SEQ * CTX_T
N_LAT_TOK = N_LAT_SEQ * LAT_T
N_TOK = N_CTX_TOK + N_LAT_TOK
N_COND = 8
Q_SCALE = math.log2(math.e) / math.sqrt(HD)

VMEM_LIMIT = 56 * 1024 * 1024

FFN_BM = 512
QKV_BM = 512
POOL_BM = 1024
FFN_GROUPS = 2
POOL_TILE = CTX_T
POOL_HALO = max(POOL_WINDOWS) // 2
ATT_TQ = 1024
ATT_CHUNK = 256
CTX_NB = 4


def _n_ctx_blk(bm):
  return N_CTX_TOK // bm


def _token_specs(split, width, bm):
  n_ctx = _n_ctx_blk(bm)
  if not split:
    return [pl.BlockSpec((bm, width), lambda i: (i, 0))]
  return [pl.BlockSpec((bm, width), lambda i: (jnp.minimum(i, n_ctx - 1), 0)),
          pl.BlockSpec((bm, width), lambda i: (jnp.maximum(i - n_ctx, 0), 0))]


def _token_shapes(split, width, dtype):
  if not split:
    return [jax.ShapeDtypeStruct((N_TOK, width), dtype)]
  return [jax.ShapeDtypeStruct((N_CTX_TOK, width), dtype),
          jax.ShapeDtypeStruct((N_LAT_TOK, width), dtype)]


def _token_load(is_ctx, refs, rows=slice(None)):
  if len(refs) == 1:
    return refs[0][rows, :]
  return jnp.where(is_ctx, refs[0][rows, :], refs[1][rows, :])


def _token_store(is_ctx, refs, value, rows=slice(None)):
  if len(refs) == 1:
    refs[0][rows, :] = value
    return

  @pl.when(is_ctx)
  def _():
    refs[0][rows, :] = value

  @pl.when(jnp.logical_not(is_ctx))
  def _():
    refs[1][rows, :] = value


def _lat_seq_pos(i, bm):
  lat_blk = jnp.maximum(i - _n_ctx_blk(bm), 0)
  return lat_blk // (LAT_T // bm), lat_blk % (LAT_T // bm)


def _cond_row(i, bm):
  return jnp.where(i < _n_ctx_blk(bm), 0, 1 + _lat_seq_pos(i, bm)[0])


def _mod_rows(ada_ref, row):
  return [ada_ref[0, m, pl.ds(row, 1), :] for m in range(N_MOD)]


def _layer_spec(shape, layer):
  zeros = (0,) * len(shape)
  return pl.BlockSpec((1,) + shape, lambda *_: (layer,) + zeros, pipeline_mode=pl.Buffered(1))


def _rms_scale(x):
  return x * lax.rsqrt(jnp.mean(x * x, axis=-1, keepdims=True) + EPS)


def _silu(x):
  return x / (1.0 + jnp.exp(-x))


BF16_SUBLANES = 16


def _ffn_weight_cast_specs(layer, n_steps, step_of):
  up_rows = D // n_steps
  dn_blks = n_steps
  while (D_FF // dn_blks) % BF16_SUBLANES:
    dn_blks //= 2
  dn_rows = D_FF // dn_blks
  up_blk = lambda *g: step_of(*g)
  dn_blk = lambda *g: jnp.minimum(step_of(*g), dn_blks - 1)
  in_specs = [pl.BlockSpec((1, up_rows, 2 * D_FF), lambda *g: (layer, up_blk(*g), 0)),
              pl.BlockSpec((1, dn_rows, D), lambda *g: (layer, dn_blk(*g), 0))]
  out_specs = [pl.BlockSpec((up_rows, 2 * D_FF), lambda *g: (up_blk(*g), 0)),
               pl.BlockSpec((dn_rows, D), lambda *g: (dn_blk(*g), 0))]
  out_shape = [jax.ShapeDtypeStruct((D, 2 * D_FF), BF16),
               jax.ShapeDtypeStruct((D_FF, D), BF16)]
  return in_specs, out_specs, out_shape


def _cast_ffn_weight_slab(src_refs, dst_refs):
  for src, dst in zip(src_refs, dst_refs):
    dst[...] = src[0].astype(BF16)


def _params(n_grid_dims):
  return pltpu.CompilerParams(
      dimension_semantics=("arbitrary",) * n_grid_dims, vmem_limit_bytes=VMEM_LIMIT)


def _mod_kernel(cond_ref, w_ref, b_ref, o_ref):
  s = _silu(cond_ref[...]).astype(BF16)
  o_ref[0, 0] = jnp.dot(s, w_ref[0].astype(BF16), preferred_element_type=F32) + b_ref[0, 0]


def _modulation(cond, w_ada, b_ada):
  return pl.pallas_call(
      _mod_kernel,
      grid=(DEPTH, N_MOD),
      in_specs=[
          pl.BlockSpec((N_COND, D), lambda l, m: (0, 0)),
          pl.BlockSpec((1, D, D), lambda l, m: (l, 0, m)),
          pl.BlockSpec((1, 1, 1, D), lambda l, m: (l, m, 0, 0)),
      ],
      out_specs=pl.BlockSpec((1, 1, N_COND, D), lambda l, m: (l, m, 0, 0)),
      out_shape=jax.ShapeDtypeStruct((DEPTH, N_MOD, N_COND, D), F32),
      compiler_params=_params(2),
      name="modulation",
  )(cond, w_ada, b_ada.reshape(DEPTH, N_MOD, 1, D))


def _swap_middle(x):
  blk = lax.broadcasted_iota(jnp.int32, x.shape, 1) // ROPE_PAIRS
  return jnp.where(blk == 1, pltpu.roll(x, HD - ROPE_PAIRS, 1),
                   jnp.where(blk == 2, pltpu.roll(x, ROPE_PAIRS, 1), x))


def _qkv_kernel(*refs, n_x):
  x_refs, refs = refs[:n_x], refs[n_x:]
  ada_ref, g_ref, w_ref, qkg_ref, cs_ref = refs[:5]
  q_ref, k_ref, v_ref, nk_ref, nv_ref = refs[-5:]
  i = pl.program_id(0)
  is_ctx = i < _n_ctx_blk(QKV_BM)
  shift, scale = _mod_rows(ada_ref, _cond_row(i, QKV_BM))[:2]
  gq = qkg_ref[0, 0:1] * Q_SCALE
  gk = qkg_ref[0, 1:2]
  groups = [slice(s * CTX_T, (s + 1) * CTX_T) for s in range(QKV_BM // CTX_T)]

  def body(x_ref, latent):
    h = [((_rms_scale(x_ref[rows, :]) * g_ref[0, 0:1]) * (1.0 + scale) + shift).astype(BF16)
         for rows in groups]
    qkv = [jnp.dot(h_s, w_ref[0], preferred_element_type=F32) for h_s in h]
    for s, (rows, qkv_s) in enumerate(zip(groups, qkv)):
      if latent:
        cos = cs_ref[rows, :HD]
        sin = cs_ref[rows, HD:]
        rope = lambda x: x * cos + pltpu.roll(x, HD // 2, 1) * sin
      else:
        rope = lambda x: x
      for hd in range(N_HEADS):
        cols = slice(hd * HD, (hd + 1) * HD)
        q_ref[rows, cols] = rope(_rms_scale(qkv_s[:, cols]) * gq).astype(BF16)
      v = qkv_s[:, Q_W + KV_W:]
      v_ref[rows, :] = v.astype(BF16)
      for hd in range(N_KV):
        cols = slice(hd * HD, (hd + 1) * HD)
        k = rope(_rms_scale(qkv_s[:, Q_W + hd * HD:Q_W + (hd + 1) * HD]) * gk)
        k_ref[rows, cols] = k.astype(BF16)
        if not latent:
          nk_ref[s, 0, pl.ds(hd, CTX_T, stride=N_KV), :] = _swap_middle(k)
          nv_ref[s, 0, pl.ds(hd, CTX_T, stride=N_KV), :] = v[:, cols]

  @pl.when(is_ctx)
  def _():
    body(x_refs[0], latent=False)

  @pl.when(jnp.logical_not(is_ctx))
  def _():
    body(x_refs[-1], latent=True)


def _qkv(xs, ada, gains, w_qkv, qk_gains, rope_cs, new_kv, layer):
  a = layer // 2
  bm = QKV_BM
  new_spec = pl.BlockSpec(
      (bm // CTX_T, 1, CTX_T * N_KV, HD),
      lambda i: (jnp.minimum(i, _n_ctx_blk(bm) - 1), a, 0, 0))
  new_shape = jax.ShapeDtypeStruct((N_CTX_SEQ, N_ATTN, CTX_T * N_KV, HD), F32)
  args = list(xs) + [ada, gains, w_qkv, qk_gains, rope_cs]
  in_specs = _token_specs(len(xs) == 2, D, bm) + [
      _layer_spec((N_MOD, N_COND, D), layer),
      _layer_spec((4, D), layer),
      _layer_spec((D, QKV_W), a),
      _layer_spec((2, HD), a),
      pl.BlockSpec((bm, 2 * HD), lambda i: (_lat_seq_pos(i, bm)[1], 0)),
  ]
  aliases = {}
  if new_kv is not None:
    aliases = {len(args): 3, len(args) + 1: 4}
    args += list(new_kv)
    in_specs += [pl.BlockSpec(memory_space=pl.ANY)] * 2
  return pl.pallas_call(
      functools.partial(_qkv_kernel, n_x=len(xs)),
      grid=(N_TOK // bm,),
      in_specs=in_specs,
      out_specs=(_token_specs(False, Q_W, bm) + _token_specs(False, KV_W, bm) * 2
                 + [new_spec] * 2),
      out_shape=(_token_shapes(False, Q_W, BF16) + _token_shapes(False, KV_W, BF16) * 2
                 + [new_shape] * 2),
      input_output_aliases=aliases,
      compiler_params=_params(1),
      name="qkv",
  )(*args)


def _nt_dot(a, b):
  return lax.dot_general(a, b, (((1,), (1,)), ((), ())), preferred_element_type=F32)


def _with_ones(v):
  return jnp.concatenate([v, jnp.ones_like(v)], axis=1)


def _softmax_pv(s, v1):
  p = jnp.exp2(s - jnp.max(s, axis=-1, keepdims=True)).astype(BF16)
  oa = jnp.dot(p, v1, preferred_element_type=F32)
  return oa[:, :HD] / oa[:, HD:]


def _attn_ctx_kernel(q_ref, k_ref, v_ref, o_ref):
  for b in range(CTX_NB):
    rows = slice(b * CTX_T, (b + 1) * CTX_T)
    for h in range(N_KV):
      kv_cols = slice(h * HD, (h + 1) * HD)
      v1 = _with_ones(v_ref[rows, kv_cols])
      for g in range(GQA):
        cols = slice((h * GQA + g) * HD, (h * GQA + g + 1) * HD)
        s = _nt_dot(q_ref[rows, cols], k_ref[rows, kv_cols])
        o_ref[rows, cols] = _softmax_pv(s, v1).astype(BF16)


def _attn_context(q, k, v):
  bm = CTX_NB * CTX_T
  row = lambda i: (i, 0)
  return pl.pallas_call(
      _attn_ctx_kernel,
      grid=(N_CTX_TOK // bm,),
      in_specs=[
          pl.BlockSpec((bm, Q_W), row),
          pl.BlockSpec((bm, KV_W), row),
          pl.BlockSpec((bm, KV_W), row),
      ],
      out_specs=pl.BlockSpec((bm, Q_W), row),
      out_shape=jax.ShapeDtypeStruct((N_TOK, Q_W), BF16),
      compiler_params=_params(1),
      name="attn_context",
  )(q, k, v)


def _attn_lat_kernel(*refs, cast_weights):
  q_ref, k_ref, v_ref, ck_ref, cv_ref, _ = refs[:6]
  kcat, vcat = refs[-2:]
  if cast_weights:
    _cast_ffn_weight_slab(refs[6:8], refs[-4:-2])
    o_ref = refs[8]
  else:
    o_ref = refs[6]

  @pl.when(pl.program_id(2) == 0)
  def _():
    head_rows = pl.ds(pl.program_id(1), PAST, stride=N_KV)
    kcat[:LAT_T] = k_ref[...]
    kcat[LAT_T:] = _swap_middle(ck_ref[0, 0, head_rows, :]).astype(BF16)
    vcat[:LAT_T] = _with_ones(v_ref[...])
    vcat[LAT_T:] = _with_ones(cv_ref[0, 0, head_rows, :].astype(BF16))

  chains = [(g, r, ATT_CHUNK) for g in range(GQA) for r in range(0, ATT_TQ, ATT_CHUNK)]
  halves = lambda g, r, n: [(g, r, n // 2), (g, r + n // 2, n // 2)]
  chains = halves(*chains[0]) + chains[1:-1] + halves(*chains[-1])
  for g, r, n in chains:
    rows, cols = slice(r, r + n), slice(g * HD, (g + 1) * HD)
    s = _nt_dot(q_ref[rows, cols], kcat[...])
    o_ref[rows, cols] = _softmax_pv(s, vcat[...]).astype(BF16)


def _attn_latent(q, k, v, cache_k, cache_v, mix, a, cast_w):
  nq = LAT_T // ATT_TQ
  q0 = N_CTX_TOK // ATT_TQ
  kv0 = N_CTX_TOK // LAT_T
  cache_rows = (N_LAT_SEQ, N_ATTN, PAST * N_KV, HD)
  cache_spec = pl.BlockSpec((1, 1, PAST * N_KV, HD), lambda b, h, j: (b, a, 0, 0))
  in_specs = [
      pl.BlockSpec((ATT_TQ, GQA * HD), lambda b, h, j: (q0 + b * nq + j, h)),
      pl.BlockSpec((LAT_T, HD), lambda b, h, j: (kv0 + b, h)),
      pl.BlockSpec((LAT_T, HD), lambda b, h, j: (kv0 + b, h)),
      cache_spec,
      cache_spec,
      pl.BlockSpec(memory_space=pl.ANY),
  ]
  args = [q, k, v, cache_k.reshape(cache_rows), cache_v.reshape(cache_rows), mix]
  out_specs = [pl.BlockSpec((ATT_TQ, GQA * HD), lambda b, h, j: (q0 + b * nq + j, h))]
  out_shape = [jax.ShapeDtypeStruct((N_TOK, Q_W), BF16)]
  if cast_w is not None:
    cast_in, cast_out, cast_shape = _ffn_weight_cast_specs(
        0, N_LAT_SEQ * N_KV * nq, lambda b, h, j: (b * N_KV + h) * nq + j)
    in_specs += cast_in
    args += list(cast_w)
    out_specs += cast_out
    out_shape += cast_shape
  outs = pl.pallas_call(
      functools.partial(_attn_lat_kernel, cast_weights=cast_w is not None),
      grid=(N_LAT_SEQ, N_KV, nq),
      in_specs=in_specs,
      out_specs=out_specs,
      out_shape=out_shape,
      scratch_shapes=[pltpu.VMEM((LAT_T + PAST, HD), BF16),
                      pltpu.VMEM((LAT_T + PAST, 2 * HD), BF16)],
      input_output_aliases={5: 0},
      compiler_params=_params(3),
      name="attn_latent",
  )(*args)
  return outs[0], outs[1:]


def _edge_window_sums(slab, w):
  rows = slab.shape[0]
  back, fwd, m = slab, slab, 1
  while m < w // 2:
    back = back + pltpu.roll(back, m, 0)
    fwd = fwd + pltpu.roll(fwd, rows - m, 0)
    m *= 2
  return (pltpu.roll(back, 1, 0) + fwd)[POOL_HALO:2 * POOL_HALO]


def _pool_bands():
  t = np.arange(POOL_TILE)[:, None]
  j = np.arange(POOL_TILE)[None, :]
  bands = [(j >= t - w // 2) & (j < t + w - w // 2) for w in POOL_WINDOWS]
  return jnp.asarray(np.stack(bands), dtype=BF16)


def _pool_kernel(x_ref, xprev_ref, xnext_ref, ada_ref, g_ref, band_ref, w_ref, ps_ref, o_ref):
  i = pl.program_id(0)
  is_ctx = i < _n_ctx_blk(POOL_BM)
  shift, scale = _mod_rows(ada_ref, _cond_row(i, POOL_BM))[:2]
  pre = lambda x: (_rms_scale(x) * g_ref[0, 0:1]) * (1.0 + scale) + shift
  h = pre(x_ref[...])
  seq_pos = _lat_seq_pos(i, POOL_BM)[1]
  is_first = jnp.logical_or(is_ctx, seq_pos == 0)
  is_last = jnp.logical_or(is_ctx, seq_pos == LAT_T // POOL_BM - 1)
  ctx_edge = jnp.where(is_ctx, 1.0, 0.0)
  first_edge = jnp.where(is_first, 1.0, 0.0)
  last_edge = jnp.where(is_last, 1.0, 0.0)
  h_prev = jnp.where(is_first, 0.0, pre(xprev_ref[...]))
  h_next = jnp.where(is_last, 0.0, pre(xnext_ref[...]))
  h_hi = h.astype(BF16)
  h_lo = (h - h_hi.astype(F32)).astype(BF16)
  n_tiles = POOL_BM // POOL_TILE
  row = lax.broadcasted_iota(jnp.int32, (POOL_TILE, POOL_DG), 0)
  units = [(g, w, j) for g, w in enumerate(POOL_WINDOWS) for j in range(n_tiles)]
  col_of = lambda g: slice(g * POOL_DG, (g + 1) * POOL_DG)
  row_of = lambda j: slice(j * POOL_TILE, (j + 1) * POOL_TILE)

  parts = [jnp.dot(band_ref[g],
                   jnp.concatenate([h_hi[row_of(j), col_of(g)], h_lo[row_of(j), col_of(g)]], axis=1),
                   preferred_element_type=F32) for g, w, j in units]

  pooled = []
  for (g, w, j), part in zip(units, parts):
    r0, cols = j * POOL_TILE, col_of(g)
    hg = h[row_of(j), cols]
    inner = part[:, :POOL_DG] + part[:, POOL_DG:]
    before = (h_prev[:, cols] if j == 0
              else jnp.where(is_ctx, 0.0, h[r0 - POOL_HALO:r0, cols]))
    after = (h_next[:, cols] if j == n_tiles - 1
             else jnp.where(is_ctx, 0.0, h[r0 + POOL_TILE:r0 + POOL_TILE + POOL_HALO, cols]))
    top = _edge_window_sums(jnp.concatenate([before, hg[:2 * POOL_HALO]], axis=0), w)
    bot = _edge_window_sums(jnp.concatenate([hg[-2 * POOL_HALO:], after], axis=0), w)
    win = jnp.concatenate([top, inner[POOL_HALO:-POOL_HALO], bot], axis=0)
    starts = first_edge if j == 0 else ctx_edge
    ends = last_edge if j == n_tiles - 1 else ctx_edge
    lo_cut = jnp.maximum(w // 2 - row, 0).astype(F32)
    hi_cut = jnp.maximum(row + (w - w // 2) - POOL_TILE, 0).astype(F32)
    cnt = w - lo_cut * starts - hi_cut * ends
    pooled.append((win / cnt - hg).astype(BF16))

  for (g, w, j), p in zip(units, pooled):
    og = jnp.dot(p, w_ref[0, g], preferred_element_type=F32)
    o_ref[row_of(j), col_of(g)] = (og * ps_ref[0, :, col_of(g)]).astype(BF16)


def _pool(x, ada, gains, w_pool, pool_scale, layer):
  p = layer // 2
  bm = POOL_BM
  halo_per_blk = bm // POOL_HALO
  n_halo = N_TOK // POOL_HALO
  prev_spec = pl.BlockSpec((POOL_HALO, D), lambda i: (jnp.maximum(i * halo_per_blk - 1, 0), 0))
  next_spec = pl.BlockSpec(
      (POOL_HALO, D), lambda i: (jnp.minimum((i + 1) * halo_per_blk, n_halo - 1), 0))
  return pl.pallas_call(
      _pool_kernel,
      grid=(N_TOK // bm,),
      in_specs=_token_specs(False, D, bm) + [
          prev_spec, next_spec,
          _layer_spec((N_MOD, N_COND, D), layer),
          _layer_spec((4, D), layer),
          pl.BlockSpec((N_POOL_G, POOL_TILE, POOL_TILE), lambda i: (0, 0, 0),
                       pipeline_mode=pl.Buffered(1)),
          _layer_spec((N_POOL_G, POOL_DG, POOL_DG), p),
          _layer_spec((1, D), p),
      ],
      out_specs=_token_specs(False, D, bm)[0],
      out_shape=_token_shapes(False, D, BF16)[0],
      compiler_params=_params(1),
      name="pool",
  )(x, x, x, ada, gains, _pool_bands(), w_pool, pool_scale.reshape(-1, 1, D))


def _ffn_kernel(*refs, n_x, n_out, has_wo, cast_next):
  ahead = not has_wo
  x_refs, refs = refs[:n_x], refs[n_x:]
  if ahead:
    xn_refs, refs = refs[:n_x], refs[n_x:]
    (x1_scr, h_scr), refs = refs[-2:], refs[:-2]
  if cast_next:
    cast_dst, refs = refs[-2:], refs[:-2]
  o_refs, refs = refs[-n_out:], refs[:-n_out]
  if cast_next:
    cast_src, refs = refs[-2:], refs[:-2]
  if has_wo:
    mix_ref, ada_ref, g_ref, wo_ref, wup_ref, wdn_ref = refs
  else:
    mix_ref, mixn_ref, ada_ref, g_ref, wup_ref, wdn_ref = refs
  i = pl.program_id(0)
  nxt = jnp.minimum(i + 1, pl.num_programs(0) - 1)
  n_ctx = _n_ctx_blk(FFN_BM)
  mod = _mod_rows(ada_ref, _cond_row(i, FFN_BM))
  gate_f = mod[5]
  if cast_next:
    _cast_ffn_weight_slab(cast_src, cast_dst)

  def pre_ffn(x, mix, mod):
    _, _, gate_m, shift_f, scale_f, _ = mod
    if has_wo:
      a = jnp.dot(mix, wo_ref[0], preferred_element_type=F32)
    else:
      a = mix.astype(F32)
    x1 = x + gate_m * (_rms_scale(a) * g_ref[0, 1:2])
    return x1, ((_rms_scale(x1) * g_ref[0, 2:3]) * (1.0 + scale_f) + shift_f).astype(BF16)

  group = FFN_BM // FFN_GROUPS
  groups = [slice(r * group, (r + 1) * group) for r in range(FFN_GROUPS)]
  pre_group = lambda rows: pre_ffn(_token_load(i < n_ctx, x_refs, rows), mix_ref[rows, :], mod)
  if ahead:
    @pl.when(i == 0)
    def _():
      x1_scr[...], h_scr[...] = pre_group(groups[0])

    x1_h = [(x1_scr[...], h_scr[...])] + [pre_group(rows) for rows in groups[1:]]
  else:
    x1_h = [pre_group(rows) for rows in groups]
  x1, h = zip(*x1_h)
  gu, y = [], []
  for r in range(FFN_GROUPS + 1):
    if r < FFN_GROUPS:
      gu.append(jnp.dot(h[r], wup_ref[...], preferred_element_type=F32))
    if ahead and r == FFN_GROUPS - 1:
      x1_h_nxt = pre_ffn(_token_load(nxt < n_ctx, xn_refs), mixn_ref[...],
                         _mod_rows(ada_ref, _cond_row(nxt, FFN_BM)))
    if r > 0:
      act = (_silu(gu[r - 1][:, :D_FF]) * gu[r - 1][:, D_FF:]).astype(BF16)
      y.append(jnp.dot(act, wdn_ref[...], preferred_element_type=F32))
  for rows, x1_r, y_r in zip(groups, x1, y):
    _token_store(i < n_ctx, o_refs, x1_r + gate_f * (_rms_scale(y_r) * g_ref[0, 3:4]), rows)
  if ahead:
    x1_scr[...], h_scr[...] = x1_h_nxt


def _next_group_specs(split, width, bm, group):
  n_blk, n_ctx, per_blk = N_TOK // bm, _n_ctx_blk(bm), bm // group
  nxt = lambda i: jnp.minimum(i + 1, n_blk - 1)
  if not split:
    return [pl.BlockSpec((group, width), lambda i: (nxt(i) * per_blk, 0))]
  return [
      pl.BlockSpec((group, width), lambda i: (jnp.minimum(nxt(i), n_ctx - 1) * per_blk, 0)),
      pl.BlockSpec((group, width), lambda i: (jnp.maximum(nxt(i) - n_ctx, 0) * per_blk, 0))]


def _ffn(xs, mix, ada, gains, w_o, w_up, w_down, next_w, layer, split_out):
  has_wo = w_o is not None
  cast_next = next_w is not None
  resident = lambda shape: pl.BlockSpec(shape, lambda i: (0, 0), pipeline_mode=pl.Buffered(1))
  bm = FFN_BM
  n_blk = N_TOK // bm
  group = bm // FFN_GROUPS
  split_in = len(xs) == 2
  ahead = not has_wo
  nxt_specs = lambda split: _next_group_specs(split, D, bm, group) if ahead else []
  in_specs = (_token_specs(split_in, D, bm) + nxt_specs(split_in)
              + _token_specs(False, D, bm) + nxt_specs(False) + [
                  _layer_spec((N_MOD, N_COND, D), layer),
                  _layer_spec((4, D), layer),
              ])
  args = list(xs) * (2 if ahead else 1) + [mix] * (2 if ahead else 1) + [ada, gains]
  if has_wo:
    in_specs.append(_layer_spec((Q_W, D), layer // 2))
    args.append(w_o)
  in_specs += [resident((D, 2 * D_FF)), resident((D_FF, D))]
  args += [w_up, w_down]
  n_out = 2 if split_out else 1
  out_specs = _token_specs(split_out, D, bm)
  out_shape = _token_shapes(split_out, D, F32)
  if cast_next:
    cast_in, cast_out, cast_shape = _ffn_weight_cast_specs(layer + 1, n_blk, lambda i: i)
    in_specs += cast_in
    args += list(next_w)
    out_specs += cast_out
    out_shape += cast_shape
  outs = pl.pallas_call(
      functools.partial(_ffn_kernel, n_x=len(xs), n_out=n_out, has_wo=has_wo,
                        cast_next=cast_next),
      grid=(n_blk,),
      in_specs=in_specs,
      out_specs=out_specs,
      out_shape=out_shape,
      scratch_shapes=[pltpu.VMEM((group, D), F32), pltpu.VMEM((group, D), BF16)] if ahead else [],
      compiler_params=_params(1),
      name="ffn_attn" if has_wo else "ffn_pool",
  )(*args)
  return outs[:n_out], outs[n_out:]


def _rope_tables():
  pos = np.arange(LAT_T)
  row = (pos // GRID_W).astype(np.float32)
  col = (pos % GRID_W).astype(np.float32)
  inv = np.float32(ROPE_BASE) ** (-np.arange(ROPE_PAIRS, dtype=np.float32) / np.float32(ROPE_PAIRS))
  ar = row[:, None] * inv
  ac = col[:, None] * inv
  cos = np.concatenate([np.cos(ar), np.cos(ac), np.cos(ar), np.cos(ac)], axis=1)
  sin = np.concatenate([-np.sin(ar), -np.sin(ac), np.sin(ar), np.sin(ac)], axis=1)
  return jnp.asarray(np.concatenate([cos, sin], axis=1), dtype=F32)


def _paired_qk_columns(w_qkv, qk_gains):
  def paired(a):
    blocks = a.reshape(a.shape[:-1] + (-1, 2, 2, ROPE_PAIRS))
    return jnp.swapaxes(blocks, -3, -2).reshape(a.shape)

  w = jnp.concatenate([paired(w_qkv[..., :Q_W + KV_W]), w_qkv[..., Q_W + KV_W:]], axis=-1)
  return w, paired(qk_gains)


def kernel(x_prompt, x_sample, c, cache_k, cache_v, c_ctx, w_ada, b_ada, norm_gains,
           w_qkv, qk_gains, w_o, w_pool, pool_scale, w_up, w_down):
  xs = [x_prompt.reshape(N_CTX_TOK, D), x_sample.reshape(N_LAT_TOK, D)]
  cond = jnp.concatenate(
      [c_ctx[None, :], c, jnp.zeros((N_COND - 1 - N_LAT_SEQ, D), F32)], axis=0)
  ada = _modulation(cond, w_ada, b_ada)
  rope_cs = _rope_tables()
  w_qkv_b, qk_gains_p = _paired_qk_columns(w_qkv.astype(BF16), qk_gains)
  w_o_b = w_o.astype(BF16)
  w_pool_b = w_pool.astype(BF16)

  new_kv = ffn_w = None
  for layer in range(DEPTH):
    last = layer == DEPTH - 1
    next_w = None if last else (w_up, w_down)
    if layer % 2 == 0:
      q, k, v, *new_kv = _qkv(xs, ada, norm_gains, w_qkv_b, qk_gains_p, rope_cs, new_kv, layer)
      mix, cast_w = _attn_latent(q, k, v, cache_k, cache_v, _attn_context(q, k, v), layer // 2,
                                 (w_up, w_down) if ffn_w is None else None)
      ffn_w = ffn_w or cast_w
      xs, ffn_w = _ffn(xs, mix, ada, norm_gains, w_o_b, *ffn_w, next_w, layer, last)
    else:
      mix = _pool(xs[0], ada, norm_gains, w_pool_b, pool_scale, layer)
      xs, ffn_w = _ffn(xs, mix, ada, norm_gains, None, *ffn_w, next_w, layer, last)

  out_kv = (N_CTX_SEQ, N_ATTN, CTX_T, N_KV, HD)
  return (xs[0].reshape(N_CTX_SEQ, CTX_T, D), xs[1].reshape(N_LAT_SEQ, LAT_T, D),
          new_kv[0].reshape(out_kv), new_kv[1].reshape(out_kv))
```

```python
import functools
import math

import jax
import jax.numpy as jnp
import numpy as np
from jax import lax
from jax.experimental import pallas as pl
from jax.experimental.pallas import tpu as pltpu

F32 = jnp.float32
BF16 = jnp.bfloat16

D = 1024
N_CTX_SEQ = 32
CTX_T = 256
N_LAT_SEQ = 4
LAT_T = 2048
PAST = 512
DEPTH = 4
N_ATTN = 2
GRID_W = 64
HD = 128
N_HEADS = 8
N_KV = 2
GQA = N_HEADS // N_KV
Q_W = N_HEADS * HD
KV_W = N_KV * HD
QKV_W = Q_W + 2 * KV_W
ROPE_PAIRS = HD // 4
ROPE_BASE = 10000.0
POOL_WINDOWS = (2, 4, 8, 16)
N_POOL_G = len(POOL_WINDOWS)
POOL_DG = D // N_POOL_G
D_FF = 2816
N_MOD = 6
EPS = 1e-6

N_CTX_TOK = N_CTX_SEQ * CTX_T
N_LAT_TOK = N_LAT_SEQ * LAT_T
N_TOK = N_CTX_TOK + N_LAT_TOK
N_COND = 8
Q_SCALE = math.log2(math.e) / math.sqrt(HD)

VMEM_LIMIT = 56 * 1024 * 1024

FFN_BM = 512
QKV_BM = 512
POOL_BM = 1024
FFN_GROUPS = 2
POOL_TILE = CTX_T
POOL_HALO = max(POOL_WINDOWS) // 2
ATT_TQ = 1024
ATT_CHUNK = 256
CTX_NB = 4


def _n_ctx_blk(bm):
  return N_CTX_TOK // bm


def _token_specs(split, width, bm):
  n_ctx = _n_ctx_blk(bm)
  if not split:
    return [pl.BlockSpec((bm, width), lambda i: (i, 0))]
  return [pl.BlockSpec((bm, width), lambda i: (jnp.minimum(i, n_ctx - 1), 0)),
          pl.BlockSpec((bm, width), lambda i: (jnp.maximum(i - n_ctx, 0), 0))]


def _token_shapes(split, width, dtype):
  if not split:
    return [jax.ShapeDtypeStruct((N_TOK, width), dtype)]
  return [jax.ShapeDtypeStruct((N_CTX_TOK, width), dtype),
          jax.ShapeDtypeStruct((N_LAT_TOK, width), dtype)]


def _token_load(is_ctx, refs, rows=slice(None)):
  if len(refs) == 1:
    return refs[0][rows, :]
  return jnp.where(is_ctx, refs[0][rows, :], refs[1][rows, :])


def _token_store(is_ctx, refs, value, rows=slice(None)):
  if len(refs) == 1:
    refs[0][rows, :] = value
    return

  @pl.when(is_ctx)
  def _():
    refs[0][rows, :] = value

  @pl.when(jnp.logical_not(is_ctx))
  def _():
    refs[1][rows, :] = value


def _lat_seq_pos(i, bm):
  lat_blk = jnp.maximum(i - _n_ctx_blk(bm), 0)
  return lat_blk // (LAT_T // bm), lat_blk % (LAT_T // bm)


def _cond_row(i, bm):
  return jnp.where(i < _n_ctx_blk(bm), 0, 1 + _lat_seq_pos(i, bm)[0])


def _mod_rows(ada_ref, row):
  return [ada_ref[0, m, pl.ds(row, 1), :] for m in range(N_MOD)]


def _layer_spec(shape, layer):
  zeros = (0,) * len(shape)
  return pl.BlockSpec((1,) + shape, lambda *_: (layer,) + zeros, pipeline_mode=pl.Buffered(1))


def _rms_scale(x):
  return x * lax.rsqrt(jnp.mean(x * x, axis=-1, keepdims=True) + EPS)


def _silu(x):
  return x / (1.0 + jnp.exp(-x))


BF16_SUBLANES = 16


def _ffn_weight_cast_specs(layer, n_steps, step_of):
  up_rows = D // n_steps
  dn_blks = n_steps
  while (D_FF // dn_blks) % BF16_SUBLANES:
    dn_blks //= 2
  dn_rows = D_FF // dn_blks
  up_blk = lambda *g: step_of(*g)
  dn_blk = lambda *g: jnp.minimum(step_of(*g), dn_blks - 1)
  in_specs = [pl.BlockSpec((1, up_rows, 2 * D_FF), lambda *g: (layer, up_blk(*g), 0)),
              pl.BlockSpec((1, dn_rows, D), lambda *g: (layer, dn_blk(*g), 0))]
  out_specs = [pl.BlockSpec((up_rows, 2 * D_FF), lambda *g: (up_blk(*g), 0)),
               pl.BlockSpec((dn_rows, D), lambda *g: (dn_blk(*g), 0))]
  out_shape = [jax.ShapeDtypeStruct((D, 2 * D_FF), BF16),
               jax.ShapeDtypeStruct((D_FF, D), BF16)]
  return in_specs, out_specs, out_shape


def _cast_ffn_weight_slab(src_refs, dst_refs):
  for src, dst in zip(src_refs, dst_refs):
    dst[...] = src[0].astype(BF16)


def _params(n_grid_dims):
  return pltpu.CompilerParams(
      dimension_semantics=("arbitrary",) * n_grid_dims, vmem_limit_bytes=VMEM_LIMIT)


MOD_PER_STEP = 3


def _mod_kernel(cond_ref, w_ref, b_ref, o_ref):
  s = _silu(cond_ref[...]).astype(BF16)
  for m in range(MOD_PER_STEP):
    w = w_ref[0, :, m * D:(m + 1) * D].astype(BF16)
    o_ref[0, m] = jnp.dot(s, w, preferred_element_type=F32) + b_ref[0, m]


def _modulation(cond, w_ada, b_ada):
  return pl.pallas_call(
      _mod_kernel,
      grid=(DEPTH, N_MOD // MOD_PER_STEP),
      in_specs=[
          pl.BlockSpec((N_COND, D), lambda l, m: (0, 0)),
          pl.BlockSpec((1, D, MOD_PER_STEP * D), lambda l, m: (l, 0, m)),
          pl.BlockSpec((1, MOD_PER_STEP, 1, D), lambda l, m: (l, m, 0, 0)),
      ],
      out_specs=pl.BlockSpec((1, MOD_PER_STEP, N_COND, D), lambda l, m: (l, m, 0, 0)),
      out_shape=jax.ShapeDtypeStruct((DEPTH, N_MOD, N_COND, D), F32),
      compiler_params=_params(2),
      name="modulation",
  )(cond, w_ada, b_ada.reshape(DEPTH, N_MOD, 1, D))


def _swap_middle(x):
  blk = lax.broadcasted_iota(jnp.int32, x.shape, 1) // ROPE_PAIRS
  return jnp.where(blk == 1, pltpu.roll(x, HD - ROPE_PAIRS, 1),
                   jnp.where(blk == 2, pltpu.roll(x, ROPE_PAIRS, 1), x))


def _qkv_kernel(*refs, n_x):
  x_refs, refs = refs[:n_x], refs[n_x:]
  ada_ref, g_ref, w_ref, qkg_ref, cs_ref = refs[:5]
  q_ref, k_ref, v_ref, nk_ref, nv_ref = refs[-5:]
  i = pl.program_id(0)
  is_ctx = i < _n_ctx_blk(QKV_BM)
  shift, scale = _mod_rows(ada_ref, _cond_row(i, QKV_BM))[:2]
  gain = g_ref[0, 0:1] * (1.0 + scale)
  gq = qkg_ref[0, 0:1] * Q_SCALE
  gk = qkg_ref[0, 1:2]
  groups = [slice(s * CTX_T, (s + 1) * CTX_T) for s in range(QKV_BM // CTX_T)]

  def body(x_ref, latent):
    h = [(_rms_scale(x_ref[rows, :]) * gain + shift).astype(BF16) for rows in groups]
    qkv = [jnp.dot(h_s, w_ref[0], preferred_element_type=F32) for h_s in h]
    for s, (rows, qkv_s) in enumerate(zip(groups, qkv)):
      if latent:
        cos = cs_ref[rows, :HD]
        sin = cs_ref[rows, HD:]
        rope = lambda x: x * cos + pltpu.roll(x, HD // 2, 1) * sin
      else:
        rope = lambda x: x
      for hd in range(N_HEADS):
        cols = slice(hd * HD, (hd + 1) * HD)
        q_ref[rows, cols] = rope(_rms_scale(qkv_s[:, cols]) * gq).astype(BF16)
      v = qkv_s[:, Q_W + KV_W:]
      v_ref[rows, :] = v.astype(BF16)
      for hd in range(N_KV):
        cols = slice(hd * HD, (hd + 1) * HD)
        k = rope(_rms_scale(qkv_s[:, Q_W + hd * HD:Q_W + (hd + 1) * HD]) * gk)
        k_ref[rows, cols] = k.astype(BF16)
        if not latent:
          nk_ref[s, 0, pl.ds(hd, CTX_T, stride=N_KV), :] = _swap_middle(k)
          nv_ref[s, 0, pl.ds(hd, CTX_T, stride=N_KV), :] = v[:, cols]

  @pl.when(is_ctx)
  def _():
    body(x_refs[0], latent=False)

  @pl.when(jnp.logical_not(is_ctx))
  def _():
    body(x_refs[-1], latent=True)


def _qkv(xs, ada, gains, w_qkv, qk_gains, rope_cs, new_kv, layer):
  a = layer // 2
  bm = QKV_BM
  new_spec = pl.BlockSpec(
      (bm // CTX_T, 1, CTX_T * N_KV, HD),
      lambda i: (jnp.minimum(i, _n_ctx_blk(bm) - 1), a, 0, 0))
  new_shape = jax.ShapeDtypeStruct((N_CTX_SEQ, N_ATTN, CTX_T * N_KV, HD), F32)
  args = list(xs) + [ada, gains, w_qkv, qk_gains, rope_cs]
  in_specs = _token_specs(len(xs) == 2, D, bm) + [
      _layer_spec((N_MOD, N_COND, D), layer),
      _layer_spec((4, D), layer),
      _layer_spec((D, QKV_W), a),
      _layer_spec((2, HD), a),
      pl.BlockSpec((bm, 2 * HD), lambda i: (_lat_seq_pos(i, bm)[1], 0)),
  ]
  aliases = {}
  if new_kv is not None:
    aliases = {len(args): 3, len(args) + 1: 4}
    args += list(new_kv)
    in_specs += [pl.BlockSpec(memory_space=pl.ANY)] * 2
  return pl.pallas_call(
      functools.partial(_qkv_kernel, n_x=len(xs)),
      grid=(N_TOK // bm,),
      in_specs=in_specs,
      out_specs=(_token_specs(False, Q_W, bm) + _token_specs(False, KV_W, bm) * 2
                 + [new_spec] * 2),
      out_shape=(_token_shapes(False, Q_W, BF16) + _token_shapes(False, KV_W, BF16) * 2
                 + [new_shape] * 2),
      input_output_aliases=aliases,
      compiler_params=_params(1),
      name="qkv",
  )(*args)


def _nt_dot(a, b):
  return lax.dot_general(a, b, (((1,), (1,)), ((), ())), preferred_element_type=F32)


def _with_ones(v):
  return jnp.concatenate([v, jnp.ones_like(v)], axis=1)


def _softmax_pv(s, v1):
  p = jnp.exp2(s - jnp.max(s, axis=-1, keepdims=True)).astype(BF16)
  oa = jnp.dot(p, v1, preferred_element_type=F32)
  return oa[:, :HD] / oa[:, HD:]


def _attn_ctx_kernel(q_ref, k_ref, v_ref, o_ref):
  for b in range(CTX_NB):
    rows = slice(b * CTX_T, (b + 1) * CTX_T)
    for h in range(N_KV):
      kv_cols = slice(h * HD, (h + 1) * HD)
      v1 = _with_ones(v_ref[rows, kv_cols])
      for g in range(GQA):
        cols = slice((h * GQA + g) * HD, (h * GQA + g + 1) * HD)
        s = _nt_dot(q_ref[rows, cols], k_ref[rows, kv_cols])
        o_ref[rows, cols] = _softmax_pv(s, v1).astype(BF16)


def _attn_context(q, k, v):
  bm = CTX_NB * CTX_T
  row = lambda i: (i, 0)
  return pl.pallas_call(
      _attn_ctx_kernel,
      grid=(N_CTX_TOK // bm,),
      in_specs=[
          pl.BlockSpec((bm, Q_W), row),
          pl.BlockSpec((bm, KV_W), row),
          pl.BlockSpec((bm, KV_W), row),
      ],
      out_specs=pl.BlockSpec((bm, Q_W), row),
      out_shape=jax.ShapeDtypeStruct((N_TOK, Q_W), BF16),
      compiler_params=_params(1),
      name="attn_context",
  )(q, k, v)


def _attn_lat_kernel(*refs, cast_weights):
  q_ref, k_ref, v_ref, ck_ref, cv_ref, _ = refs[:6]
  kcat, vcat = refs[-2:]
  if cast_weights:
    _cast_ffn_weight_slab(refs[6:8], refs[-4:-2])
    o_ref = refs[8]
  else:
    o_ref = refs[6]

  @pl.when(pl.program_id(2) == 0)
  def _():
    head_rows = pl.ds(pl.program_id(1), PAST, stride=N_KV)
    kcat[:LAT_T] = k_ref[...]
    kcat[LAT_T:] = _swap_middle(ck_ref[0, 0, head_rows, :]).astype(BF16)
    vcat[:LAT_T] = _with_ones(v_ref[...])
    vcat[LAT_T:] = _with_ones(cv_ref[0, 0, head_rows, :].astype(BF16))

  chains = [(g, r, ATT_CHUNK) for g in range(GQA) for r in range(0, ATT_TQ, ATT_CHUNK)]
  halves = lambda g, r, n: [(g, r, n // 2), (g, r + n // 2, n // 2)]
  chains = halves(*chains[0]) + chains[1:-1] + halves(*chains[-1])
  for g, r, n in chains:
    rows, cols = slice(r, r + n), slice(g * HD, (g + 1) * HD)
    s = _nt_dot(q_ref[rows, cols], kcat[...])
    o_ref[rows, cols] = _softmax_pv(s, vcat[...]).astype(BF16)


def _attn_latent(q, k, v, cache_k, cache_v, mix, a, cast_w):
  nq = LAT_T // ATT_TQ
  q0 = N_CTX_TOK // ATT_TQ
  kv0 = N_CTX_TOK // LAT_T
  cache_rows = (N_LAT_SEQ, N_ATTN, PAST * N_KV, HD)
  cache_spec = pl.BlockSpec((1, 1, PAST * N_KV, HD), lambda b, h, j: (b, a, 0, 0))
  in_specs = [
      pl.BlockSpec((ATT_TQ, GQA * HD), lambda b, h, j: (q0 + b * nq + j, h)),
      pl.BlockSpec((LAT_T, HD), lambda b, h, j: (kv0 + b, h)),
      pl.BlockSpec((LAT_T, HD), lambda b, h, j: (kv0 + b, h)),
      cache_spec,
      cache_spec,
      pl.BlockSpec(memory_space=pl.ANY),
  ]
  args = [q, k, v, cache_k.reshape(cache_rows), cache_v.reshape(cache_rows), mix]
  out_specs = [pl.BlockSpec((ATT_TQ, GQA * HD), lambda b, h, j: (q0 + b * nq + j, h))]
  out_shape = [jax.ShapeDtypeStruct((N_TOK, Q_W), BF16)]
  if cast_w is not None:
    cast_in, cast_out, cast_shape = _ffn_weight_cast_specs(
        0, N_LAT_SEQ * N_KV * nq, lambda b, h, j: (b * N_KV + h) * nq + j)
    in_specs += cast_in
    args += list(cast_w)
    out_specs += cast_out
    out_shape += cast_shape
  outs = pl.pallas_call(
      functools.partial(_attn_lat_kernel, cast_weights=cast_w is not None),
      grid=(N_LAT_SEQ, N_KV, nq),
      in_specs=in_specs,
      out_specs=out_specs,
      out_shape=out_shape,
      scratch_shapes=[pltpu.VMEM((LAT_T + PAST, HD), BF16),
                      pltpu.VMEM((LAT_T + PAST, 2 * HD), BF16)],
      input_output_aliases={5: 0},
      compiler_params=_params(3),
      name="attn_latent",
  )(*args)
  return outs[0], outs[1:]


def _edge_window_sums(slab, w):
  rows = slab.shape[0]
  back, fwd, m = slab, slab, 1
  while m < w // 2:
    back = back + pltpu.roll(back, m, 0)
    fwd = fwd + pltpu.roll(fwd, rows - m, 0)
    m *= 2
  return (pltpu.roll(back, 1, 0) + fwd)[POOL_HALO:2 * POOL_HALO]


def _pool_bands():
  t = np.arange(POOL_TILE)[:, None]
  j = np.arange(POOL_TILE)[None, :]
  bands = [((j >= t - w // 2) & (j < t + w - w // 2)) / w for w in POOL_WINDOWS]
  return jnp.asarray(np.stack(bands), dtype=BF16)


def _pool_kernel(x_ref, xprev_ref, xnext_ref, ada_ref, g_ref, band_ref, w_ref, ps_ref, o_ref):
  i = pl.program_id(0)
  is_ctx = i < _n_ctx_blk(POOL_BM)
  shift, scale = _mod_rows(ada_ref, _cond_row(i, POOL_BM))[:2]
  gain = g_ref[0, 0:1] * (1.0 + scale)
  pre = lambda x: _rms_scale(x) * gain + shift
  h = pre(x_ref[...])
  seq_pos = _lat_seq_pos(i, POOL_BM)[1]
  is_first = jnp.logical_or(is_ctx, seq_pos == 0)
  is_last = jnp.logical_or(is_ctx, seq_pos == LAT_T // POOL_BM - 1)
  ctx_edge = jnp.where(is_ctx, 1.0, 0.0)
  first_edge = jnp.where(is_first, 1.0, 0.0)
  last_edge = jnp.where(is_last, 1.0, 0.0)
  h_prev = jnp.where(is_first, 0.0, pre(xprev_ref[...]))
  h_next = jnp.where(is_last, 0.0, pre(xnext_ref[...]))
  h_hi = h.astype(BF16)
  h_lo = (h - h_hi.astype(F32)).astype(BF16)
  n_tiles = POOL_BM // POOL_TILE
  row = lax.broadcasted_iota(jnp.int32, (POOL_HALO, POOL_DG), 0)
  units = [(g, w, j) for g, w in enumerate(POOL_WINDOWS) for j in range(n_tiles)]
  col_of = lambda g: slice(g * POOL_DG, (g + 1) * POOL_DG)
  row_of = lambda j: slice(j * POOL_TILE, (j + 1) * POOL_TILE)

  parts = [jnp.dot(band_ref[g],
                   jnp.concatenate([h_hi[row_of(j), col_of(g)], h_lo[row_of(j), col_of(g)]], axis=1),
                   preferred_element_type=F32) for g, w, j in units]

  pooled = []
  for (g, w, j), part in zip(units, parts):
    r0, cols = j * POOL_TILE, col_of(g)
    hg = h[row_of(j), cols]
    inner = part[:, :POOL_DG] + part[:, POOL_DG:]
    before = (h_prev[:, cols] if j == 0
              else jnp.where(is_ctx, 0.0, h[r0 - POOL_HALO:r0, cols]))
    after = (h_next[:, cols] if j == n_tiles - 1
             else jnp.where(is_ctx, 0.0, h[r0 + POOL_TILE:r0 + POOL_TILE + POOL_HALO, cols]))
    top = _edge_window_sums(jnp.concatenate([before, hg[:2 * POOL_HALO]], axis=0), w)
    bot = _edge_window_sums(jnp.concatenate([hg[-2 * POOL_HALO:], after], axis=0), w)
    starts = first_edge if j == 0 else ctx_edge
    ends = last_edge if j == n_tiles - 1 else ctx_edge
    top_cnt = w - jnp.maximum(w // 2 - row, 0).astype(F32) * starts
    bot_cnt = w - jnp.maximum(row + (w - w // 2) - POOL_HALO, 0).astype(F32) * ends
    mean = jnp.concatenate(
        [top / top_cnt, inner[POOL_HALO:-POOL_HALO], bot / bot_cnt], axis=0)
    pooled.append((mean - hg).astype(BF16))

  for (g, w, j), p in zip(units, pooled):
    og = jnp.dot(p, w_ref[0, g], preferred_element_type=F32)
    o_ref[row_of(j), col_of(g)] = (og * ps_ref[0, :, col_of(g)]).astype(BF16)


def _pool(x, ada, gains, w_pool, pool_scale, layer):
  p = layer // 2
  bm = POOL_BM
  halo_per_blk = bm // POOL_HALO
  n_halo = N_TOK // POOL_HALO
  prev_spec = pl.BlockSpec((POOL_HALO, D), lambda i: (jnp.maximum(i * halo_per_blk - 1, 0), 0))
  next_spec = pl.BlockSpec(
      (POOL_HALO, D), lambda i: (jnp.minimum((i + 1) * halo_per_blk, n_halo - 1), 0))
  return pl.pallas_call(
      _pool_kernel,
      grid=(N_TOK // bm,),
      in_specs=_token_specs(False, D, bm) + [
          prev_spec, next_spec,
          _layer_spec((N_MOD, N_COND, D), layer),
          _layer_spec((4, D), layer),
          pl.BlockSpec((N_POOL_G, POOL_TILE, POOL_TILE), lambda i: (0, 0, 0),
                       pipeline_mode=pl.Buffered(1)),
          _layer_spec((N_POOL_G, POOL_DG, POOL_DG), p),
          _layer_spec((1, D), p),
      ],
      out_specs=_token_specs(False, D, bm)[0],
      out_shape=_token_shapes(False, D, BF16)[0],
      compiler_params=_params(1),
      name="pool",
  )(x, x, x, ada, gains, _pool_bands(), w_pool, pool_scale.reshape(-1, 1, D))


def _ffn_kernel(*refs, n_x, n_out, has_wo, cast_next):
  ahead = not has_wo
  x_refs, refs = refs[:n_x], refs[n_x:]
  if ahead:
    xn_refs, refs = refs[:n_x], refs[n_x:]
    (x1_scr, h_scr), refs = refs[-2:], refs[:-2]
  if cast_next:
    cast_dst, refs = refs[-2:], refs[:-2]
  o_refs, refs = refs[-n_out:], refs[:-n_out]
  if cast_next:
    cast_src, refs = refs[-2:], refs[:-2]
  if has_wo:
    mix_ref, ada_ref, g_ref, wo_ref, wup_ref, wdn_ref = refs
  else:
    mix_ref, mixn_ref, ada_ref, g_ref, wup_ref, wdn_ref = refs
  i = pl.program_id(0)
  nxt = jnp.minimum(i + 1, pl.num_programs(0) - 1)
  n_ctx = _n_ctx_blk(FFN_BM)
  mod = _mod_rows(ada_ref, _cond_row(i, FFN_BM))
  post_gain = mod[5] * g_ref[0, 3:4]
  if cast_next:
    _cast_ffn_weight_slab(cast_src, cast_dst)

  def pre_ffn(x, mix, mod):
    _, _, gate_m, shift_f, scale_f, _ = mod
    if has_wo:
      a = jnp.dot(mix, wo_ref[0], preferred_element_type=F32)
    else:
      a = mix.astype(F32)
    x1 = x + _rms_scale(a) * (gate_m * g_ref[0, 1:2])
    return x1, (_rms_scale(x1) * (g_ref[0, 2:3] * (1.0 + scale_f)) + shift_f).astype(BF16)

  group = FFN_BM // FFN_GROUPS
  groups = [slice(r * group, (r + 1) * group) for r in range(FFN_GROUPS)]
  pre_group = lambda rows: pre_ffn(_token_load(i < n_ctx, x_refs, rows), mix_ref[rows, :], mod)
  if ahead:
    @pl.when(i == 0)
    def _():
      x1_scr[...], h_scr[...] = pre_group(groups[0])

    x1_h = [(x1_scr[...], h_scr[...])] + [pre_group(rows) for rows in groups[1:]]
  else:
    x1_h = [pre_group(rows) for rows in groups]
  x1, h = zip(*x1_h)
  gu, y = [], []
  for r in range(FFN_GROUPS + 1):
    if r < FFN_GROUPS:
      gu.append(jnp.dot(h[r], wup_ref[...], preferred_element_type=F32))
    if ahead and r == FFN_GROUPS - 1:
      x1_h_nxt = pre_ffn(_token_load(nxt < n_ctx, xn_refs), mixn_ref[...],
                         _mod_rows(ada_ref, _cond_row(nxt, FFN_BM)))
    if r > 0:
      act = (_silu(gu[r - 1][:, :D_FF]) * gu[r - 1][:, D_FF:]).astype(BF16)
      y.append(jnp.dot(act, wdn_ref[...], preferred_element_type=F32))
  for rows, x1_r, y_r in zip(groups, x1, y):
    _token_store(i < n_ctx, o_refs, x1_r + _rms_scale(y_r) * post_gain, rows)
  if ahead:
    x1_scr[...], h_scr[...] = x1_h_nxt


def _next_group_specs(split, width, bm, group):
  n_blk, n_ctx, per_blk = N_TOK // bm, _n_ctx_blk(bm), bm // group
  nxt = lambda i: jnp.minimum(i + 1, n_blk - 1)
  if not split:
    return [pl.BlockSpec((group, width), lambda i: (nxt(i) * per_blk, 0))]
  return [
      pl.BlockSpec((group, width), lambda i: (jnp.minimum(nxt(i), n_ctx - 1) * per_blk, 0)),
      pl.BlockSpec((group, width), lambda i: (jnp.maximum(nxt(i) - n_ctx, 0) * per_blk, 0))]


def _ffn(xs, mix, ada, gains, w_o, w_up, w_down, next_w, layer, split_out):
  has_wo = w_o is not None
  cast_next = next_w is not None
  resident = lambda shape: pl.BlockSpec(shape, lambda i: (0, 0), pipeline_mode=pl.Buffered(1))
  bm = FFN_BM
  n_blk = N_TOK // bm
  group = bm // FFN_GROUPS
  split_in = len(xs) == 2
  ahead = not has_wo
  nxt_specs = lambda split: _next_group_specs(split, D, bm, group) if ahead else []
  in_specs = (_token_specs(split_in, D, bm) + nxt_specs(split_in)
              + _token_specs(False, D, bm) + nxt_specs(False) + [
                  _layer_spec((N_MOD, N_COND, D), layer),
                  _layer_spec((4, D), layer),
              ])
  args = list(xs) * (2 if ahead else 1) + [mix] * (2 if ahead else 1) + [ada, gains]
  if has_wo:
    in_specs.append(_layer_spec((Q_W, D), layer // 2))
    args.append(w_o)
  in_specs += [resident((D, 2 * D_FF)), resident((D_FF, D))]
  args += [w_up, w_down]
  n_out = 2 if split_out else 1
  out_specs = _token_specs(split_out, D, bm)
  out_shape = _token_shapes(split_out, D, F32)
  if cast_next:
    cast_in, cast_out, cast_shape = _ffn_weight_cast_specs(layer + 1, n_blk, lambda i: i)
    in_specs += cast_in
    args += list(next_w)
    out_specs += cast_out
    out_shape += cast_shape
  outs = pl.pallas_call(
      functools.partial(_ffn_kernel, n_x=len(xs), n_out=n_out, has_wo=has_wo,
                        cast_next=cast_next),
      grid=(n_blk,),
      in_specs=in_specs,
      out_specs=out_specs,
      out_shape=out_shape,
      scratch_shapes=[pltpu.VMEM((group, D), F32), pltpu.VMEM((group, D), BF16)] if ahead else [],
      compiler_params=_params(1),
      name="ffn_attn" if has_wo else "ffn_pool",
  )(*args)
  return outs[:n_out], outs[n_out:]


def _rope_tables():
  pos = np.arange(LAT_T)
  row = (pos // GRID_W).astype(np.float32)
  col = (pos % GRID_W).astype(np.float32)
  inv = np.float32(ROPE_BASE) ** (-np.arange(ROPE_PAIRS, dtype=np.float32) / np.float32(ROPE_PAIRS))
  ar = row[:, None] * inv
  ac = col[:, None] * inv
  cos = np.concatenate([np.cos(ar), np.cos(ac), np.cos(ar), np.cos(ac)], axis=1)
  sin = np.concatenate([-np.sin(ar), -np.sin(ac), np.sin(ar), np.sin(ac)], axis=1)
  return jnp.asarray(np.concatenate([cos, sin], axis=1), dtype=F32)


def _paired_qk_columns(w_qkv, qk_gains):
  def paired(a):
    blocks = a.reshape(a.shape[:-1] + (-1, 2, 2, ROPE_PAIRS))
    return jnp.swapaxes(blocks, -3, -2).reshape(a.shape)

  w = jnp.concatenate([paired(w_qkv[..., :Q_W + KV_W]), w_qkv[..., Q_W + KV_W:]], axis=-1)
  return w, paired(qk_gains)


def kernel(x_prompt, x_sample, c, cache_k, cache_v, c_ctx, w_ada, b_ada, norm_gains,
           w_qkv, qk_gains, w_o, w_pool, pool_scale, w_up, w_down):
  xs = [x_prompt.reshape(N_CTX_TOK, D), x_sample.reshape(N_LAT_TOK, D)]
  cond = jnp.concatenate(
      [c_ctx[None, :], c, jnp.zeros((N_COND - 1 - N_LAT_SEQ, D), F32)], axis=0)
  ada = _modulation(cond, w_ada, b_ada)
  rope_cs = _rope_tables()
  w_qkv_b, qk_gains_p = _paired_qk_columns(w_qkv.astype(BF16), qk_gains)
  w_o_b = w_o.astype(BF16)
  w_pool_b = w_pool.astype(BF16)

  new_kv = ffn_w = None
  for layer in range(DEPTH):
    last = layer == DEPTH - 1
    next_w = None if last else (w_up, w_down)
    if layer % 2 == 0:
      q, k, v, *new_kv = _qkv(xs, ada, norm_gains, w_qkv_b, qk_gains_p, rope_cs, new_kv, layer)
      mix, cast_w = _attn_latent(q, k, v, cache_k, cache_v, _attn_context(q, k, v), layer // 2,
                                 (w_up, w_down) if ffn_w is None else None)
      ffn_w = ffn_w or cast_w
      xs, ffn_w = _ffn(xs, mix, ada, norm_gains, w_o_b, *ffn_w, next_w, layer, last)
    else:
      mix = _pool(xs[0], ada, norm_gains, w_pool_b, pool_scale, layer)
      xs, ffn_w = _ffn(xs, mix, ada, norm_gains, None, *ffn_w, next_w, layer, last)

  out_kv = (N_CTX_SEQ, N_ATTN, CTX_T, N_KV, HD)
  return (xs[0].reshape(N_CTX_SEQ, CTX_T, D), xs[1].reshape(N_LAT_SEQ, LAT_T, D),
          new_kv[0].reshape(out_kv), new_kv[1].reshape(out_kv))
```

```python
import functools
import math

import jax
import jax.numpy as jnp
import numpy as np
from jax import lax
from jax.experimental import pallas as pl
from jax.experimental.pallas import tpu as pltpu

F32 = jnp.float32
BF16 = jnp.bfloat16

D = 1024
N_CTX_SEQ = 32
CTX_T = 256
N_LAT_SEQ = 4
LAT_T = 2048
PAST = 512
DEPTH = 4
N_ATTN = 2
GRID_W = 64
HD = 128
N_HEADS = 8
N_KV = 2
GQA = N_HEADS // N_KV
Q_W = N_HEADS * HD
KV_W = N_KV * HD
QKV_W = Q_W + 2 * KV_W
ROPE_PAIRS = HD // 4
ROPE_BASE = 10000.0
POOL_WINDOWS = (2, 4, 8, 16)
N_POOL_G = len(POOL_WINDOWS)
POOL_DG = D // N_POOL_G
D_FF = 2816
N_MOD = 6
EPS = 1e-6

N_CTX_TOK = N_CTX_SEQ * CTX_T
N_LAT_TOK = N_LAT_SEQ * LAT_T
N_TOK = N_CTX_TOK + N_LAT_TOK
N_COND = 8
Q_SCALE = math.log2(math.e) / math.sqrt(HD)

VMEM_LIMIT = 56 * 1024 * 1024

FFN_BM = 512
QKV_BM = 512
POOL_BM = 1024
FFN_GROUPS = 2
POOL_TILE = CTX_T
POOL_HALO = max(POOL_WINDOWS) // 2
ATT_TQ = 1024
ATT_CHUNK = 256
CTX_NB = 4


def _n_ctx_blk(bm):
  return N_CTX_TOK // bm


def _token_specs(split, width, bm):
  n_ctx = _n_ctx_blk(bm)
  if not split:
    return [pl.BlockSpec((bm, width), lambda i: (i, 0))]
  return [pl.BlockSpec((bm, width), lambda i: (jnp.minimum(i, n_ctx - 1), 0)),
          pl.BlockSpec((bm, width), lambda i: (jnp.maximum(i - n_ctx, 0), 0))]


def _token_shapes(split, width, dtype):
  if not split:
    return [jax.ShapeDtypeStruct((N_TOK, width), dtype)]
  return [jax.ShapeDtypeStruct((N_CTX_TOK, width), dtype),
          jax.ShapeDtypeStruct((N_LAT_TOK, width), dtype)]


def _token_load(is_ctx, refs, rows=slice(None)):
  if len(refs) == 1:
    return refs[0][rows, :]
  return jnp.where(is_ctx, refs[0][rows, :], refs[1][rows, :])


def _token_store(is_ctx, refs, value, rows=slice(None)):
  if len(refs) == 1:
    refs[0][rows, :] = value
    return

  @pl.when(is_ctx)
  def _():
    refs[0][rows, :] = value

  @pl.when(jnp.logical_not(is_ctx))
  def _():
    refs[1][rows, :] = value


def _lat_seq_pos(i, bm):
  lat_blk = jnp.maximum(i - _n_ctx_blk(bm), 0)
  return lat_blk // (LAT_T // bm), lat_blk % (LAT_T // bm)


def _cond_row(i, bm):
  return jnp.where(i < _n_ctx_blk(bm), 0, 1 + _lat_seq_pos(i, bm)[0])


def _mod_rows(ada_ref, row):
  return [ada_ref[0, m, pl.ds(row, 1), :] for m in range(N_MOD)]


def _layer_spec(shape, layer):
  zeros = (0,) * len(shape)
  return pl.BlockSpec((1,) + shape, lambda *_: (layer,) + zeros, pipeline_mode=pl.Buffered(1))


def _rms_scale(x):
  return x * lax.rsqrt(jnp.mean(x * x, axis=-1, keepdims=True) + EPS)


def _silu(x):
  half = 0.5 * x
  return half + half * jnp.tanh(half)


BF16_SUBLANES = 16


def _ffn_weight_cast_specs(layer, n_steps, step_of):
  up_rows = D // n_steps
  dn_blks = n_steps
  while (D_FF // dn_blks) % BF16_SUBLANES:
    dn_blks //= 2
  dn_rows = D_FF // dn_blks
  up_blk = lambda *g: step_of(*g)
  dn_blk = lambda *g: jnp.minimum(step_of(*g), dn_blks - 1)
  in_specs = [pl.BlockSpec((1, up_rows, 2 * D_FF), lambda *g: (layer, up_blk(*g), 0)),
              pl.BlockSpec((1, dn_rows, D), lambda *g: (layer, dn_blk(*g), 0))]
  out_specs = [pl.BlockSpec((up_rows, 2 * D_FF), lambda *g: (up_blk(*g), 0)),
               pl.BlockSpec((dn_rows, D), lambda *g: (dn_blk(*g), 0))]
  out_shape = [jax.ShapeDtypeStruct((D, 2 * D_FF), BF16),
               jax.ShapeDtypeStruct((D_FF, D), BF16)]
  return in_specs, out_specs, out_shape


def _cast_ffn_weight_slab(src_refs, dst_refs):
  for src, dst in zip(src_refs, dst_refs):
    dst[...] = src[0].astype(BF16)


def _params(n_grid_dims):
  return pltpu.CompilerParams(
      dimension_semantics=("arbitrary",) * n_grid_dims, vmem_limit_bytes=VMEM_LIMIT)


MOD_PER_STEP = 3


def _mod_kernel(cond_ref, w_ref, b_ref, o_ref):
  s = _silu(cond_ref[...]).astype(BF16)
  for m in range(MOD_PER_STEP):
    w = w_ref[0, :, m * D:(m + 1) * D].astype(BF16)
    o_ref[0, m] = jnp.dot(s, w, preferred_element_type=F32) + b_ref[0, m]


def _modulation(cond, w_ada, b_ada):
  return pl.pallas_call(
      _mod_kernel,
      grid=(DEPTH, N_MOD // MOD_PER_STEP),
      in_specs=[
          pl.BlockSpec((N_COND, D), lambda l, m: (0, 0)),
          pl.BlockSpec((1, D, MOD_PER_STEP * D), lambda l, m: (l, 0, m)),
          pl.BlockSpec((1, MOD_PER_STEP, 1, D), lambda l, m: (l, m, 0, 0)),
      ],
      out_specs=pl.BlockSpec((1, MOD_PER_STEP, N_COND, D), lambda l, m: (l, m, 0, 0)),
      out_shape=jax.ShapeDtypeStruct((DEPTH, N_MOD, N_COND, D), F32),
      compiler_params=_params(2),
      name="modulation",
  )(cond, w_ada, b_ada.reshape(DEPTH, N_MOD, 1, D))


def _swap_middle(x):
  blk = lax.broadcasted_iota(jnp.int32, x.shape, 1) // ROPE_PAIRS
  return jnp.where(blk == 1, pltpu.roll(x, HD - ROPE_PAIRS, 1),
                   jnp.where(blk == 2, pltpu.roll(x, ROPE_PAIRS, 1), x))


def _qkv_kernel(*refs, n_x):
  x_refs, refs = refs[:n_x], refs[n_x:]
  ada_ref, g_ref, w_ref, qkg_ref, cs_ref = refs[:5]
  q_ref, k_ref, v_ref, nk_ref, nv_ref = refs[-5:]
  i = pl.program_id(0)
  is_ctx = i < _n_ctx_blk(QKV_BM)
  shift, scale = _mod_rows(ada_ref, _cond_row(i, QKV_BM))[:2]
  gain = g_ref[0, 0:1] * (1.0 + scale)
  gq = qkg_ref[0, 0:1] * Q_SCALE
  gk = qkg_ref[0, 1:2]
  groups = [slice(s * CTX_T, (s + 1) * CTX_T) for s in range(QKV_BM // CTX_T)]

  def body(x_ref, latent):
    h = [(_rms_scale(x_ref[rows, :]) * gain + shift).astype(BF16) for rows in groups]
    qkv = [jnp.dot(h_s, w_ref[0], preferred_element_type=F32) for h_s in h]
    for s, (rows, qkv_s) in enumerate(zip(groups, qkv)):
      if latent:
        cos = cs_ref[rows, :HD]
        sin = cs_ref[rows, HD:]
        rope = lambda x: x * cos + pltpu.roll(x, HD // 2, 1) * sin
      else:
        rope = lambda x: x
      for hd in range(N_HEADS):
        cols = slice(hd * HD, (hd + 1) * HD)
        q_ref[rows, cols] = rope(_rms_scale(qkv_s[:, cols]) * gq).astype(BF16)
      v = qkv_s[:, Q_W + KV_W:]
      v_ref[rows, :] = v.astype(BF16)
      for hd in range(N_KV):
        cols = slice(hd * HD, (hd + 1) * HD)
        k = rope(_rms_scale(qkv_s[:, Q_W + hd * HD:Q_W + (hd + 1) * HD]) * gk)
        k_ref[rows, cols] = k.astype(BF16)
        if not latent:
          nk_ref[s, 0, pl.ds(hd, CTX_T, stride=N_KV), :] = _swap_middle(k)
          nv_ref[s, 0, pl.ds(hd, CTX_T, stride=N_KV), :] = v[:, cols]

  @pl.when(is_ctx)
  def _():
    body(x_refs[0], latent=False)

  @pl.when(jnp.logical_not(is_ctx))
  def _():
    body(x_refs[-1], latent=True)


def _qkv(xs, ada, gains, w_qkv, qk_gains, rope_cs, new_kv, layer):
  a = layer // 2
  bm = QKV_BM
  new_spec = pl.BlockSpec(
      (bm // CTX_T, 1, CTX_T * N_KV, HD),
      lambda i: (jnp.minimum(i, _n_ctx_blk(bm) - 1), a, 0, 0))
  new_shape = jax.ShapeDtypeStruct((N_CTX_SEQ, N_ATTN, CTX_T * N_KV, HD), F32)
  args = list(xs) + [ada, gains, w_qkv, qk_gains, rope_cs]
  in_specs = _token_specs(len(xs) == 2, D, bm) + [
      _layer_spec((N_MOD, N_COND, D), layer),
      _layer_spec((4, D), layer),
      _layer_spec((D, QKV_W), a),
      _layer_spec((2, HD), a),
      pl.BlockSpec((bm, 2 * HD), lambda i: (_lat_seq_pos(i, bm)[1], 0)),
  ]
  aliases = {}
  if new_kv is not None:
    aliases = {len(args): 3, len(args) + 1: 4}
    args += list(new_kv)
    in_specs += [pl.BlockSpec(memory_space=pl.ANY)] * 2
  return pl.pallas_call(
      functools.partial(_qkv_kernel, n_x=len(xs)),
      grid=(N_TOK // bm,),
      in_specs=in_specs,
      out_specs=(_token_specs(False, Q_W, bm) + _token_specs(False, KV_W, bm) * 2
                 + [new_spec] * 2),
      out_shape=(_token_shapes(False, Q_W, BF16) + _token_shapes(False, KV_W, BF16) * 2
                 + [new_shape] * 2),
      input_output_aliases=aliases,
      compiler_params=_params(1),
      name="qkv",
  )(*args)


def _nt_dot(a, b):
  return lax.dot_general(a, b, (((1,), (1,)), ((), ())), preferred_element_type=F32)


def _with_ones(v):
  return jnp.concatenate([v, jnp.ones_like(v)], axis=1)


def _softmax_pv(s, v1):
  p = jnp.exp2(s - jnp.max(s, axis=-1, keepdims=True)).astype(BF16)
  oa = jnp.dot(p, v1, preferred_element_type=F32)
  return oa[:, :HD] / oa[:, HD:]


def _attn_ctx_kernel(q_ref, k_ref, v_ref, o_ref):
  for b in range(CTX_NB):
    rows = slice(b * CTX_T, (b + 1) * CTX_T)
    for h in range(N_KV):
      kv_cols = slice(h * HD, (h + 1) * HD)
      v1 = _with_ones(v_ref[rows, kv_cols])
      for g in range(GQA):
        cols = slice((h * GQA + g) * HD, (h * GQA + g + 1) * HD)
        s = _nt_dot(q_ref[rows, cols], k_ref[rows, kv_cols])
        o_ref[rows, cols] = _softmax_pv(s, v1).astype(BF16)


def _attn_context(q, k, v):
  bm = CTX_NB * CTX_T
  row = lambda i: (i, 0)
  return pl.pallas_call(
      _attn_ctx_kernel,
      grid=(N_CTX_TOK // bm,),
      in_specs=[
          pl.BlockSpec((bm, Q_W), row),
          pl.BlockSpec((bm, KV_W), row),
          pl.BlockSpec((bm, KV_W), row),
      ],
      out_specs=pl.BlockSpec((bm, Q_W), row),
      out_shape=jax.ShapeDtypeStruct((N_TOK, Q_W), BF16),
      compiler_params=_params(1),
      name="attn_context",
  )(q, k, v)


def _attn_lat_kernel(*refs, cast_weights):
  q_ref, k_ref, v_ref, ck_ref, cv_ref, _ = refs[:6]
  kcat, vcat = refs[-2:]
  if cast_weights:
    _cast_ffn_weight_slab(refs[6:8], refs[-4:-2])
    o_ref = refs[8]
  else:
    o_ref = refs[6]

  @pl.when(pl.program_id(2) == 0)
  def _():
    head_rows = pl.ds(pl.program_id(1), PAST, stride=N_KV)
    kcat[:LAT_T] = k_ref[...]
    kcat[LAT_T:] = _swap_middle(ck_ref[0, 0, head_rows, :]).astype(BF16)
    vcat[:LAT_T] = _with_ones(v_ref[...])
    vcat[LAT_T:] = _with_ones(cv_ref[0, 0, head_rows, :].astype(BF16))

  chains = [(g, r, ATT_CHUNK) for g in range(GQA) for r in range(0, ATT_TQ, ATT_CHUNK)]
  halves = lambda g, r, n: [(g, r, n // 2), (g, r + n // 2, n // 2)]
  chains = halves(*chains[0]) + chains[1:-1] + halves(*chains[-1])
  for g, r, n in chains:
    rows, cols = slice(r, r + n), slice(g * HD, (g + 1) * HD)
    s = _nt_dot(q_ref[rows, cols], kcat[...])
    o_ref[rows, cols] = _softmax_pv(s, vcat[...]).astype(BF16)


def _attn_latent(q, k, v, cache_k, cache_v, mix, a, cast_w):
  nq = LAT_T // ATT_TQ
  q0 = N_CTX_TOK // ATT_TQ
  kv0 = N_CTX_TOK // LAT_T
  cache_rows = (N_LAT_SEQ, N_ATTN, PAST * N_KV, HD)
  cache_spec = pl.BlockSpec((1, 1, PAST * N_KV, HD), lambda b, h, j: (b, a, 0, 0))
  in_specs = [
      pl.BlockSpec((ATT_TQ, GQA * HD), lambda b, h, j: (q0 + b * nq + j, h)),
      pl.BlockSpec((LAT_T, HD), lambda b, h, j: (kv0 + b, h)),
      pl.BlockSpec((LAT_T, HD), lambda b, h, j: (kv0 + b, h)),
      cache_spec,
      cache_spec,
      pl.BlockSpec(memory_space=pl.ANY),
  ]
  args = [q, k, v, cache_k.reshape(cache_rows), cache_v.reshape(cache_rows), mix]
  out_specs = [pl.BlockSpec((ATT_TQ, GQA * HD), lambda b, h, j: (q0 + b * nq + j, h))]
  out_shape = [jax.ShapeDtypeStruct((N_TOK, Q_W), BF16)]
  if cast_w is not None:
    cast_in, cast_out, cast_shape = _ffn_weight_cast_specs(
        0, N_LAT_SEQ * N_KV * nq, lambda b, h, j: (b * N_KV + h) * nq + j)
    in_specs += cast_in
    args += list(cast_w)
    out_specs += cast_out
    out_shape += cast_shape
  outs = pl.pallas_call(
      functools.partial(_attn_lat_kernel, cast_weights=cast_w is not None),
      grid=(N_LAT_SEQ, N_KV, nq),
      in_specs=in_specs,
      out_specs=out_specs,
      out_shape=out_shape,
      scratch_shapes=[pltpu.VMEM((LAT_T + PAST, HD), BF16),
                      pltpu.VMEM((LAT_T + PAST, 2 * HD), BF16)],
      input_output_aliases={5: 0},
      compiler_params=_params(3),
      name="attn_latent",
  )(*args)
  return outs[0], outs[1:]


def _edge_window_sums(slab, w):
  rows = slab.shape[0]
  back, fwd, m = slab, slab, 1
  while m < w // 2:
    back = back + pltpu.roll(back, m, 0)
    fwd = fwd + pltpu.roll(fwd, rows - m, 0)
    m *= 2
  return (pltpu.roll(back, 1, 0) + fwd)[POOL_HALO:2 * POOL_HALO]


def _pool_bands():
  t = np.arange(POOL_TILE)[:, None]
  j = np.arange(POOL_TILE)[None, :]
  bands = [((j >= t - w // 2) & (j < t + w - w // 2)) / w for w in POOL_WINDOWS]
  return jnp.asarray(np.stack(bands), dtype=BF16)


def _pool_kernel(x_ref, xprev_ref, xnext_ref, ada_ref, g_ref, band_ref, w_ref, ps_ref, o_ref):
  i = pl.program_id(0)
  is_ctx = i < _n_ctx_blk(POOL_BM)
  shift, scale = _mod_rows(ada_ref, _cond_row(i, POOL_BM))[:2]
  gain = g_ref[0, 0:1] * (1.0 + scale)
  pre = lambda x: _rms_scale(x) * gain + shift
  h = pre(x_ref[...])
  seq_pos = _lat_seq_pos(i, POOL_BM)[1]
  is_first = jnp.logical_or(is_ctx, seq_pos == 0)
  is_last = jnp.logical_or(is_ctx, seq_pos == LAT_T // POOL_BM - 1)
  ctx_edge = jnp.where(is_ctx, 1.0, 0.0)
  first_edge = jnp.where(is_first, 1.0, 0.0)
  last_edge = jnp.where(is_last, 1.0, 0.0)
  h_prev = jnp.where(is_first, 0.0, pre(xprev_ref[...]))
  h_next = jnp.where(is_last, 0.0, pre(xnext_ref[...]))
  h_hi = h.astype(BF16)
  h_lo = (h - h_hi.astype(F32)).astype(BF16)
  n_tiles = POOL_BM // POOL_TILE
  row = lax.broadcasted_iota(jnp.int32, (POOL_HALO, POOL_DG), 0)
  units = [(g, w, j) for g, w in enumerate(POOL_WINDOWS) for j in range(n_tiles)]
  col_of = lambda g: slice(g * POOL_DG, (g + 1) * POOL_DG)
  row_of = lambda j: slice(j * POOL_TILE, (j + 1) * POOL_TILE)

  parts = [jnp.dot(band_ref[g],
                   jnp.concatenate([h_hi[row_of(j), col_of(g)], h_lo[row_of(j), col_of(g)]], axis=1),
                   preferred_element_type=F32) for g, w, j in units]

  pooled = []
  for (g, w, j), part in zip(units, parts):
    r0, cols = j * POOL_TILE, col_of(g)
    hg = h[row_of(j), cols]
    inner = part[:, :POOL_DG] + part[:, POOL_DG:]
    before = (h_prev[:, cols] if j == 0
              else jnp.where(is_ctx, 0.0, h[r0 - POOL_HALO:r0, cols]))
    after = (h_next[:, cols] if j == n_tiles - 1
             else jnp.where(is_ctx, 0.0, h[r0 + POOL_TILE:r0 + POOL_TILE + POOL_HALO, cols]))
    top = _edge_window_sums(jnp.concatenate([before, hg[:2 * POOL_HALO]], axis=0), w)
    bot = _edge_window_sums(jnp.concatenate([hg[-2 * POOL_HALO:], after], axis=0), w)
    starts = first_edge if j == 0 else ctx_edge
    ends = last_edge if j == n_tiles - 1 else ctx_edge
    top_cnt = w - jnp.maximum(w // 2 - row, 0).astype(F32) * starts
    bot_cnt = w - jnp.maximum(row + (w - w // 2) - POOL_HALO, 0).astype(F32) * ends
    mean = jnp.concatenate(
        [top / top_cnt, inner[POOL_HALO:-POOL_HALO], bot / bot_cnt], axis=0)
    pooled.append((mean - hg).astype(BF16))

  for (g, w, j), p in zip(units, pooled):
    og = jnp.dot(p, w_ref[0, g], preferred_element_type=F32)
    o_ref[row_of(j), col_of(g)] = (og * ps_ref[0, :, col_of(g)]).astype(BF16)


def _pool(x, ada, gains, w_pool, pool_scale, layer):
  p = layer // 2
  bm = POOL_BM
  halo_per_blk = bm // POOL_HALO
  n_halo = N_TOK // POOL_HALO
  prev_spec = pl.BlockSpec((POOL_HALO, D), lambda i: (jnp.maximum(i * halo_per_blk - 1, 0), 0))
  next_spec = pl.BlockSpec(
      (POOL_HALO, D), lambda i: (jnp.minimum((i + 1) * halo_per_blk, n_halo - 1), 0))
  return pl.pallas_call(
      _pool_kernel,
      grid=(N_TOK // bm,),
      in_specs=_token_specs(False, D, bm) + [
          prev_spec, next_spec,
          _layer_spec((N_MOD, N_COND, D), layer),
          _layer_spec((4, D), layer),
          pl.BlockSpec((N_POOL_G, POOL_TILE, POOL_TILE), lambda i: (0, 0, 0),
                       pipeline_mode=pl.Buffered(1)),
          _layer_spec((N_POOL_G, POOL_DG, POOL_DG), p),
          _layer_spec((1, D), p),
      ],
      out_specs=_token_specs(False, D, bm)[0],
      out_shape=_token_shapes(False, D, BF16)[0],
      compiler_params=_params(1),
      name="pool",
  )(x, x, x, ada, gains, _pool_bands(), w_pool, pool_scale.reshape(-1, 1, D))


def _ffn_kernel(*refs, n_x, n_out, has_wo, cast_next):
  ahead = not has_wo
  x_refs, refs = refs[:n_x], refs[n_x:]
  if ahead:
    xn_refs, refs = refs[:n_x], refs[n_x:]
    (x1_scr, h_scr), refs = refs[-2:], refs[:-2]
  if cast_next:
    cast_dst, refs = refs[-2:], refs[:-2]
  o_refs, refs = refs[-n_out:], refs[:-n_out]
  if cast_next:
    cast_src, refs = refs[-2:], refs[:-2]
  if has_wo:
    mix_ref, ada_ref, g_ref, wo_ref, wup_ref, wdn_ref = refs
  else:
    mix_ref, mixn_ref, ada_ref, g_ref, wup_ref, wdn_ref = refs
  i = pl.program_id(0)
  nxt = jnp.minimum(i + 1, pl.num_programs(0) - 1)
  n_ctx = _n_ctx_blk(FFN_BM)
  mod = _mod_rows(ada_ref, _cond_row(i, FFN_BM))
  post_gain = mod[5] * g_ref[0, 3:4]
  if cast_next:
    _cast_ffn_weight_slab(cast_src, cast_dst)

  def pre_ffn(x, mix, mod):
    _, _, gate_m, shift_f, scale_f, _ = mod
    if has_wo:
      a = jnp.dot(mix, wo_ref[0], preferred_element_type=F32)
    else:
      a = mix.astype(F32)
    x1 = x + _rms_scale(a) * (gate_m * g_ref[0, 1:2])
    return x1, (_rms_scale(x1) * (g_ref[0, 2:3] * (1.0 + scale_f)) + shift_f).astype(BF16)

  group = FFN_BM // FFN_GROUPS
  groups = [slice(r * group, (r + 1) * group) for r in range(FFN_GROUPS)]
  pre_group = lambda rows: pre_ffn(_token_load(i < n_ctx, x_refs, rows), mix_ref[rows, :], mod)
  if ahead:
    @pl.when(i == 0)
    def _():
      x1_scr[...], h_scr[...] = pre_group(groups[0])

    x1_h = [(x1_scr[...], h_scr[...])] + [pre_group(rows) for rows in groups[1:]]
  else:
    x1_h = [pre_group(rows) for rows in groups]
  x1, h = zip(*x1_h)
  gu, y = [], []
  for r in range(FFN_GROUPS + 1):
    if r < FFN_GROUPS:
      gu.append(jnp.dot(h[r], wup_ref[...], preferred_element_type=F32))
    if ahead and r == FFN_GROUPS - 1:
      x1_h_nxt = pre_ffn(_token_load(nxt < n_ctx, xn_refs), mixn_ref[...],
                         _mod_rows(ada_ref, _cond_row(nxt, FFN_BM)))
    if r > 0:
      act = (_silu(gu[r - 1][:, :D_FF]) * gu[r - 1][:, D_FF:]).astype(BF16)
      y.append(jnp.dot(act, wdn_ref[...], preferred_element_type=F32))
  for rows, x1_r, y_r in zip(groups, x1, y):
    _token_store(i < n_ctx, o_refs, x1_r + _rms_scale(y_r) * post_gain, rows)
  if ahead:
    x1_scr[...], h_scr[...] = x1_h_nxt


def _next_group_specs(split, width, bm, group):
  n_blk, n_ctx, per_blk = N_TOK // bm, _n_ctx_blk(bm), bm // group
  nxt = lambda i: jnp.minimum(i + 1, n_blk - 1)
  if not split:
    return [pl.BlockSpec((group, width), lambda i: (nxt(i) * per_blk, 0))]
  return [
      pl.BlockSpec((group, width), lambda i: (jnp.minimum(nxt(i), n_ctx - 1) * per_blk, 0)),
      pl.BlockSpec((group, width), lambda i: (jnp.maximum(nxt(i) - n_ctx, 0) * per_blk, 0))]


def _ffn(xs, mix, ada, gains, w_o, w_up, w_down, next_w, layer, split_out):
  has_wo = w_o is not None
  cast_next = next_w is not None
  resident = lambda shape: pl.BlockSpec(shape, lambda i: (0, 0), pipeline_mode=pl.Buffered(1))
  bm = FFN_BM
  n_blk = N_TOK // bm
  group = bm // FFN_GROUPS
  split_in = len(xs) == 2
  ahead = not has_wo
  nxt_specs = lambda split: _next_group_specs(split, D, bm, group) if ahead else []
  in_specs = (_token_specs(split_in, D, bm) + nxt_specs(split_in)
              + _token_specs(False, D, bm) + nxt_specs(False) + [
                  _layer_spec((N_MOD, N_COND, D), layer),
                  _layer_spec((4, D), layer),
              ])
  args = list(xs) * (2 if ahead else 1) + [mix] * (2 if ahead else 1) + [ada, gains]
  if has_wo:
    in_specs.append(_layer_spec((Q_W, D), layer // 2))
    args.append(w_o)
  in_specs += [resident((D, 2 * D_FF)), resident((D_FF, D))]
  args += [w_up, w_down]
  n_out = 2 if split_out else 1
  out_specs = _token_specs(split_out, D, bm)
  out_shape = _token_shapes(split_out, D, F32)
  if cast_next:
    cast_in, cast_out, cast_shape = _ffn_weight_cast_specs(layer + 1, n_blk, lambda i: i)
    in_specs += cast_in
    args += list(next_w)
    out_specs += cast_out
    out_shape += cast_shape
  outs = pl.pallas_call(
      functools.partial(_ffn_kernel, n_x=len(xs), n_out=n_out, has_wo=has_wo,
                        cast_next=cast_next),
      grid=(n_blk,),
      in_specs=in_specs,
      out_specs=out_specs,
      out_shape=out_shape,
      scratch_shapes=[pltpu.VMEM((group, D), F32), pltpu.VMEM((group, D), BF16)] if ahead else [],
      compiler_params=_params(1),
      name="ffn_attn" if has_wo else "ffn_pool",
  )(*args)
  return outs[:n_out], outs[n_out:]


def _rope_tables():
  pos = np.arange(LAT_T)
  row = (pos // GRID_W).astype(np.float32)
  col = (pos % GRID_W).astype(np.float32)
  inv = np.float32(ROPE_BASE) ** (-np.arange(ROPE_PAIRS, dtype=np.float32) / np.float32(ROPE_PAIRS))
  ar = row[:, None] * inv
  ac = col[:, None] * inv
  cos = np.concatenate([np.cos(ar), np.cos(ac), np.cos(ar), np.cos(ac)], axis=1)
  sin = np.concatenate([-np.sin(ar), -np.sin(ac), np.sin(ar), np.sin(ac)], axis=1)
  return jnp.asarray(np.concatenate([cos, sin], axis=1), dtype=F32)


def _paired_qk_columns(w_qkv, qk_gains):
  def paired(a):
    blocks = a.reshape(a.shape[:-1] + (-1, 2, 2, ROPE_PAIRS))
    return jnp.swapaxes(blocks, -3, -2).reshape(a.shape)

  w = jnp.concatenate([paired(w_qkv[..., :Q_W + KV_W]), w_qkv[..., Q_W + KV_W:]], axis=-1)
  return w, paired(qk_gains)


def kernel(x_prompt, x_sample, c, cache_k, cache_v, c_ctx, w_ada, b_ada, norm_gains,
           w_qkv, qk_gains, w_o, w_pool, pool_scale, w_up, w_down):
  xs = [x_prompt.reshape(N_CTX_TOK, D), x_sample.reshape(N_LAT_TOK, D)]
  cond = jnp.concatenate(
      [c_ctx[None, :], c, jnp.zeros((N_COND - 1 - N_LAT_SEQ, D), F32)], axis=0)
  ada = _modulation(cond, w_ada, b_ada)
  rope_cs = _rope_tables()
  w_qkv_b, qk_gains_p = _paired_qk_columns(w_qkv.astype(BF16), qk_gains)
  w_o_b = w_o.astype(BF16)
  w_pool_b = w_pool.astype(BF16)

  new_kv = ffn_w = None
  for layer in range(DEPTH):
    last = layer == DEPTH - 1
    next_w = None if last else (w_up, w_down)
    if layer % 2 == 0:
      q, k, v, *new_kv = _qkv(xs, ada, norm_gains, w_qkv_b, qk_gains_p, rope_cs, new_kv, layer)
      mix, cast_w = _attn_latent(q, k, v, cache_k, cache_v, _attn_context(q, k, v), layer // 2,
                                 (w_up, w_down) if ffn_w is None else None)
      ffn_w = ffn_w or cast_w
      xs, ffn_w = _ffn(xs, mix, ada, norm_gains, w_o_b, *ffn_w, next_w, layer, last)
    else:
      mix = _pool(xs[0], ada, norm_gains, w_pool_b, pool_scale, layer)
      xs, ffn_w = _ffn(xs, mix, ada, norm_gains, None, *ffn_w, next_w, layer, last)

  out_kv = (N_CTX_SEQ, N_ATTN, CTX_T, N_KV, HD)
  return (xs[0].reshape(N_CTX_SEQ, CTX_T, D), xs[1].reshape(N_LAT_SEQ, LAT_T, D),
          new_kv[0].reshape(out_kv), new_kv[1].reshape(out_kv))
```

```python
import functools
import math

import jax
import jax.numpy as jnp
import numpy as np
from jax import lax
from jax.experimental import pallas as pl
from jax.experimental.pallas import tpu as pltpu

F32 = jnp.float32
BF16 = jnp.bfloat16

D = 1024
N_CTX_SEQ = 32
CTX_T = 256
N_LAT_SEQ = 4
LAT_T = 2048
PAST = 512
DEPTH = 4
N_ATTN = 2
GRID_W = 64
HD = 128
N_HEADS = 8
N_KV = 2
GQA = N_HEADS // N_KV
Q_W = N_HEADS * HD
KV_W = N_KV * HD
QKV_W = Q_W + 2 * KV_W
ROPE_PAIRS = HD // 4
ROPE_BASE = 10000.0
POOL_WINDOWS = (2, 4, 8, 16)
N_POOL_G = len(POOL_WINDOWS)
POOL_DG = D // N_POOL_G
D_FF = 2816
N_MOD = 6
EPS = 1e-6

N_CTX_TOK = N_CTX_SEQ * CTX_T
N_LAT_TOK = N_LAT_SEQ * LAT_T
N_TOK = N_CTX_TOK + N_LAT_TOK
N_COND = 8
Q_SCALE = math.log2(math.e) / math.sqrt(HD)

VMEM_LIMIT = 56 * 1024 * 1024

FFN_BM = 512
QKV_BM = 512
POOL_BM = 1024
FFN_GROUPS = 2
POOL_TILE = CTX_T
POOL_HALO = max(POOL_WINDOWS) // 2
ATT_TQ = 1024
ATT_CHUNK = 256
CTX_NB = 8


def _n_ctx_blk(bm):
  return N_CTX_TOK // bm


def _token_specs(split, width, bm):
  n_ctx = _n_ctx_blk(bm)
  if not split:
    return [pl.BlockSpec((bm, width), lambda i: (i, 0))]
  return [pl.BlockSpec((bm, width), lambda i: (jnp.minimum(i, n_ctx - 1), 0)),
          pl.BlockSpec((bm, width), lambda i: (jnp.maximum(i - n_ctx, 0), 0))]


def _token_shapes(split, width, dtype):
  if not split:
    return [jax.ShapeDtypeStruct((N_TOK, width), dtype)]
  return [jax.ShapeDtypeStruct((N_CTX_TOK, width), dtype),
          jax.ShapeDtypeStruct((N_LAT_TOK, width), dtype)]


def _token_load(is_ctx, refs, rows=slice(None)):
  if len(refs) == 1:
    return refs[0][rows, :]
  return jnp.where(is_ctx, refs[0][rows, :], refs[1][rows, :])


def _token_store(is_ctx, refs, value, rows=slice(None)):
  if len(refs) == 1:
    refs[0][rows, :] = value
    return

  @pl.when(is_ctx)
  def _():
    refs[0][rows, :] = value

  @pl.when(jnp.logical_not(is_ctx))
  def _():
    refs[1][rows, :] = value


def _lat_seq_pos(i, bm):
  lat_blk = jnp.maximum(i - _n_ctx_blk(bm), 0)
  return lat_blk // (LAT_T // bm), lat_blk % (LAT_T // bm)


def _cond_row(i, bm):
  return jnp.where(i < _n_ctx_blk(bm), 0, 1 + _lat_seq_pos(i, bm)[0])


def _mod_rows(ada_ref, row):
  return [ada_ref[0, m, pl.ds(row, 1), :] for m in range(N_MOD)]


def _layer_spec(shape, layer):
  zeros = (0,) * len(shape)
  return pl.BlockSpec((1,) + shape, lambda *_: (layer,) + zeros, pipeline_mode=pl.Buffered(1))


def _rms_scale(x):
  return x * lax.rsqrt(jnp.mean(x * x, axis=-1, keepdims=True) + EPS)


def _silu(x):
  return x / (1.0 + jnp.exp(-x))


BF16_SUBLANES = 16


def _ffn_weight_cast_specs(layer, n_steps, step_of):
  up_rows = D // n_steps
  dn_blks = n_steps
  while (D_FF // dn_blks) % BF16_SUBLANES:
    dn_blks //= 2
  dn_rows = D_FF // dn_blks
  up_blk = lambda *g: step_of(*g)
  dn_blk = lambda *g: jnp.minimum(step_of(*g), dn_blks - 1)
  in_specs = [pl.BlockSpec((1, up_rows, 2 * D_FF), lambda *g: (layer, up_blk(*g), 0)),
              pl.BlockSpec((1, dn_rows, D), lambda *g: (layer, dn_blk(*g), 0))]
  out_specs = [pl.BlockSpec((up_rows, 2 * D_FF), lambda *g: (up_blk(*g), 0)),
               pl.BlockSpec((dn_rows, D), lambda *g: (dn_blk(*g), 0))]
  out_shape = [jax.ShapeDtypeStruct((D, 2 * D_FF), BF16),
               jax.ShapeDtypeStruct((D_FF, D), BF16)]
  return in_specs, out_specs, out_shape


def _cast_ffn_weight_slab(src_refs, dst_refs):
  for src, dst in zip(src_refs, dst_refs):
    dst[...] = src[0].astype(BF16)


def _params(n_grid_dims):
  return pltpu.CompilerParams(
      dimension_semantics=("arbitrary",) * n_grid_dims, vmem_limit_bytes=VMEM_LIMIT)


MOD_PER_STEP = 3


def _mod_kernel(cond_ref, w_ref, b_ref, o_ref):
  s = _silu(cond_ref[...]).astype(BF16)
  for m in range(MOD_PER_STEP):
    w = w_ref[0, :, m * D:(m + 1) * D].astype(BF16)
    o_ref[0, m] = jnp.dot(s, w, preferred_element_type=F32) + b_ref[0, m]


def _modulation(cond, w_ada, b_ada):
  return pl.pallas_call(
      _mod_kernel,
      grid=(DEPTH, N_MOD // MOD_PER_STEP),
      in_specs=[
          pl.BlockSpec((N_COND, D), lambda l, m: (0, 0)),
          pl.BlockSpec((1, D, MOD_PER_STEP * D), lambda l, m: (l, 0, m)),
          pl.BlockSpec((1, MOD_PER_STEP, 1, D), lambda l, m: (l, m, 0, 0)),
      ],
      out_specs=pl.BlockSpec((1, MOD_PER_STEP, N_COND, D), lambda l, m: (l, m, 0, 0)),
      out_shape=jax.ShapeDtypeStruct((DEPTH, N_MOD, N_COND, D), F32),
      compiler_params=_params(2),
      name="modulation",
  )(cond, w_ada, b_ada.reshape(DEPTH, N_MOD, 1, D))


def _swap_middle(x):
  blk = lax.broadcasted_iota(jnp.int32, x.shape, 1) // ROPE_PAIRS
  return jnp.where(blk == 1, pltpu.roll(x, HD - ROPE_PAIRS, 1),
                   jnp.where(blk == 2, pltpu.roll(x, ROPE_PAIRS, 1), x))


def _qkv_kernel(*refs, n_x):
  x_refs, refs = refs[:n_x], refs[n_x:]
  ada_ref, g_ref, w_ref, qkg_ref, cs_ref = refs[:5]
  q_ref, k_ref, v_ref, nk_ref, nv_ref = refs[-5:]
  i = pl.program_id(0)
  is_ctx = i < _n_ctx_blk(QKV_BM)
  shift, scale = _mod_rows(ada_ref, _cond_row(i, QKV_BM))[:2]
  gain = g_ref[0, 0:1] * (1.0 + scale)
  gq = qkg_ref[0, 0:1] * Q_SCALE
  gk = qkg_ref[0, 1:2]
  groups = [slice(s * CTX_T, (s + 1) * CTX_T) for s in range(QKV_BM // CTX_T)]

  def body(x_ref, latent):
    h = [(_rms_scale(x_ref[rows, :]) * gain + shift).astype(BF16) for rows in groups]
    qkv = [jnp.dot(h_s, w_ref[0], preferred_element_type=F32) for h_s in h]
    for s, (rows, qkv_s) in enumerate(zip(groups, qkv)):
      if latent:
        cos = cs_ref[rows, :HD]
        sin = cs_ref[rows, HD:]
        rope = lambda x: x * cos + pltpu.roll(x, HD // 2, 1) * sin
      else:
        rope = lambda x: x
      for hd in range(N_HEADS):
        cols = slice(hd * HD, (hd + 1) * HD)
        q_ref[rows, cols] = rope(_rms_scale(qkv_s[:, cols]) * gq).astype(BF16)
      v = qkv_s[:, Q_W + KV_W:]
      v_ref[rows, :] = v.astype(BF16)
      for hd in range(N_KV):
        cols = slice(hd * HD, (hd + 1) * HD)
        k = rope(_rms_scale(qkv_s[:, Q_W + hd * HD:Q_W + (hd + 1) * HD]) * gk)
        k_ref[rows, cols] = k.astype(BF16)
        if not latent:
          nk_ref[s, 0, pl.ds(hd, CTX_T, stride=N_KV), :] = _swap_middle(k)
          nv_ref[s, 0, pl.ds(hd, CTX_T, stride=N_KV), :] = v[:, cols]

  @pl.when(is_ctx)
  def _():
    body(x_refs[0], latent=False)

  @pl.when(jnp.logical_not(is_ctx))
  def _():
    body(x_refs[-1], latent=True)


def _qkv(xs, ada, gains, w_qkv, qk_gains, rope_cs, new_kv, layer):
  a = layer // 2
  bm = QKV_BM
  new_spec = pl.BlockSpec(
      (bm // CTX_T, 1, CTX_T * N_KV, HD),
      lambda i: (jnp.minimum(i, _n_ctx_blk(bm) - 1), a, 0, 0))
  new_shape = jax.ShapeDtypeStruct((N_CTX_SEQ, N_ATTN, CTX_T * N_KV, HD), F32)
  args = list(xs) + [ada, gains, w_qkv, qk_gains, rope_cs]
  in_specs = _token_specs(len(xs) == 2, D, bm) + [
      _layer_spec((N_MOD, N_COND, D), layer),
      _layer_spec((4, D), layer),
      _layer_spec((D, QKV_W), a),
      _layer_spec((2, HD), a),
      pl.BlockSpec((bm, 2 * HD), lambda i: (_lat_seq_pos(i, bm)[1], 0)),
  ]
  aliases = {}
  if new_kv is not None:
    aliases = {len(args): 3, len(args) + 1: 4}
    args += list(new_kv)
    in_specs += [pl.BlockSpec(memory_space=pl.ANY)] * 2
  return pl.pallas_call(
      functools.partial(_qkv_kernel, n_x=len(xs)),
      grid=(N_TOK // bm,),
      in_specs=in_specs,
      out_specs=(_token_specs(False, Q_W, bm) + _token_specs(False, KV_W, bm) * 2
                 + [new_spec] * 2),
      out_shape=(_token_shapes(False, Q_W, BF16) + _token_shapes(False, KV_W, BF16) * 2
                 + [new_shape] * 2),
      input_output_aliases=aliases,
      compiler_params=_params(1),
      name="qkv",
  )(*args)


def _nt_dot(a, b):
  return lax.dot_general(a, b, (((1,), (1,)), ((), ())), preferred_element_type=F32)


def _with_ones(v):
  return jnp.concatenate([v, jnp.ones_like(v)], axis=1)


def _softmax_pv(s, v1):
  p = jnp.exp2(s - jnp.max(s, axis=-1, keepdims=True)).astype(BF16)
  oa = jnp.dot(p, v1, preferred_element_type=F32)
  return oa[:, :HD] / oa[:, HD:]


def _attn_ctx_kernel(q_ref, k_ref, v_ref, o_ref):
  for b in range(CTX_NB):
    rows = slice(b * CTX_T, (b + 1) * CTX_T)
    for h in range(N_KV):
      kv_cols = slice(h * HD, (h + 1) * HD)
      v1 = _with_ones(v_ref[rows, kv_cols])
      for g in range(GQA):
        cols = slice((h * GQA + g) * HD, (h * GQA + g + 1) * HD)
        s = _nt_dot(q_ref[rows, cols], k_ref[rows, kv_cols])
        o_ref[rows, cols] = _softmax_pv(s, v1).astype(BF16)


def _attn_context(q, k, v):
  bm = CTX_NB * CTX_T
  row = lambda i: (i, 0)
  return pl.pallas_call(
      _attn_ctx_kernel,
      grid=(N_CTX_TOK // bm,),
      in_specs=[
          pl.BlockSpec((bm, Q_W), row),
          pl.BlockSpec((bm, KV_W), row),
          pl.BlockSpec((bm, KV_W), row),
      ],
      out_specs=pl.BlockSpec((bm, Q_W), row),
      out_shape=jax.ShapeDtypeStruct((N_TOK, Q_W), BF16),
      compiler_params=_params(1),
      name="attn_context",
  )(q, k, v)


def _attn_lat_kernel(*refs, cast_weights):
  q_ref, k_ref, v_ref, ck_ref, cv_ref, _ = refs[:6]
  kcat, vcat = refs[-2:]
  if cast_weights:
    _cast_ffn_weight_slab(refs[6:8], refs[-4:-2])
    o_ref = refs[8]
  else:
    o_ref = refs[6]

  @pl.when(pl.program_id(2) == 0)
  def _():
    head_rows = pl.ds(pl.program_id(1), PAST, stride=N_KV)
    kcat[:LAT_T] = k_ref[...]
    kcat[LAT_T:] = _swap_middle(ck_ref[0, 0, head_rows, :]).astype(BF16)
    vcat[:LAT_T] = _with_ones(v_ref[...])
    vcat[LAT_T:] = _with_ones(cv_ref[0, 0, head_rows, :].astype(BF16))

  chains = [(g, r, ATT_CHUNK) for g in range(GQA) for r in range(0, ATT_TQ, ATT_CHUNK)]
  halves = lambda g, r, n: [(g, r, n // 2), (g, r + n // 2, n // 2)]
  chains = halves(*chains[0]) + chains[1:-1] + halves(*chains[-1])
  for g, r, n in chains:
    rows, cols = slice(r, r + n), slice(g * HD, (g + 1) * HD)
    s = _nt_dot(q_ref[rows, cols], kcat[...])
    o_ref[rows, cols] = _softmax_pv(s, vcat[...]).astype(BF16)


def _attn_latent(q, k, v, cache_k, cache_v, mix, a, cast_w):
  nq = LAT_T // ATT_TQ
  q0 = N_CTX_TOK // ATT_TQ
  kv0 = N_CTX_TOK // LAT_T
  cache_rows = (N_LAT_SEQ, N_ATTN, PAST * N_KV, HD)
  cache_spec = pl.BlockSpec((1, 1, PAST * N_KV, HD), lambda b, h, j: (b, a, 0, 0))
  in_specs = [
      pl.BlockSpec((ATT_TQ, GQA * HD), lambda b, h, j: (q0 + b * nq + j, h)),
      pl.BlockSpec((LAT_T, HD), lambda b, h, j: (kv0 + b, h)),
      pl.BlockSpec((LAT_T, HD), lambda b, h, j: (kv0 + b, h)),
      cache_spec,
      cache_spec,
      pl.BlockSpec(memory_space=pl.ANY),
  ]
  args = [q, k, v, cache_k.reshape(cache_rows), cache_v.reshape(cache_rows), mix]
  out_specs = [pl.BlockSpec((ATT_TQ, GQA * HD), lambda b, h, j: (q0 + b * nq + j, h))]
  out_shape = [jax.ShapeDtypeStruct((N_TOK, Q_W), BF16)]
  if cast_w is not None:
    cast_in, cast_out, cast_shape = _ffn_weight_cast_specs(
        0, N_LAT_SEQ * N_KV * nq, lambda b, h, j: (b * N_KV + h) * nq + j)
    in_specs += cast_in
    args += list(cast_w)
    out_specs += cast_out
    out_shape += cast_shape
  outs = pl.pallas_call(
      functools.partial(_attn_lat_kernel, cast_weights=cast_w is not None),
      grid=(N_LAT_SEQ, N_KV, nq),
      in_specs=in_specs,
      out_specs=out_specs,
      out_shape=out_shape,
      scratch_shapes=[pltpu.VMEM((LAT_T + PAST, HD), BF16),
                      pltpu.VMEM((LAT_T + PAST, 2 * HD), BF16)],
      input_output_aliases={5: 0},
      compiler_params=_params(3),
      name="attn_latent",
  )(*args)
  return outs[0], outs[1:]


def _edge_window_sums(slab, w):
  rows = slab.shape[0]
  back, fwd, m = slab, slab, 1
  while m < w // 2:
    back = back + pltpu.roll(back, m, 0)
    fwd = fwd + pltpu.roll(fwd, rows - m, 0)
    m *= 2
  return (pltpu.roll(back, 1, 0) + fwd)[POOL_HALO:2 * POOL_HALO]


def _pool_bands():
  t = np.arange(POOL_TILE)[:, None]
  j = np.arange(POOL_TILE)[None, :]
  bands = [((j >= t - w // 2) & (j < t + w - w // 2)) / w for w in POOL_WINDOWS]
  return jnp.asarray(np.stack(bands), dtype=BF16)


def _pool_kernel(x_ref, xprev_ref, xnext_ref, ada_ref, g_ref, band_ref, w_ref, ps_ref, o_ref):
  i = pl.program_id(0)
  is_ctx = i < _n_ctx_blk(POOL_BM)
  shift, scale = _mod_rows(ada_ref, _cond_row(i, POOL_BM))[:2]
  gain = g_ref[0, 0:1] * (1.0 + scale)
  pre = lambda x: _rms_scale(x) * gain + shift
  h = pre(x_ref[...])
  seq_pos = _lat_seq_pos(i, POOL_BM)[1]
  is_first = jnp.logical_or(is_ctx, seq_pos == 0)
  is_last = jnp.logical_or(is_ctx, seq_pos == LAT_T // POOL_BM - 1)
  ctx_edge = jnp.where(is_ctx, 1.0, 0.0)
  first_edge = jnp.where(is_first, 1.0, 0.0)
  last_edge = jnp.where(is_last, 1.0, 0.0)
  h_prev = jnp.where(is_first, 0.0, pre(xprev_ref[...]))
  h_next = jnp.where(is_last, 0.0, pre(xnext_ref[...]))
  h_hi = h.astype(BF16)
  h_lo = (h - h_hi.astype(F32)).astype(BF16)
  n_tiles = POOL_BM // POOL_TILE
  row = lax.broadcasted_iota(jnp.int32, (POOL_HALO, POOL_DG), 0)
  units = [(g, w, j) for g, w in enumerate(POOL_WINDOWS) for j in range(n_tiles)]
  col_of = lambda g: slice(g * POOL_DG, (g + 1) * POOL_DG)
  row_of = lambda j: slice(j * POOL_TILE, (j + 1) * POOL_TILE)

  parts = [jnp.dot(band_ref[g],
                   jnp.concatenate([h_hi[row_of(j), col_of(g)], h_lo[row_of(j), col_of(g)]], axis=1),
                   preferred_element_type=F32) for g, w, j in units]

  pooled = []
  for (g, w, j), part in zip(units, parts):
    r0, cols = j * POOL_TILE, col_of(g)
    hg = h[row_of(j), cols]
    inner = part[:, :POOL_DG] + part[:, POOL_DG:]
    before = (h_prev[:, cols] if j == 0
              else jnp.where(is_ctx, 0.0, h[r0 - POOL_HALO:r0, cols]))
    after = (h_next[:, cols] if j == n_tiles - 1
             else jnp.where(is_ctx, 0.0, h[r0 + POOL_TILE:r0 + POOL_TILE + POOL_HALO, cols]))
    top = _edge_window_sums(jnp.concatenate([before, hg[:2 * POOL_HALO]], axis=0), w)
    bot = _edge_window_sums(jnp.concatenate([hg[-2 * POOL_HALO:], after], axis=0), w)
    starts = first_edge if j == 0 else ctx_edge
    ends = last_edge if j == n_tiles - 1 else ctx_edge
    top_cnt = w - jnp.maximum(w // 2 - row, 0).astype(F32) * starts
    bot_cnt = w - jnp.maximum(row + (w - w // 2) - POOL_HALO, 0).astype(F32) * ends
    mean = jnp.concatenate(
        [top / top_cnt, inner[POOL_HALO:-POOL_HALO], bot / bot_cnt], axis=0)
    pooled.append((mean - hg).astype(BF16))

  for (g, w, j), p in zip(units, pooled):
    og = jnp.dot(p, w_ref[0, g], preferred_element_type=F32)
    o_ref[row_of(j), col_of(g)] = (og * ps_ref[0, :, col_of(g)]).astype(BF16)


def _pool(x, ada, gains, w_pool, pool_scale, layer):
  p = layer // 2
  bm = POOL_BM
  halo_per_blk = bm // POOL_HALO
  n_halo = N_TOK // POOL_HALO
  prev_spec = pl.BlockSpec((POOL_HALO, D), lambda i: (jnp.maximum(i * halo_per_blk - 1, 0), 0))
  next_spec = pl.BlockSpec(
      (POOL_HALO, D), lambda i: (jnp.minimum((i + 1) * halo_per_blk, n_halo - 1), 0))
  return pl.pallas_call(
      _pool_kernel,
      grid=(N_TOK // bm,),
      in_specs=_token_specs(False, D, bm) + [
          prev_spec, next_spec,
          _layer_spec((N_MOD, N_COND, D), layer),
          _layer_spec((4, D), layer),
          pl.BlockSpec((N_POOL_G, POOL_TILE, POOL_TILE), lambda i: (0, 0, 0),
                       pipeline_mode=pl.Buffered(1)),
          _layer_spec((N_POOL_G, POOL_DG, POOL_DG), p),
          _layer_spec((1, D), p),
      ],
      out_specs=_token_specs(False, D, bm)[0],
      out_shape=_token_shapes(False, D, BF16)[0],
      compiler_params=_params(1),
      name="pool",
  )(x, x, x, ada, gains, _pool_bands(), w_pool, pool_scale.reshape(-1, 1, D))


def _ffn_kernel(*refs, n_x, n_out, has_wo, cast_next):
  ahead = not has_wo
  x_refs, refs = refs[:n_x], refs[n_x:]
  if ahead:
    xn_refs, refs = refs[:n_x], refs[n_x:]
    (x1_scr, h_scr), refs = refs[-2:], refs[:-2]
  if cast_next:
    cast_dst, refs = refs[-2:], refs[:-2]
  o_refs, refs = refs[-n_out:], refs[:-n_out]
  if cast_next:
    cast_src, refs = refs[-2:], refs[:-2]
  if has_wo:
    mix_ref, ada_ref, g_ref, wo_ref, wup_ref, wdn_ref = refs
  else:
    mix_ref, mixn_ref, ada_ref, g_ref, wup_ref, wdn_ref = refs
  i = pl.program_id(0)
  nxt = jnp.minimum(i + 1, pl.num_programs(0) - 1)
  n_ctx = _n_ctx_blk(FFN_BM)
  mod = _mod_rows(ada_ref, _cond_row(i, FFN_BM))
  post_gain = mod[5] * g_ref[0, 3:4]
  if cast_next:
    _cast_ffn_weight_slab(cast_src, cast_dst)

  def pre_ffn(x, mix, mod):
    _, _, gate_m, shift_f, scale_f, _ = mod
    if has_wo:
      a = jnp.dot(mix, wo_ref[0], preferred_element_type=F32)
    else:
      a = mix.astype(F32)
    x1 = x + _rms_scale(a) * (gate_m * g_ref[0, 1:2])
    return x1, (_rms_scale(x1) * (g_ref[0, 2:3] * (1.0 + scale_f)) + shift_f).astype(BF16)

  group = FFN_BM // FFN_GROUPS
  groups = [slice(r * group, (r + 1) * group) for r in range(FFN_GROUPS)]
  pre_group = lambda rows: pre_ffn(_token_load(i < n_ctx, x_refs, rows), mix_ref[rows, :], mod)
  if ahead:
    @pl.when(i == 0)
    def _():
      x1_scr[...], h_scr[...] = pre_group(groups[0])

    x1_h = [(x1_scr[...], h_scr[...])] + [pre_group(rows) for rows in groups[1:]]
  else:
    x1_h = [pre_group(rows) for rows in groups]
  x1, h = zip(*x1_h)
  gu, y = [], []
  for r in range(FFN_GROUPS + 1):
    if r < FFN_GROUPS:
      gu.append(jnp.dot(h[r], wup_ref[...], preferred_element_type=F32))
    if ahead and r == FFN_GROUPS - 1:
      x1_h_nxt = pre_ffn(_token_load(nxt < n_ctx, xn_refs), mixn_ref[...],
                         _mod_rows(ada_ref, _cond_row(nxt, FFN_BM)))
    if r > 0:
      act = (_silu(gu[r - 1][:, :D_FF]) * gu[r - 1][:, D_FF:]).astype(BF16)
      y.append(jnp.dot(act, wdn_ref[...], preferred_element_type=F32))
  for rows, x1_r, y_r in zip(groups, x1, y):
    _token_store(i < n_ctx, o_refs, x1_r + _rms_scale(y_r) * post_gain, rows)
  if ahead:
    x1_scr[...], h_scr[...] = x1_h_nxt


def _next_group_specs(split, width, bm, group):
  n_blk, n_ctx, per_blk = N_TOK // bm, _n_ctx_blk(bm), bm // group
  nxt = lambda i: jnp.minimum(i + 1, n_blk - 1)
  if not split:
    return [pl.BlockSpec((group, width), lambda i: (nxt(i) * per_blk, 0))]
  return [
      pl.BlockSpec((group, width), lambda i: (jnp.minimum(nxt(i), n_ctx - 1) * per_blk, 0)),
      pl.BlockSpec((group, width), lambda i: (jnp.maximum(nxt(i) - n_ctx, 0) * per_blk, 0))]


def _ffn(xs, mix, ada, gains, w_o, w_up, w_down, next_w, layer, split_out):
  has_wo = w_o is not None
  cast_next = next_w is not None
  resident = lambda shape: pl.BlockSpec(shape, lambda i: (0, 0), pipeline_mode=pl.Buffered(1))
  bm = FFN_BM
  n_blk = N_TOK // bm
  group = bm // FFN_GROUPS
  split_in = len(xs) == 2
  ahead = not has_wo
  nxt_specs = lambda split: _next_group_specs(split, D, bm, group) if ahead else []
  in_specs = (_token_specs(split_in, D, bm) + nxt_specs(split_in)
              + _token_specs(False, D, bm) + nxt_specs(False) + [
                  _layer_spec((N_MOD, N_COND, D), layer),
                  _layer_spec((4, D), layer),
              ])
  args = list(xs) * (2 if ahead else 1) + [mix] * (2 if ahead else 1) + [ada, gains]
  if has_wo:
    in_specs.append(_layer_spec((Q_W, D), layer // 2))
    args.append(w_o)
  in_specs += [resident((D, 2 * D_FF)), resident((D_FF, D))]
  args += [w_up, w_down]
  n_out = 2 if split_out else 1
  out_specs = _token_specs(split_out, D, bm)
  out_shape = _token_shapes(split_out, D, F32)
  if cast_next:
    cast_in, cast_out, cast_shape = _ffn_weight_cast_specs(layer + 1, n_blk, lambda i: i)
    in_specs += cast_in
    args += list(next_w)
    out_specs += cast_out
    out_shape += cast_shape
  outs = pl.pallas_call(
      functools.partial(_ffn_kernel, n_x=len(xs), n_out=n_out, has_wo=has_wo,
                        cast_next=cast_next),
      grid=(n_blk,),
      in_specs=in_specs,
      out_specs=out_specs,
      out_shape=out_shape,
      scratch_shapes=[pltpu.VMEM((group, D), F32), pltpu.VMEM((group, D), BF16)] if ahead else [],
      compiler_params=_params(1),
      name="ffn_attn" if has_wo else "ffn_pool",
  )(*args)
  return outs[:n_out], outs[n_out:]


def _rope_tables():
  pos = np.arange(LAT_T)
  row = (pos // GRID_W).astype(np.float32)
  col = (pos % GRID_W).astype(np.float32)
  inv = np.float32(ROPE_BASE) ** (-np.arange(ROPE_PAIRS, dtype=np.float32) / np.float32(ROPE_PAIRS))
  ar = row[:, None] * inv
  ac = col[:, None] * inv
  cos = np.concatenate([np.cos(ar), np.cos(ac), np.cos(ar), np.cos(ac)], axis=1)
  sin = np.concatenate([-np.sin(ar), -np.sin(ac), np.sin(ar), np.sin(ac)], axis=1)
  return jnp.asarray(np.concatenate([cos, sin], axis=1), dtype=F32)


def _paired_qk_columns(w_qkv, qk_gains):
  def paired(a):
    blocks = a.reshape(a.shape[:-1] + (-1, 2, 2, ROPE_PAIRS))
    return jnp.swapaxes(blocks, -3, -2).reshape(a.shape)

  w = jnp.concatenate([paired(w_qkv[..., :Q_W + KV_W]), w_qkv[..., Q_W + KV_W:]], axis=-1)
  return w, paired(qk_gains)


def kernel(x_prompt, x_sample, c, cache_k, cache_v, c_ctx, w_ada, b_ada, norm_gains,
           w_qkv, qk_gains, w_o, w_pool, pool_scale, w_up, w_down):
  xs = [x_prompt.reshape(N_CTX_TOK, D), x_sample.reshape(N_LAT_TOK, D)]
  cond = jnp.concatenate(
      [c_ctx[None, :], c, jnp.zeros((N_COND - 1 - N_LAT_SEQ, D), F32)], axis=0)
  ada = _modulation(cond, w_ada, b_ada)
  rope_cs = _rope_tables()
  w_qkv_b, qk_gains_p = _paired_qk_columns(w_qkv.astype(BF16), qk_gains)
  w_o_b = w_o.astype(BF16)
  w_pool_b = w_pool.astype(BF16)

  new_kv = ffn_w = None
  for layer in range(DEPTH):
    last = layer == DEPTH - 1
    next_w = None if last else (w_up, w_down)
    if layer % 2 == 0:
      q, k, v, *new_kv = _qkv(xs, ada, norm_gains, w_qkv_b, qk_gains_p, rope_cs, new_kv, layer)
      mix, cast_w = _attn_latent(q, k, v, cache_k, cache_v, _attn_context(q, k, v), layer // 2,
                                 (w_up, w_down) if ffn_w is None else None)
      ffn_w = ffn_w or cast_w
      xs, ffn_w = _ffn(xs, mix, ada, norm_gains, w_o_b, *ffn_w, next_w, layer, last)
    else:
      mix = _pool(xs[0], ada, norm_gains, w_pool_b, pool_scale, layer)
      xs, ffn_w = _ffn(xs, mix, ada, norm_gains, None, *ffn_w, next_w, layer, last)

  out_kv = (N_CTX_SEQ, N_ATTN, CTX_T, N_KV, HD)
  return (xs[0].reshape(N_CTX_SEQ, CTX_T, D), xs[1].reshape(N_LAT_SEQ, LAT_T, D),
          new_kv[0].reshape(out_kv), new_kv[1].reshape(out_kv))
```

```python
import functools
import math

import jax
import jax.numpy as jnp
import numpy as np
from jax import lax
from jax.experimental import pallas as pl
from jax.experimental.pallas import tpu as pltpu

F32 = jnp.float32
BF16 = jnp.bfloat16

D = 1024
N_CTX_SEQ = 32
CTX_T = 256
N_LAT_SEQ = 4
LAT_T = 2048
PAST = 512
DEPTH = 4
N_ATTN = 2
GRID_W = 64
HD = 128
N_HEADS = 8
N_KV = 2
GQA = N_HEADS // N_KV
Q_W = N_HEADS * HD
KV_W = N_KV * HD
QKV_W = Q_W + 2 * KV_W
ROPE_PAIRS = HD // 4
ROPE_BASE = 10000.0
POOL_WINDOWS = (2, 4, 8, 16)
N_POOL_G = len(POOL_WINDOWS)
POOL_DG = D // N_POOL_G
D_FF = 2816
N_MOD = 6
EPS = 1e-6

N_CTX_TOK = N_CTX_SEQ * CTX_T
N_LAT_TOK = N_LAT_SEQ * LAT_T
N_TOK = N_CTX_TOK + N_LAT_TOK
N_COND = 8
Q_SCALE = math.log2(math.e) / math.sqrt(HD)

VMEM_LIMIT = 56 * 1024 * 1024

FFN_BM = 512
QKV_BM = 512
POOL_BM = 1024
FFN_GROUPS = 2
POOL_TILE = CTX_T
POOL_HALO = max(POOL_WINDOWS) // 2
ATT_TQ = 1024
ATT_CHUNK = 256
CTX_NB = 8


def _n_ctx_blk(bm):
  return N_CTX_TOK // bm


def _token_specs(split, width, bm):
  n_ctx = _n_ctx_blk(bm)
  if not split:
    return [pl.BlockSpec((bm, width), lambda i: (i, 0))]
  return [pl.BlockSpec((bm, width), lambda i: (jnp.minimum(i, n_ctx - 1), 0)),
          pl.BlockSpec((bm, width), lambda i: (jnp.maximum(i - n_ctx, 0), 0))]


def _token_shapes(split, width, dtype):
  if not split:
    return [jax.ShapeDtypeStruct((N_TOK, width), dtype)]
  return [jax.ShapeDtypeStruct((N_CTX_TOK, width), dtype),
          jax.ShapeDtypeStruct((N_LAT_TOK, width), dtype)]


def _token_load(is_ctx, refs, rows=slice(None)):
  if len(refs) == 1:
    return refs[0][rows, :]
  return jnp.where(is_ctx, refs[0][rows, :], refs[1][rows, :])


def _token_store(is_ctx, refs, value, rows=slice(None)):
  if len(refs) == 1:
    refs[0][rows, :] = value
    return

  @pl.when(is_ctx)
  def _():
    refs[0][rows, :] = value

  @pl.when(jnp.logical_not(is_ctx))
  def _():
    refs[1][rows, :] = value


def _lat_seq_pos(i, bm):
  lat_blk = jnp.maximum(i - _n_ctx_blk(bm), 0)
  return lat_blk // (LAT_T // bm), lat_blk % (LAT_T // bm)


def _cond_row(i, bm):
  return jnp.where(i < _n_ctx_blk(bm), 0, 1 + _lat_seq_pos(i, bm)[0])


def _mod_rows(ada_ref, row):
  return [ada_ref[0, m, pl.ds(row, 1), :] for m in range(N_MOD)]


def _layer_spec(shape, layer):
  zeros = (0,) * len(shape)
  return pl.BlockSpec((1,) + shape, lambda *_: (layer,) + zeros, pipeline_mode=pl.Buffered(1))


def _rms_scale(x):
  return x * lax.rsqrt(jnp.mean(x * x, axis=-1, keepdims=True) + EPS)


def _silu(x):
  return x / (1.0 + jnp.exp(-x))


BF16_SUBLANES = 16


def _ffn_weight_cast_specs(layer, n_steps, step_of):
  up_rows = D // n_steps
  dn_blks = n_steps
  while (D_FF // dn_blks) % BF16_SUBLANES:
    dn_blks //= 2
  dn_rows = D_FF // dn_blks
  up_blk = lambda *g: step_of(*g)
  dn_blk = lambda *g: jnp.minimum(step_of(*g), dn_blks - 1)
  in_specs = [pl.BlockSpec((1, up_rows, 2 * D_FF), lambda *g: (layer, up_blk(*g), 0)),
              pl.BlockSpec((1, dn_rows, D), lambda *g: (layer, dn_blk(*g), 0))]
  out_specs = [pl.BlockSpec((up_rows, 2 * D_FF), lambda *g: (up_blk(*g), 0)),
               pl.BlockSpec((dn_rows, D), lambda *g: (dn_blk(*g), 0))]
  out_shape = [jax.ShapeDtypeStruct((D, 2 * D_FF), BF16),
               jax.ShapeDtypeStruct((D_FF, D), BF16)]
  return in_specs, out_specs, out_shape


def _cast_ffn_weight_slab(src_refs, dst_refs):
  for src, dst in zip(src_refs, dst_refs):
    dst[...] = src[0].astype(BF16)


def _params(n_grid_dims):
  return pltpu.CompilerParams(
      dimension_semantics=("arbitrary",) * n_grid_dims, vmem_limit_bytes=VMEM_LIMIT)


MOD_PER_STEP = 3


def _mod_kernel(cond_ref, w_ref, b_ref, o_ref):
  s = _silu(cond_ref[...]).astype(BF16)
  for m in range(MOD_PER_STEP):
    w = w_ref[0, :, m * D:(m + 1) * D].astype(BF16)
    o_ref[0, m] = jnp.dot(s, w, preferred_element_type=F32) + b_ref[0, m]


def _modulation(cond, w_ada, b_ada):
  return pl.pallas_call(
      _mod_kernel,
      grid=(DEPTH, N_MOD // MOD_PER_STEP),
      in_specs=[
          pl.BlockSpec((N_COND, D), lambda l, m: (0, 0)),
          pl.BlockSpec((1, D, MOD_PER_STEP * D), lambda l, m: (l, 0, m)),
          pl.BlockSpec((1, MOD_PER_STEP, 1, D), lambda l, m: (l, m, 0, 0)),
      ],
      out_specs=pl.BlockSpec((1, MOD_PER_STEP, N_COND, D), lambda l, m: (l, m, 0, 0)),
      out_shape=jax.ShapeDtypeStruct((DEPTH, N_MOD, N_COND, D), F32),
      compiler_params=_params(2),
      name="modulation",
  )(cond, w_ada, b_ada.reshape(DEPTH, N_MOD, 1, D))


def _swap_middle(x):
  blk = lax.broadcasted_iota(jnp.int32, x.shape, 1) // ROPE_PAIRS
  return jnp.where(blk == 1, pltpu.roll(x, HD - ROPE_PAIRS, 1),
                   jnp.where(blk == 2, pltpu.roll(x, ROPE_PAIRS, 1), x))


def _qkv_kernel(*refs, n_x):
  x_refs, refs = refs[:n_x], refs[n_x:]
  ada_ref, g_ref, w_ref, qkg_ref, cs_ref = refs[:5]
  q_ref, k_ref, v_ref, nk_ref, nv_ref, w_scr = refs[-6:]
  i = pl.program_id(0)
  is_ctx = i < _n_ctx_blk(QKV_BM)
  shift, scale = _mod_rows(ada_ref, _cond_row(i, QKV_BM))[:2]
  gain = g_ref[0, 0:1] * (1.0 + scale)
  qk_gains = _swap_middle(qkg_ref[0])
  gq = qk_gains[0:1] * Q_SCALE
  gk = qk_gains[1:2]

  @pl.when(i == 0)
  def _():
    for hd in range(N_HEADS + N_KV):
      cols = slice(hd * HD, (hd + 1) * HD)
      w_scr[:, cols] = _swap_middle(w_ref[0, :, cols]).astype(BF16)
    w_scr[:, Q_W + KV_W:] = w_ref[0, :, Q_W + KV_W:].astype(BF16)
  groups = [slice(s * CTX_T, (s + 1) * CTX_T) for s in range(QKV_BM // CTX_T)]

  def body(x_ref, latent):
    h = [(_rms_scale(x_ref[rows, :]) * gain + shift).astype(BF16) for rows in groups]
    qkv = [jnp.dot(h_s, w_scr[...], preferred_element_type=F32) for h_s in h]
    for s, (rows, qkv_s) in enumerate(zip(groups, qkv)):
      if latent:
        cos = cs_ref[rows, :HD]
        sin = cs_ref[rows, HD:]
        rope = lambda x: x * cos + pltpu.roll(x, HD // 2, 1) * sin
      else:
        rope = lambda x: x
      for hd in range(N_HEADS):
        cols = slice(hd * HD, (hd + 1) * HD)
        q_ref[rows, cols] = rope(_rms_scale(qkv_s[:, cols]) * gq).astype(BF16)
      v = qkv_s[:, Q_W + KV_W:]
      v_ref[rows, :] = v.astype(BF16)
      for hd in range(N_KV):
        cols = slice(hd * HD, (hd + 1) * HD)
        k = rope(_rms_scale(qkv_s[:, Q_W + hd * HD:Q_W + (hd + 1) * HD]) * gk)
        k_ref[rows, cols] = k.astype(BF16)
        if not latent:
          nk_ref[s, 0, pl.ds(hd, CTX_T, stride=N_KV), :] = _swap_middle(k)
          nv_ref[s, 0, pl.ds(hd, CTX_T, stride=N_KV), :] = v[:, cols]

  @pl.when(is_ctx)
  def _():
    body(x_refs[0], latent=False)

  @pl.when(jnp.logical_not(is_ctx))
  def _():
    body(x_refs[-1], latent=True)


def _qkv(xs, ada, gains, w_qkv, qk_gains, rope_cs, new_kv, layer):
  a = layer // 2
  bm = QKV_BM
  new_spec = pl.BlockSpec(
      (bm // CTX_T, 1, CTX_T * N_KV, HD),
      lambda i: (jnp.minimum(i, _n_ctx_blk(bm) - 1), a, 0, 0))
  new_shape = jax.ShapeDtypeStruct((N_CTX_SEQ, N_ATTN, CTX_T * N_KV, HD), F32)
  args = list(xs) + [ada, gains, w_qkv, qk_gains, rope_cs]
  in_specs = _token_specs(len(xs) == 2, D, bm) + [
      _layer_spec((N_MOD, N_COND, D), layer),
      _layer_spec((4, D), layer),
      _layer_spec((D, QKV_W), a),
      _layer_spec((2, HD), a),
      pl.BlockSpec((bm, 2 * HD), lambda i: (_lat_seq_pos(i, bm)[1], 0)),
  ]
  aliases = {}
  if new_kv is not None:
    aliases = {len(args): 3, len(args) + 1: 4}
    args += list(new_kv)
    in_specs += [pl.BlockSpec(memory_space=pl.ANY)] * 2
  return pl.pallas_call(
      functools.partial(_qkv_kernel, n_x=len(xs)),
      grid=(N_TOK // bm,),
      in_specs=in_specs,
      out_specs=(_token_specs(False, Q_W, bm) + _token_specs(False, KV_W, bm) * 2
                 + [new_spec] * 2),
      out_shape=(_token_shapes(False, Q_W, BF16) + _token_shapes(False, KV_W, BF16) * 2
                 + [new_shape] * 2),
      scratch_shapes=[pltpu.VMEM((D, QKV_W), BF16)],
      input_output_aliases=aliases,
      compiler_params=_params(1),
      name="qkv",
  )(*args)


def _nt_dot(a, b):
  return lax.dot_general(a, b, (((1,), (1,)), ((), ())), preferred_element_type=F32)


def _with_ones(v):
  return jnp.concatenate([v, jnp.ones_like(v)], axis=1)


def _softmax_pv(s, v1):
  p = jnp.exp2(s - jnp.max(s, axis=-1, keepdims=True)).astype(BF16)
  oa = jnp.dot(p, v1, preferred_element_type=F32)
  return oa[:, :HD] / oa[:, HD:]


def _attn_ctx_kernel(q_ref, k_ref, v_ref, o_ref):
  for b in range(CTX_NB):
    rows = slice(b * CTX_T, (b + 1) * CTX_T)
    for h in range(N_KV):
      kv_cols = slice(h * HD, (h + 1) * HD)
      v1 = _with_ones(v_ref[rows, kv_cols])
      for g in range(GQA):
        cols = slice((h * GQA + g) * HD, (h * GQA + g + 1) * HD)
        s = _nt_dot(q_ref[rows, cols], k_ref[rows, kv_cols])
        o_ref[rows, cols] = _softmax_pv(s, v1).astype(BF16)


def _attn_context(q, k, v):
  bm = CTX_NB * CTX_T
  row = lambda i: (i, 0)
  return pl.pallas_call(
      _attn_ctx_kernel,
      grid=(N_CTX_TOK // bm,),
      in_specs=[
          pl.BlockSpec((bm, Q_W), row),
          pl.BlockSpec((bm, KV_W), row),
          pl.BlockSpec((bm, KV_W), row),
      ],
      out_specs=pl.BlockSpec((bm, Q_W), row),
      out_shape=jax.ShapeDtypeStruct((N_TOK, Q_W), BF16),
      compiler_params=_params(1),
      name="attn_context",
  )(q, k, v)


def _attn_lat_kernel(*refs, cast_weights):
  q_ref, k_ref, v_ref, ck_ref, cv_ref, _ = refs[:6]
  kcat, vcat = refs[-2:]
  if cast_weights:
    _cast_ffn_weight_slab(refs[6:8], refs[-4:-2])
    o_ref = refs[8]
  else:
    o_ref = refs[6]

  @pl.when(pl.program_id(2) == 0)
  def _():
    head_rows = pl.ds(pl.program_id(1), PAST, stride=N_KV)
    kcat[:LAT_T] = k_ref[...]
    kcat[LAT_T:] = _swap_middle(ck_ref[0, 0, head_rows, :]).astype(BF16)
    vcat[:LAT_T] = _with_ones(v_ref[...])
    vcat[LAT_T:] = _with_ones(cv_ref[0, 0, head_rows, :].astype(BF16))

  chains = [(g, r, ATT_CHUNK) for g in range(GQA) for r in range(0, ATT_TQ, ATT_CHUNK)]
  halves = lambda g, r, n: [(g, r, n // 2), (g, r + n // 2, n // 2)]
  chains = halves(*chains[0]) + chains[1:-1] + halves(*chains[-1])
  for g, r, n in chains:
    rows, cols = slice(r, r + n), slice(g * HD, (g + 1) * HD)
    s = _nt_dot(q_ref[rows, cols], kcat[...])
    o_ref[rows, cols] = _softmax_pv(s, vcat[...]).astype(BF16)


def _attn_latent(q, k, v, cache_k, cache_v, mix, a, cast_w):
  nq = LAT_T // ATT_TQ
  q0 = N_CTX_TOK // ATT_TQ
  kv0 = N_CTX_TOK // LAT_T
  cache_rows = (N_LAT_SEQ, N_ATTN, PAST * N_KV, HD)
  cache_spec = pl.BlockSpec((1, 1, PAST * N_KV, HD), lambda b, h, j: (b, a, 0, 0))
  in_specs = [
      pl.BlockSpec((ATT_TQ, GQA * HD), lambda b, h, j: (q0 + b * nq + j, h)),
      pl.BlockSpec((LAT_T, HD), lambda b, h, j: (kv0 + b, h)),
      pl.BlockSpec((LAT_T, HD), lambda b, h, j: (kv0 + b, h)),
      cache_spec,
      cache_spec,
      pl.BlockSpec(memory_space=pl.ANY),
  ]
  args = [q, k, v, cache_k.reshape(cache_rows), cache_v.reshape(cache_rows), mix]
  out_specs = [pl.BlockSpec((ATT_TQ, GQA * HD), lambda b, h, j: (q0 + b * nq + j, h))]
  out_shape = [jax.ShapeDtypeStruct((N_TOK, Q_W), BF16)]
  if cast_w is not None:
    cast_in, cast_out, cast_shape = _ffn_weight_cast_specs(
        0, N_LAT_SEQ * N_KV * nq, lambda b, h, j: (b * N_KV + h) * nq + j)
    in_specs += cast_in
    args += list(cast_w)
    out_specs += cast_out
    out_shape += cast_shape
  outs = pl.pallas_call(
      functools.partial(_attn_lat_kernel, cast_weights=cast_w is not None),
      grid=(N_LAT_SEQ, N_KV, nq),
      in_specs=in_specs,
      out_specs=out_specs,
      out_shape=out_shape,
      scratch_shapes=[pltpu.VMEM((LAT_T + PAST, HD), BF16),
                      pltpu.VMEM((LAT_T + PAST, 2 * HD), BF16)],
      input_output_aliases={5: 0},
      compiler_params=_params(3),
      name="attn_latent",
  )(*args)
  return outs[0], outs[1:]


def _edge_window_sums(slab, w):
  rows = slab.shape[0]
  back, fwd, m = slab, slab, 1
  while m < w // 2:
    back = back + pltpu.roll(back, m, 0)
    fwd = fwd + pltpu.roll(fwd, rows - m, 0)
    m *= 2
  return (pltpu.roll(back, 1, 0) + fwd)[POOL_HALO:2 * POOL_HALO]


def _pool_bands():
  t = np.arange(POOL_TILE)[:, None]
  j = np.arange(POOL_TILE)[None, :]
  bands = [((j >= t - w // 2) & (j < t + w - w // 2)) / w for w in POOL_WINDOWS]
  return jnp.asarray(np.stack(bands), dtype=BF16)


def _pool_kernel(x_ref, xprev_ref, xnext_ref, ada_ref, g_ref, band_ref, w_ref, ps_ref, o_ref):
  i = pl.program_id(0)
  is_ctx = i < _n_ctx_blk(POOL_BM)
  shift, scale = _mod_rows(ada_ref, _cond_row(i, POOL_BM))[:2]
  gain = g_ref[0, 0:1] * (1.0 + scale)
  pre = lambda x: _rms_scale(x) * gain + shift
  h = pre(x_ref[...])
  seq_pos = _lat_seq_pos(i, POOL_BM)[1]
  is_first = jnp.logical_or(is_ctx, seq_pos == 0)
  is_last = jnp.logical_or(is_ctx, seq_pos == LAT_T // POOL_BM - 1)
  ctx_edge = jnp.where(is_ctx, 1.0, 0.0)
  first_edge = jnp.where(is_first, 1.0, 0.0)
  last_edge = jnp.where(is_last, 1.0, 0.0)
  h_prev = jnp.where(is_first, 0.0, pre(xprev_ref[...]))
  h_next = jnp.where(is_last, 0.0, pre(xnext_ref[...]))
  h_hi = h.astype(BF16)
  h_lo = (h - h_hi.astype(F32)).astype(BF16)
  n_tiles = POOL_BM // POOL_TILE
  row = lax.broadcasted_iota(jnp.int32, (POOL_HALO, POOL_DG), 0)
  units = [(g, w, j) for g, w in enumerate(POOL_WINDOWS) for j in range(n_tiles)]
  col_of = lambda g: slice(g * POOL_DG, (g + 1) * POOL_DG)
  row_of = lambda j: slice(j * POOL_TILE, (j + 1) * POOL_TILE)

  parts = [jnp.dot(band_ref[g],
                   jnp.concatenate([h_hi[row_of(j), col_of(g)], h_lo[row_of(j), col_of(g)]], axis=1),
                   preferred_element_type=F32) for g, w, j in units]

  pooled = []
  for (g, w, j), part in zip(units, parts):
    r0, cols = j * POOL_TILE, col_of(g)
    hg = h[row_of(j), cols]
    inner = part[:, :POOL_DG] + part[:, POOL_DG:]
    before = (h_prev[:, cols] if j == 0
              else jnp.where(is_ctx, 0.0, h[r0 - POOL_HALO:r0, cols]))
    after = (h_next[:, cols] if j == n_tiles - 1
             else jnp.where(is_ctx, 0.0, h[r0 + POOL_TILE:r0 + POOL_TILE + POOL_HALO, cols]))
    top = _edge_window_sums(jnp.concatenate([before, hg[:2 * POOL_HALO]], axis=0), w)
    bot = _edge_window_sums(jnp.concatenate([hg[-2 * POOL_HALO:], after], axis=0), w)
    starts = first_edge if j == 0 else ctx_edge
    ends = last_edge if j == n_tiles - 1 else ctx_edge
    top_cnt = w - jnp.maximum(w // 2 - row, 0).astype(F32) * starts
    bot_cnt = w - jnp.maximum(row + (w - w // 2) - POOL_HALO, 0).astype(F32) * ends
    mean = jnp.concatenate(
        [top / top_cnt, inner[POOL_HALO:-POOL_HALO], bot / bot_cnt], axis=0)
    pooled.append((mean - hg).astype(BF16))

  for (g, w, j), p in zip(units, pooled):
    og = jnp.dot(p, w_ref[0, g], preferred_element_type=F32)
    o_ref[row_of(j), col_of(g)] = (og * ps_ref[0, :, col_of(g)]).astype(BF16)


def _pool(x, ada, gains, w_pool, pool_scale, layer):
  p = layer // 2
  bm = POOL_BM
  halo_per_blk = bm // POOL_HALO
  n_halo = N_TOK // POOL_HALO
  prev_spec = pl.BlockSpec((POOL_HALO, D), lambda i: (jnp.maximum(i * halo_per_blk - 1, 0), 0))
  next_spec = pl.BlockSpec(
      (POOL_HALO, D), lambda i: (jnp.minimum((i + 1) * halo_per_blk, n_halo - 1), 0))
  return pl.pallas_call(
      _pool_kernel,
      grid=(N_TOK // bm,),
      in_specs=_token_specs(False, D, bm) + [
          prev_spec, next_spec,
          _layer_spec((N_MOD, N_COND, D), layer),
          _layer_spec((4, D), layer),
          pl.BlockSpec((N_POOL_G, POOL_TILE, POOL_TILE), lambda i: (0, 0, 0),
                       pipeline_mode=pl.Buffered(1)),
          _layer_spec((N_POOL_G, POOL_DG, POOL_DG), p),
          _layer_spec((1, D), p),
      ],
      out_specs=_token_specs(False, D, bm)[0],
      out_shape=_token_shapes(False, D, BF16)[0],
      compiler_params=_params(1),
      name="pool",
  )(x, x, x, ada, gains, _pool_bands(), w_pool, pool_scale.reshape(-1, 1, D))


def _ffn_kernel(*refs, n_x, n_out, has_wo, cast_next):
  ahead = not has_wo
  x_refs, refs = refs[:n_x], refs[n_x:]
  if ahead:
    xn_refs, refs = refs[:n_x], refs[n_x:]
    (x1_scr, h_scr), refs = refs[-2:], refs[:-2]
  if cast_next:
    cast_dst, refs = refs[-2:], refs[:-2]
  o_refs, refs = refs[-n_out:], refs[:-n_out]
  if cast_next:
    cast_src, refs = refs[-2:], refs[:-2]
  if has_wo:
    mix_ref, ada_ref, g_ref, wo_ref, wup_ref, wdn_ref = refs
  else:
    mix_ref, mixn_ref, ada_ref, g_ref, wup_ref, wdn_ref = refs
  i = pl.program_id(0)
  nxt = jnp.minimum(i + 1, pl.num_programs(0) - 1)
  n_ctx = _n_ctx_blk(FFN_BM)
  mod = _mod_rows(ada_ref, _cond_row(i, FFN_BM))
  post_gain = mod[5] * g_ref[0, 3:4]
  if cast_next:
    _cast_ffn_weight_slab(cast_src, cast_dst)

  def pre_ffn(x, mix, mod):
    _, _, gate_m, shift_f, scale_f, _ = mod
    if has_wo:
      a = jnp.dot(mix, wo_ref[0], preferred_element_type=F32)
    else:
      a = mix.astype(F32)
    x1 = x + _rms_scale(a) * (gate_m * g_ref[0, 1:2])
    return x1, (_rms_scale(x1) * (g_ref[0, 2:3] * (1.0 + scale_f)) + shift_f).astype(BF16)

  group = FFN_BM // FFN_GROUPS
  groups = [slice(r * group, (r + 1) * group) for r in range(FFN_GROUPS)]
  pre_group = lambda rows: pre_ffn(_token_load(i < n_ctx, x_refs, rows), mix_ref[rows, :], mod)
  if ahead:
    @pl.when(i == 0)
    def _():
      x1_scr[...], h_scr[...] = pre_group(groups[0])

    x1_h = [(x1_scr[...], h_scr[...])] + [pre_group(rows) for rows in groups[1:]]
  else:
    x1_h = [pre_group(rows) for rows in groups]
  x1, h = zip(*x1_h)
  gu, y = [], []
  for r in range(FFN_GROUPS + 1):
    if r < FFN_GROUPS:
      gu.append(jnp.dot(h[r], wup_ref[...], preferred_element_type=F32))
    if ahead and r == FFN_GROUPS - 1:
      x1_h_nxt = pre_ffn(_token_load(nxt < n_ctx, xn_refs), mixn_ref[...],
                         _mod_rows(ada_ref, _cond_row(nxt, FFN_BM)))
    if r > 0:
      act = (_silu(gu[r - 1][:, :D_FF]) * gu[r - 1][:, D_FF:]).astype(BF16)
      y.append(jnp.dot(act, wdn_ref[...], preferred_element_type=F32))
  for rows, x1_r, y_r in zip(groups, x1, y):
    _token_store(i < n_ctx, o_refs, x1_r + _rms_scale(y_r) * post_gain, rows)
  if ahead:
    x1_scr[...], h_scr[...] = x1_h_nxt


def _next_group_specs(split, width, bm, group):
  n_blk, n_ctx, per_blk = N_TOK // bm, _n_ctx_blk(bm), bm // group
  nxt = lambda i: jnp.minimum(i + 1, n_blk - 1)
  if not split:
    return [pl.BlockSpec((group, width), lambda i: (nxt(i) * per_blk, 0))]
  return [
      pl.BlockSpec((group, width), lambda i: (jnp.minimum(nxt(i), n_ctx - 1) * per_blk, 0)),
      pl.BlockSpec((group, width), lambda i: (jnp.maximum(nxt(i) - n_ctx, 0) * per_blk, 0))]


def _ffn(xs, mix, ada, gains, w_o, w_up, w_down, next_w, layer, split_out):
  has_wo = w_o is not None
  cast_next = next_w is not None
  resident = lambda shape: pl.BlockSpec(shape, lambda i: (0, 0), pipeline_mode=pl.Buffered(1))
  bm = FFN_BM
  n_blk = N_TOK // bm
  group = bm // FFN_GROUPS
  split_in = len(xs) == 2
  ahead = not has_wo
  nxt_specs = lambda split: _next_group_specs(split, D, bm, group) if ahead else []
  in_specs = (_token_specs(split_in, D, bm) + nxt_specs(split_in)
              + _token_specs(False, D, bm) + nxt_specs(False) + [
                  _layer_spec((N_MOD, N_COND, D), layer),
                  _layer_spec((4, D), layer),
              ])
  args = list(xs) * (2 if ahead else 1) + [mix] * (2 if ahead else 1) + [ada, gains]
  if has_wo:
    in_specs.append(_layer_spec((Q_W, D), layer // 2))
    args.append(w_o)
  in_specs += [resident((D, 2 * D_FF)), resident((D_FF, D))]
  args += [w_up, w_down]
  n_out = 2 if split_out else 1
  out_specs = _token_specs(split_out, D, bm)
  out_shape = _token_shapes(split_out, D, F32)
  if cast_next:
    cast_in, cast_out, cast_shape = _ffn_weight_cast_specs(layer + 1, n_blk, lambda i: i)
    in_specs += cast_in
    args += list(next_w)
    out_specs += cast_out
    out_shape += cast_shape
  outs = pl.pallas_call(
      functools.partial(_ffn_kernel, n_x=len(xs), n_out=n_out, has_wo=has_wo,
                        cast_next=cast_next),
      grid=(n_blk,),
      in_specs=in_specs,
      out_specs=out_specs,
      out_shape=out_shape,
      scratch_shapes=[pltpu.VMEM((group, D), F32), pltpu.VMEM((group, D), BF16)] if ahead else [],
      compiler_params=_params(1),
      name="ffn_attn" if has_wo else "ffn_pool",
  )(*args)
  return outs[:n_out], outs[n_out:]


def _rope_tables():
  pos = np.arange(LAT_T)
  row = (pos // GRID_W).astype(np.float32)
  col = (pos % GRID_W).astype(np.float32)
  inv = np.float32(ROPE_BASE) ** (-np.arange(ROPE_PAIRS, dtype=np.float32) / np.float32(ROPE_PAIRS))
  ar = row[:, None] * inv
  ac = col[:, None] * inv
  cos = np.concatenate([np.cos(ar), np.cos(ac), np.cos(ar), np.cos(ac)], axis=1)
  sin = np.concatenate([-np.sin(ar), -np.sin(ac), np.sin(ar), np.sin(ac)], axis=1)
  return jnp.asarray(np.concatenate([cos, sin], axis=1), dtype=F32)


def kernel(x_prompt, x_sample, c, cache_k, cache_v, c_ctx, w_ada, b_ada, norm_gains,
           w_qkv, qk_gains, w_o, w_pool, pool_scale, w_up, w_down):
  xs = [x_prompt.reshape(N_CTX_TOK, D), x_sample.reshape(N_LAT_TOK, D)]
  cond = jnp.concatenate(
      [c_ctx[None, :], c, jnp.zeros((N_COND - 1 - N_LAT_SEQ, D), F32)], axis=0)
  ada = _modulation(cond, w_ada, b_ada)
  rope_cs = _rope_tables()
  w_o_b = w_o.astype(BF16)
  w_pool_b = w_pool.astype(BF16)

  new_kv = ffn_w = None
  for layer in range(DEPTH):
    last = layer == DEPTH - 1
    next_w = None if last else (w_up, w_down)
    if layer % 2 == 0:
      q, k, v, *new_kv = _qkv(xs, ada, norm_gains, w_qkv, qk_gains, rope_cs, new_kv, layer)
      mix, cast_w = _attn_latent(q, k, v, cache_k, cache_v, _attn_context(q, k, v), layer // 2,
                                 (w_up, w_down) if ffn_w is None else None)
      ffn_w = ffn_w or cast_w
      xs, ffn_w = _ffn(xs, mix, ada, norm_gains, w_o_b, *ffn_w, next_w, layer, last)
    else:
      mix = _pool(xs[0], ada, norm_gains, w_pool_b, pool_scale, layer)
      xs, ffn_w = _ffn(xs, mix, ada, norm_gains, None, *ffn_w, next_w, layer, last)

  out_kv = (N_CTX_SEQ, N_ATTN, CTX_T, N_KV, HD)
  return (xs[0].reshape(N_CTX_SEQ, CTX_T, D), xs[1].reshape(N_LAT_SEQ, LAT_T, D),
          new_kv[0].reshape(out_kv), new_kv[1].reshape(out_kv))
```

```python
import functools
import math

import jax
import jax.numpy as jnp
import numpy as np
from jax import lax
from jax.experimental import pallas as pl
from jax.experimental.pallas import tpu as pltpu

F32 = jnp.float32
BF16 = jnp.bfloat16

D = 1024
N_CTX_SEQ = 32
CTX_T = 256
N_LAT_SEQ = 4
LAT_T = 2048
PAST = 512
DEPTH = 4
N_ATTN = 2
GRID_W = 64
HD = 128
N_HEADS = 8
N_KV = 2
GQA = N_HEADS // N_KV
Q_W = N_HEADS * HD
KV_W = N_KV * HD
QKV_W = Q_W + 2 * KV_W
ROPE_PAIRS = HD // 4
ROPE_BASE = 10000.0
POOL_WINDOWS = (2, 4, 8, 16)
N_POOL_G = len(POOL_WINDOWS)
POOL_DG = D // N_POOL_G
D_FF = 2816
N_MOD = 6
EPS = 1e-6

N_CTX_TOK = N_CTX_SEQ * CTX_T
N_LAT_TOK = N_LAT_SEQ * LAT_T
N_TOK = N_CTX_TOK + N_LAT_TOK
N_COND = 8
Q_SCALE = math.log2(math.e) / math.sqrt(HD)

VMEM_LIMIT = 56 * 1024 * 1024

FFN_BM = 512
QKV_BM = 512
POOL_BM = 2048
FFN_GROUPS = 2
POOL_TILE = CTX_T
POOL_HALO = max(POOL_WINDOWS) // 2
ATT_TQ = 1024
ATT_CHUNK = 256
CTX_NB = 8


def _n_ctx_blk(bm):
  return N_CTX_TOK // bm


def _token_specs(split, width, bm):
  n_ctx = _n_ctx_blk(bm)
  if not split:
    return [pl.BlockSpec((bm, width), lambda i: (i, 0))]
  return [pl.BlockSpec((bm, width), lambda i: (jnp.minimum(i, n_ctx - 1), 0)),
          pl.BlockSpec((bm, width), lambda i: (jnp.maximum(i - n_ctx, 0), 0))]


def _token_shapes(split, width, dtype):
  if not split:
    return [jax.ShapeDtypeStruct((N_TOK, width), dtype)]
  return [jax.ShapeDtypeStruct((N_CTX_TOK, width), dtype),
          jax.ShapeDtypeStruct((N_LAT_TOK, width), dtype)]


def _token_load(is_ctx, refs, rows=slice(None)):
  if len(refs) == 1:
    return refs[0][rows, :]
  return jnp.where(is_ctx, refs[0][rows, :], refs[1][rows, :])


def _token_store(is_ctx, refs, value, rows=slice(None)):
  if len(refs) == 1:
    refs[0][rows, :] = value
    return

  @pl.when(is_ctx)
  def _():
    refs[0][rows, :] = value

  @pl.when(jnp.logical_not(is_ctx))
  def _():
    refs[1][rows, :] = value


def _lat_seq_pos(i, bm):
  lat_blk = jnp.maximum(i - _n_ctx_blk(bm), 0)
  return lat_blk // (LAT_T // bm), lat_blk % (LAT_T // bm)


def _cond_row(i, bm):
  return jnp.where(i < _n_ctx_blk(bm), 0, 1 + _lat_seq_pos(i, bm)[0])


def _mod_rows(ada_ref, row):
  return [ada_ref[0, m, pl.ds(row, 1), :] for m in range(N_MOD)]


def _layer_spec(shape, layer):
  zeros = (0,) * len(shape)
  return pl.BlockSpec((1,) + shape, lambda *_: (layer,) + zeros, pipeline_mode=pl.Buffered(1))


def _rms_scale(x):
  return x * lax.rsqrt(jnp.mean(x * x, axis=-1, keepdims=True) + EPS)


def _silu(x):
  return x / (1.0 + jnp.exp(-x))


BF16_SUBLANES = 16


def _ffn_weight_cast_specs(layer, n_steps, step_of):
  up_rows = D // n_steps
  dn_blks = n_steps
  while (D_FF // dn_blks) % BF16_SUBLANES:
    dn_blks //= 2
  dn_rows = D_FF // dn_blks
  up_blk = lambda *g: step_of(*g)
  dn_blk = lambda *g: jnp.minimum(step_of(*g), dn_blks - 1)
  in_specs = [pl.BlockSpec((1, up_rows, 2 * D_FF), lambda *g: (layer, up_blk(*g), 0)),
              pl.BlockSpec((1, dn_rows, D), lambda *g: (layer, dn_blk(*g), 0))]
  out_specs = [pl.BlockSpec((up_rows, 2 * D_FF), lambda *g: (up_blk(*g), 0)),
               pl.BlockSpec((dn_rows, D), lambda *g: (dn_blk(*g), 0))]
  out_shape = [jax.ShapeDtypeStruct((D, 2 * D_FF), BF16),
               jax.ShapeDtypeStruct((D_FF, D), BF16)]
  return in_specs, out_specs, out_shape


def _cast_ffn_weight_slab(src_refs, dst_refs):
  for src, dst in zip(src_refs, dst_refs):
    dst[...] = src[0].astype(BF16)


def _params(n_grid_dims):
  return pltpu.CompilerParams(
      dimension_semantics=("arbitrary",) * n_grid_dims, vmem_limit_bytes=VMEM_LIMIT)


MOD_PER_STEP = 3


def _mod_kernel(cond_ref, w_ref, b_ref, o_ref):
  s = _silu(cond_ref[...]).astype(BF16)
  for m in range(MOD_PER_STEP):
    w = w_ref[0, :, m * D:(m + 1) * D].astype(BF16)
    o_ref[0, m] = jnp.dot(s, w, preferred_element_type=F32) + b_ref[0, m]


def _modulation(cond, w_ada, b_ada):
  return pl.pallas_call(
      _mod_kernel,
      grid=(DEPTH, N_MOD // MOD_PER_STEP),
      in_specs=[
          pl.BlockSpec((N_COND, D), lambda l, m: (0, 0)),
          pl.BlockSpec((1, D, MOD_PER_STEP * D), lambda l, m: (l, 0, m)),
          pl.BlockSpec((1, MOD_PER_STEP, 1, D), lambda l, m: (l, m, 0, 0)),
      ],
      out_specs=pl.BlockSpec((1, MOD_PER_STEP, N_COND, D), lambda l, m: (l, m, 0, 0)),
      out_shape=jax.ShapeDtypeStruct((DEPTH, N_MOD, N_COND, D), F32),
      compiler_params=_params(2),
      name="modulation",
  )(cond, w_ada, b_ada.reshape(DEPTH, N_MOD, 1, D))


def _swap_middle(x):
  blk = lax.broadcasted_iota(jnp.int32, x.shape, 1) // ROPE_PAIRS
  return jnp.where(blk == 1, pltpu.roll(x, HD - ROPE_PAIRS, 1),
                   jnp.where(blk == 2, pltpu.roll(x, ROPE_PAIRS, 1), x))


def _qkv_kernel(*refs, n_x):
  x_refs, refs = refs[:n_x], refs[n_x:]
  ada_ref, g_ref, w_ref, qkg_ref, cs_ref = refs[:5]
  q_ref, k_ref, v_ref, nk_ref, nv_ref, w_scr = refs[-6:]
  i = pl.program_id(0)
  is_ctx = i < _n_ctx_blk(QKV_BM)
  shift, scale = _mod_rows(ada_ref, _cond_row(i, QKV_BM))[:2]
  gain = g_ref[0, 0:1] * (1.0 + scale)
  qk_gains = _swap_middle(qkg_ref[0])
  gq = qk_gains[0:1] * Q_SCALE
  gk = qk_gains[1:2]

  @pl.when(i == 0)
  def _():
    for hd in range(N_HEADS + N_KV):
      cols = slice(hd * HD, (hd + 1) * HD)
      w_scr[:, cols] = _swap_middle(w_ref[0, :, cols]).astype(BF16)
    w_scr[:, Q_W + KV_W:] = w_ref[0, :, Q_W + KV_W:].astype(BF16)
  groups = [slice(s * CTX_T, (s + 1) * CTX_T) for s in range(QKV_BM // CTX_T)]

  def body(x_ref, latent):
    h = [(_rms_scale(x_ref[rows, :]) * gain + shift).astype(BF16) for rows in groups]
    qkv = [jnp.dot(h_s, w_scr[...], preferred_element_type=F32) for h_s in h]
    for s, (rows, qkv_s) in enumerate(zip(groups, qkv)):
      if latent:
        cos = cs_ref[rows, :HD]
        sin = cs_ref[rows, HD:]
        rope = lambda x: x * cos + pltpu.roll(x, HD // 2, 1) * sin
      else:
        rope = lambda x: x
      for hd in range(N_HEADS):
        cols = slice(hd * HD, (hd + 1) * HD)
        q_ref[rows, cols] = rope(_rms_scale(qkv_s[:, cols]) * gq).astype(BF16)
      v = qkv_s[:, Q_W + KV_W:]
      v_ref[rows, :] = v.astype(BF16)
      for hd in range(N_KV):
        cols = slice(hd * HD, (hd + 1) * HD)
        k = rope(_rms_scale(qkv_s[:, Q_W + hd * HD:Q_W + (hd + 1) * HD]) * gk)
        k_ref[rows, cols] = k.astype(BF16)
        if not latent:
          nk_ref[s, 0, pl.ds(hd, CTX_T, stride=N_KV), :] = _swap_middle(k)
          nv_ref[s, 0, pl.ds(hd, CTX_T, stride=N_KV), :] = v[:, cols]

  @pl.when(is_ctx)
  def _():
    body(x_refs[0], latent=False)

  @pl.when(jnp.logical_not(is_ctx))
  def _():
    body(x_refs[-1], latent=True)


def _qkv(xs, ada, gains, w_qkv, qk_gains, rope_cs, new_kv, layer):
  a = layer // 2
  bm = QKV_BM
  new_spec = pl.BlockSpec(
      (bm // CTX_T, 1, CTX_T * N_KV, HD),
      lambda i: (jnp.minimum(i, _n_ctx_blk(bm) - 1), a, 0, 0))
  new_shape = jax.ShapeDtypeStruct((N_CTX_SEQ, N_ATTN, CTX_T * N_KV, HD), F32)
  args = list(xs) + [ada, gains, w_qkv, qk_gains, rope_cs]
  in_specs = _token_specs(len(xs) == 2, D, bm) + [
      _layer_spec((N_MOD, N_COND, D), layer),
      _layer_spec((4, D), layer),
      _layer_spec((D, QKV_W), a),
      _layer_spec((2, HD), a),
      pl.BlockSpec((bm, 2 * HD), lambda i: (_lat_seq_pos(i, bm)[1], 0)),
  ]
  aliases = {}
  if new_kv is not None:
    aliases = {len(args): 3, len(args) + 1: 4}
    args += list(new_kv)
    in_specs += [pl.BlockSpec(memory_space=pl.ANY)] * 2
  return pl.pallas_call(
      functools.partial(_qkv_kernel, n_x=len(xs)),
      grid=(N_TOK // bm,),
      in_specs=in_specs,
      out_specs=(_token_specs(False, Q_W, bm) + _token_specs(False, KV_W, bm) * 2
                 + [new_spec] * 2),
      out_shape=(_token_shapes(False, Q_W, BF16) + _token_shapes(False, KV_W, BF16) * 2
                 + [new_shape] * 2),
      scratch_shapes=[pltpu.VMEM((D, QKV_W), BF16)],
      input_output_aliases=aliases,
      compiler_params=_params(1),
      name="qkv",
  )(*args)


def _nt_dot(a, b):
  return lax.dot_general(a, b, (((1,), (1,)), ((), ())), preferred_element_type=F32)


def _with_ones(v):
  return jnp.concatenate([v, jnp.ones_like(v)], axis=1)


def _softmax_pv(s, v1):
  p = jnp.exp2(s - jnp.max(s, axis=-1, keepdims=True)).astype(BF16)
  oa = jnp.dot(p, v1, preferred_element_type=F32)
  return oa[:, :HD] / oa[:, HD:]


def _attn_ctx_kernel(q_ref, k_ref, v_ref, o_ref):
  for b in range(CTX_NB):
    rows = slice(b * CTX_T, (b + 1) * CTX_T)
    for h in range(N_KV):
      kv_cols = slice(h * HD, (h + 1) * HD)
      v1 = _with_ones(v_ref[rows, kv_cols])
      for g in range(GQA):
        cols = slice((h * GQA + g) * HD, (h * GQA + g + 1) * HD)
        s = _nt_dot(q_ref[rows, cols], k_ref[rows, kv_cols])
        o_ref[rows, cols] = _softmax_pv(s, v1).astype(BF16)


def _attn_context(q, k, v):
  bm = CTX_NB * CTX_T
  row = lambda i: (i, 0)
  return pl.pallas_call(
      _attn_ctx_kernel,
      grid=(N_CTX_TOK // bm,),
      in_specs=[
          pl.BlockSpec((bm, Q_W), row),
          pl.BlockSpec((bm, KV_W), row),
          pl.BlockSpec((bm, KV_W), row),
      ],
      out_specs=pl.BlockSpec((bm, Q_W), row),
      out_shape=jax.ShapeDtypeStruct((N_TOK, Q_W), BF16),
      compiler_params=_params(1),
      name="attn_context",
  )(q, k, v)


def _attn_lat_kernel(*refs, cast_weights):
  q_ref, k_ref, v_ref, ck_ref, cv_ref, _ = refs[:6]
  kcat, vcat = refs[-2:]
  if cast_weights:
    _cast_ffn_weight_slab(refs[6:8], refs[-4:-2])
    o_ref = refs[8]
  else:
    o_ref = refs[6]

  @pl.when(pl.program_id(2) == 0)
  def _():
    head_rows = pl.ds(pl.program_id(1), PAST, stride=N_KV)
    kcat[:LAT_T] = k_ref[...]
    kcat[LAT_T:] = _swap_middle(ck_ref[0, 0, head_rows, :]).astype(BF16)
    vcat[:LAT_T] = _with_ones(v_ref[...])
    vcat[LAT_T:] = _with_ones(cv_ref[0, 0, head_rows, :].astype(BF16))

  chains = [(g, r, ATT_CHUNK) for g in range(GQA) for r in range(0, ATT_TQ, ATT_CHUNK)]
  halves = lambda g, r, n: [(g, r, n // 2), (g, r + n // 2, n // 2)]
  chains = halves(*chains[0]) + chains[1:-1] + halves(*chains[-1])
  for g, r, n in chains:
    rows, cols = slice(r, r + n), slice(g * HD, (g + 1) * HD)
    s = _nt_dot(q_ref[rows, cols], kcat[...])
    o_ref[rows, cols] = _softmax_pv(s, vcat[...]).astype(BF16)


def _attn_latent(q, k, v, cache_k, cache_v, mix, a, cast_w):
  nq = LAT_T // ATT_TQ
  q0 = N_CTX_TOK // ATT_TQ
  kv0 = N_CTX_TOK // LAT_T
  cache_rows = (N_LAT_SEQ, N_ATTN, PAST * N_KV, HD)
  cache_spec = pl.BlockSpec((1, 1, PAST * N_KV, HD), lambda b, h, j: (b, a, 0, 0))
  in_specs = [
      pl.BlockSpec((ATT_TQ, GQA * HD), lambda b, h, j: (q0 + b * nq + j, h)),
      pl.BlockSpec((LAT_T, HD), lambda b, h, j: (kv0 + b, h)),
      pl.BlockSpec((LAT_T, HD), lambda b, h, j: (kv0 + b, h)),
      cache_spec,
      cache_spec,
      pl.BlockSpec(memory_space=pl.ANY),
  ]
  args = [q, k, v, cache_k.reshape(cache_rows), cache_v.reshape(cache_rows), mix]
  out_specs = [pl.BlockSpec((ATT_TQ, GQA * HD), lambda b, h, j: (q0 + b * nq + j, h))]
  out_shape = [jax.ShapeDtypeStruct((N_TOK, Q_W), BF16)]
  if cast_w is not None:
    cast_in, cast_out, cast_shape = _ffn_weight_cast_specs(
        0, N_LAT_SEQ * N_KV * nq, lambda b, h, j: (b * N_KV + h) * nq + j)
    in_specs += cast_in
    args += list(cast_w)
    out_specs += cast_out
    out_shape += cast_shape
  outs = pl.pallas_call(
      functools.partial(_attn_lat_kernel, cast_weights=cast_w is not None),
      grid=(N_LAT_SEQ, N_KV, nq),
      in_specs=in_specs,
      out_specs=out_specs,
      out_shape=out_shape,
      scratch_shapes=[pltpu.VMEM((LAT_T + PAST, HD), BF16),
                      pltpu.VMEM((LAT_T + PAST, 2 * HD), BF16)],
      input_output_aliases={5: 0},
      compiler_params=_params(3),
      name="attn_latent",
  )(*args)
  return outs[0], outs[1:]


def _edge_window_sums(slab, w):
  rows = slab.shape[0]
  back, fwd, m = slab, slab, 1
  while m < w // 2:
    back = back + pltpu.roll(back, m, 0)
    fwd = fwd + pltpu.roll(fwd, rows - m, 0)
    m *= 2
  return (pltpu.roll(back, 1, 0) + fwd)[POOL_HALO:2 * POOL_HALO]


def _pool_bands():
  t = np.arange(POOL_TILE)[:, None]
  j = np.arange(POOL_TILE)[None, :]
  bands = [((j >= t - w // 2) & (j < t + w - w // 2)) / w for w in POOL_WINDOWS]
  return jnp.asarray(np.stack(bands), dtype=BF16)


def _pool_kernel(x_ref, xprev_ref, xnext_ref, ada_ref, g_ref, band_ref, w_ref, ps_ref, o_ref):
  i = pl.program_id(0)
  is_ctx = i < _n_ctx_blk(POOL_BM)
  shift, scale = _mod_rows(ada_ref, _cond_row(i, POOL_BM))[:2]
  gain = g_ref[0, 0:1] * (1.0 + scale)
  pre = lambda x: _rms_scale(x) * gain + shift
  h = pre(x_ref[...])
  seq_pos = _lat_seq_pos(i, POOL_BM)[1]
  is_first = jnp.logical_or(is_ctx, seq_pos == 0)
  is_last = jnp.logical_or(is_ctx, seq_pos == LAT_T // POOL_BM - 1)
  ctx_edge = jnp.where(is_ctx, 1.0, 0.0)
  first_edge = jnp.where(is_first, 1.0, 0.0)
  last_edge = jnp.where(is_last, 1.0, 0.0)
  h_prev = jnp.where(is_first, 0.0, pre(xprev_ref[...]))
  h_next = jnp.where(is_last, 0.0, pre(xnext_ref[...]))
  h_hi = h.astype(BF16)
  h_lo = (h - h_hi.astype(F32)).astype(BF16)
  n_tiles = POOL_BM // POOL_TILE
  row = lax.broadcasted_iota(jnp.int32, (POOL_HALO, POOL_DG), 0)
  units = [(g, w, j) for g, w in enumerate(POOL_WINDOWS) for j in range(n_tiles)]
  col_of = lambda g: slice(g * POOL_DG, (g + 1) * POOL_DG)
  row_of = lambda j: slice(j * POOL_TILE, (j + 1) * POOL_TILE)

  parts = [jnp.dot(band_ref[g],
                   jnp.concatenate([h_hi[row_of(j), col_of(g)], h_lo[row_of(j), col_of(g)]], axis=1),
                   preferred_element_type=F32) for g, w, j in units]

  pooled = []
  for (g, w, j), part in zip(units, parts):
    r0, cols = j * POOL_TILE, col_of(g)
    hg = h[row_of(j), cols]
    inner = part[:, :POOL_DG] + part[:, POOL_DG:]
    before = (h_prev[:, cols] if j == 0
              else jnp.where(is_ctx, 0.0, h[r0 - POOL_HALO:r0, cols]))
    after = (h_next[:, cols] if j == n_tiles - 1
             else jnp.where(is_ctx, 0.0, h[r0 + POOL_TILE:r0 + POOL_TILE + POOL_HALO, cols]))
    top = _edge_window_sums(jnp.concatenate([before, hg[:2 * POOL_HALO]], axis=0), w)
    bot = _edge_window_sums(jnp.concatenate([hg[-2 * POOL_HALO:], after], axis=0), w)
    starts = first_edge if j == 0 else ctx_edge
    ends = last_edge if j == n_tiles - 1 else ctx_edge
    top_cnt = w - jnp.maximum(w // 2 - row, 0).astype(F32) * starts
    bot_cnt = w - jnp.maximum(row + (w - w // 2) - POOL_HALO, 0).astype(F32) * ends
    mean = jnp.concatenate(
        [top / top_cnt, inner[POOL_HALO:-POOL_HALO], bot / bot_cnt], axis=0)
    pooled.append((mean - hg).astype(BF16))

  for (g, w, j), p in zip(units, pooled):
    og = jnp.dot(p, w_ref[0, g], preferred_element_type=F32)
    o_ref[row_of(j), col_of(g)] = (og * ps_ref[0, :, col_of(g)]).astype(BF16)


def _pool(x, ada, gains, w_pool, pool_scale, layer):
  p = layer // 2
  bm = POOL_BM
  halo_per_blk = bm // POOL_HALO
  n_halo = N_TOK // POOL_HALO
  prev_spec = pl.BlockSpec((POOL_HALO, D), lambda i: (jnp.maximum(i * halo_per_blk - 1, 0), 0))
  next_spec = pl.BlockSpec(
      (POOL_HALO, D), lambda i: (jnp.minimum((i + 1) * halo_per_blk, n_halo - 1), 0))
  return pl.pallas_call(
      _pool_kernel,
      grid=(N_TOK // bm,),
      in_specs=_token_specs(False, D, bm) + [
          prev_spec, next_spec,
          _layer_spec((N_MOD, N_COND, D), layer),
          _layer_spec((4, D), layer),
          pl.BlockSpec((N_POOL_G, POOL_TILE, POOL_TILE), lambda i: (0, 0, 0),
                       pipeline_mode=pl.Buffered(1)),
          _layer_spec((N_POOL_G, POOL_DG, POOL_DG), p),
          _layer_spec((1, D), p),
      ],
      out_specs=_token_specs(False, D, bm)[0],
      out_shape=_token_shapes(False, D, BF16)[0],
      compiler_params=_params(1),
      name="pool",
  )(x, x, x, ada, gains, _pool_bands(), w_pool, pool_scale.reshape(-1, 1, D))


def _ffn_kernel(*refs, n_x, n_out, has_wo, cast_next):
  ahead = not has_wo
  x_refs, refs = refs[:n_x], refs[n_x:]
  if ahead:
    xn_refs, refs = refs[:n_x], refs[n_x:]
    (x1_scr, h_scr), refs = refs[-2:], refs[:-2]
  if cast_next:
    cast_dst, refs = refs[-2:], refs[:-2]
  o_refs, refs = refs[-n_out:], refs[:-n_out]
  if cast_next:
    cast_src, refs = refs[-2:], refs[:-2]
  if has_wo:
    mix_ref, ada_ref, g_ref, wo_ref, wup_ref, wdn_ref = refs
  else:
    mix_ref, mixn_ref, ada_ref, g_ref, wup_ref, wdn_ref = refs
  i = pl.program_id(0)
  nxt = jnp.minimum(i + 1, pl.num_programs(0) - 1)
  n_ctx = _n_ctx_blk(FFN_BM)
  mod = _mod_rows(ada_ref, _cond_row(i, FFN_BM))
  post_gain = mod[5] * g_ref[0, 3:4]
  if cast_next:
    _cast_ffn_weight_slab(cast_src, cast_dst)

  def pre_ffn(x, mix, mod):
    _, _, gate_m, shift_f, scale_f, _ = mod
    if has_wo:
      a = jnp.dot(mix, wo_ref[0], preferred_element_type=F32)
    else:
      a = mix.astype(F32)
    x1 = x + _rms_scale(a) * (gate_m * g_ref[0, 1:2])
    return x1, (_rms_scale(x1) * (g_ref[0, 2:3] * (1.0 + scale_f)) + shift_f).astype(BF16)

  group = FFN_BM // FFN_GROUPS
  groups = [slice(r * group, (r + 1) * group) for r in range(FFN_GROUPS)]
  pre_group = lambda rows: pre_ffn(_token_load(i < n_ctx, x_refs, rows), mix_ref[rows, :], mod)
  if ahead:
    @pl.when(i == 0)
    def _():
      x1_scr[...], h_scr[...] = pre_group(groups[0])

    x1_h = [(x1_scr[...], h_scr[...])] + [pre_group(rows) for rows in groups[1:]]
  else:
    x1_h = [pre_group(rows) for rows in groups]
  x1, h = zip(*x1_h)
  gu, y = [], []
  for r in range(FFN_GROUPS + 1):
    if r < FFN_GROUPS:
      gu.append(jnp.dot(h[r], wup_ref[...], preferred_element_type=F32))
    if ahead and r == FFN_GROUPS - 1:
      x1_h_nxt = pre_ffn(_token_load(nxt < n_ctx, xn_refs), mixn_ref[...],
                         _mod_rows(ada_ref, _cond_row(nxt, FFN_BM)))
    if r > 0:
      act = (_silu(gu[r - 1][:, :D_FF]) * gu[r - 1][:, D_FF:]).astype(BF16)
      y.append(jnp.dot(act, wdn_ref[...], preferred_element_type=F32))
  for rows, x1_r, y_r in zip(groups, x1, y):
    _token_store(i < n_ctx, o_refs, x1_r + _rms_scale(y_r) * post_gain, rows)
  if ahead:
    x1_scr[...], h_scr[...] = x1_h_nxt


def _next_group_specs(split, width, bm, group):
  n_blk, n_ctx, per_blk = N_TOK // bm, _n_ctx_blk(bm), bm // group
  nxt = lambda i: jnp.minimum(i + 1, n_blk - 1)
  if not split:
    return [pl.BlockSpec((group, width), lambda i: (nxt(i) * per_blk, 0))]
  return [
      pl.BlockSpec((group, width), lambda i: (jnp.minimum(nxt(i), n_ctx - 1) * per_blk, 0)),
      pl.BlockSpec((group, width), lambda i: (jnp.maximum(nxt(i) - n_ctx, 0) * per_blk, 0))]


def _ffn(xs, mix, ada, gains, w_o, w_up, w_down, next_w, layer, split_out):
  has_wo = w_o is not None
  cast_next = next_w is not None
  resident = lambda shape: pl.BlockSpec(shape, lambda i: (0, 0), pipeline_mode=pl.Buffered(1))
  bm = FFN_BM
  n_blk = N_TOK // bm
  group = bm // FFN_GROUPS
  split_in = len(xs) == 2
  ahead = not has_wo
  nxt_specs = lambda split: _next_group_specs(split, D, bm, group) if ahead else []
  in_specs = (_token_specs(split_in, D, bm) + nxt_specs(split_in)
              + _token_specs(False, D, bm) + nxt_specs(False) + [
                  _layer_spec((N_MOD, N_COND, D), layer),
                  _layer_spec((4, D), layer),
              ])
  args = list(xs) * (2 if ahead else 1) + [mix] * (2 if ahead else 1) + [ada, gains]
  if has_wo:
    in_specs.append(_layer_spec((Q_W, D), layer // 2))
    args.append(w_o)
  in_specs += [resident((D, 2 * D_FF)), resident((D_FF, D))]
  args += [w_up, w_down]
  n_out = 2 if split_out else 1
  out_specs = _token_specs(split_out, D, bm)
  out_shape = _token_shapes(split_out, D, F32)
  if cast_next:
    cast_in, cast_out, cast_shape = _ffn_weight_cast_specs(layer + 1, n_blk, lambda i: i)
    in_specs += cast_in
    args += list(next_w)
    out_specs += cast_out
    out_shape += cast_shape
  outs = pl.pallas_call(
      functools.partial(_ffn_kernel, n_x=len(xs), n_out=n_out, has_wo=has_wo,
                        cast_next=cast_next),
      grid=(n_blk,),
      in_specs=in_specs,
      out_specs=out_specs,
      out_shape=out_shape,
      scratch_shapes=[pltpu.VMEM((group, D), F32), pltpu.VMEM((group, D), BF16)] if ahead else [],
      compiler_params=_params(1),
      name="ffn_attn" if has_wo else "ffn_pool",
  )(*args)
  return outs[:n_out], outs[n_out:]


def _rope_tables():
  pos = np.arange(LAT_T)
  row = (pos // GRID_W).astype(np.float32)
  col = (pos % GRID_W).astype(np.float32)
  inv = np.float32(ROPE_BASE) ** (-np.arange(ROPE_PAIRS, dtype=np.float32) / np.float32(ROPE_PAIRS))
  ar = row[:, None] * inv
  ac = col[:, None] * inv
  cos = np.concatenate([np.cos(ar), np.cos(ac), np.cos(ar), np.cos(ac)], axis=1)
  sin = np.concatenate([-np.sin(ar), -np.sin(ac), np.sin(ar), np.sin(ac)], axis=1)
  return jnp.asarray(np.concatenate([cos, sin], axis=1), dtype=F32)


def kernel(x_prompt, x_sample, c, cache_k, cache_v, c_ctx, w_ada, b_ada, norm_gains,
           w_qkv, qk_gains, w_o, w_pool, pool_scale, w_up, w_down):
  xs = [x_prompt.reshape(N_CTX_TOK, D), x_sample.reshape(N_LAT_TOK, D)]
  cond = jnp.concatenate(
      [c_ctx[None, :], c, jnp.zeros((N_COND - 1 - N_LAT_SEQ, D), F32)], axis=0)
  ada = _modulation(cond, w_ada, b_ada)
  rope_cs = _rope_tables()
  w_o_b = w_o.astype(BF16)
  w_pool_b = w_pool.astype(BF16)

  new_kv = ffn_w = None
  for layer in range(DEPTH):
    last = layer == DEPTH - 1
    next_w = None if last else (w_up, w_down)
    if layer % 2 == 0:
      q, k, v, *new_kv = _qkv(xs, ada, norm_gains, w_qkv, qk_gains, rope_cs, new_kv, layer)
      mix, cast_w = _attn_latent(q, k, v, cache_k, cache_v, _attn_context(q, k, v), layer // 2,
                                 (w_up, w_down) if ffn_w is None else None)
      ffn_w = ffn_w or cast_w
      xs, ffn_w = _ffn(xs, mix, ada, norm_gains, w_o_b, *ffn_w, next_w, layer, last)
    else:
      mix = _pool(xs[0], ada, norm_gains, w_pool_b, pool_scale, layer)
      xs, ffn_w = _ffn(xs, mix, ada, norm_gains, None, *ffn_w, next_w, layer, last)

  out_kv = (N_CTX_SEQ, N_ATTN, CTX_T, N_KV, HD)
  return (xs[0].reshape(N_CTX_SEQ, CTX_T, D), xs[1].reshape(N_LAT_SEQ, LAT_T, D),
          new_kv[0].reshape(out_kv), new_kv[1].reshape(out_kv))
```

```python
import functools
import math

import jax
import jax.numpy as jnp
import numpy as np
from jax import lax
from jax.experimental import pallas as pl
from jax.experimental.pallas import tpu as pltpu

F32 = jnp.float32
BF16 = jnp.bfloat16

D = 1024
N_CTX_SEQ = 32
CTX_T = 256
N_LAT_SEQ = 4
LAT_T = 2048
PAST = 512
DEPTH = 4
N_ATTN = 2
GRID_W = 64
HD = 128
N_HEADS = 8
N_KV = 2
GQA = N_HEADS // N_KV
Q_W = N_HEADS * HD
KV_W = N_KV * HD
QKV_W = Q_W + 2 * KV_W
ROPE_PAIRS = HD // 4
ROPE_BASE = 10000.0
POOL_WINDOWS = (2, 4, 8, 16)
N_POOL_G = len(POOL_WINDOWS)
POOL_DG = D // N_POOL_G
D_FF = 2816
N_MOD = 6
EPS = 1e-6

N_CTX_TOK = N_CTX_SEQ * CTX_T
N_LAT_TOK = N_LAT_SEQ * LAT_T
N_TOK = N_CTX_TOK + N_LAT_TOK
N_COND = 8
Q_SCALE = math.log2(math.e) / math.sqrt(HD)

VMEM_LIMIT = 56 * 1024 * 1024

FFN_BM = 512
QKV_BM = 512
POOL_BM = 2048
FFN_GROUPS = 2
POOL_TILE = CTX_T
POOL_HALO = max(POOL_WINDOWS) // 2
ATT_TQ = 1024
ATT_CHUNK = 256
CTX_NB = 8


def _n_ctx_blk(bm):
  return N_CTX_TOK // bm


def _token_specs(split, width, bm):
  n_ctx = _n_ctx_blk(bm)
  if not split:
    return [pl.BlockSpec((bm, width), lambda i: (i, 0))]
  return [pl.BlockSpec((bm, width), lambda i: (jnp.minimum(i, n_ctx - 1), 0)),
          pl.BlockSpec((bm, width), lambda i: (jnp.maximum(i - n_ctx, 0), 0))]


def _token_shapes(split, width, dtype):
  if not split:
    return [jax.ShapeDtypeStruct((N_TOK, width), dtype)]
  return [jax.ShapeDtypeStruct((N_CTX_TOK, width), dtype),
          jax.ShapeDtypeStruct((N_LAT_TOK, width), dtype)]


def _token_load(is_ctx, refs, rows=slice(None)):
  if len(refs) == 1:
    return refs[0][rows, :]
  return jnp.where(is_ctx, refs[0][rows, :], refs[1][rows, :])


def _token_store(is_ctx, refs, value, rows=slice(None)):
  if len(refs) == 1:
    refs[0][rows, :] = value
    return

  @pl.when(is_ctx)
  def _():
    refs[0][rows, :] = value

  @pl.when(jnp.logical_not(is_ctx))
  def _():
    refs[1][rows, :] = value


def _lat_seq_pos(i, bm):
  lat_blk = jnp.maximum(i - _n_ctx_blk(bm), 0)
  return lat_blk // (LAT_T // bm), lat_blk % (LAT_T // bm)


def _cond_row(i, bm):
  return jnp.where(i < _n_ctx_blk(bm), 0, 1 + _lat_seq_pos(i, bm)[0])


def _mod_rows(ada_ref, row):
  return [ada_ref[0, m, pl.ds(row, 1), :] for m in range(N_MOD)]


def _layer_spec(shape, layer):
  zeros = (0,) * len(shape)
  return pl.BlockSpec((1,) + shape, lambda *_: (layer,) + zeros, pipeline_mode=pl.Buffered(1))


def _rms_scale(x):
  return x * lax.rsqrt(jnp.mean(x * x, axis=-1, keepdims=True) + EPS)


def _silu(x):
  return x / (1.0 + jnp.exp(-x))


BF16_SUBLANES = 16


def _ffn_weight_cast_specs(layer, n_steps, step_of):
  up_rows = D // n_steps
  dn_blks = n_steps
  while (D_FF // dn_blks) % BF16_SUBLANES:
    dn_blks //= 2
  dn_rows = D_FF // dn_blks
  up_blk = lambda *g: step_of(*g)
  dn_blk = lambda *g: jnp.minimum(step_of(*g), dn_blks - 1)
  in_specs = [pl.BlockSpec((1, up_rows, 2 * D_FF), lambda *g: (layer, up_blk(*g), 0)),
              pl.BlockSpec((1, dn_rows, D), lambda *g: (layer, dn_blk(*g), 0))]
  out_specs = [pl.BlockSpec((up_rows, 2 * D_FF), lambda *g: (up_blk(*g), 0)),
               pl.BlockSpec((dn_rows, D), lambda *g: (dn_blk(*g), 0))]
  out_shape = [jax.ShapeDtypeStruct((D, 2 * D_FF), BF16),
               jax.ShapeDtypeStruct((D_FF, D), BF16)]
  return in_specs, out_specs, out_shape


def _cast_ffn_weight_slab(src_refs, dst_refs):
  for src, dst in zip(src_refs, dst_refs):
    dst[...] = src[0].astype(BF16)


def _params(n_grid_dims):
  return pltpu.CompilerParams(
      dimension_semantics=("arbitrary",) * n_grid_dims, vmem_limit_bytes=VMEM_LIMIT)


MOD_PER_STEP = 3


def _mod_kernel(cond_ref, w_ref, b_ref, o_ref):
  s = _silu(cond_ref[...]).astype(BF16)
  for m in range(MOD_PER_STEP):
    w = w_ref[0, :, m * D:(m + 1) * D].astype(BF16)
    o_ref[0, m] = jnp.dot(s, w, preferred_element_type=F32) + b_ref[0, m]


def _modulation(cond, w_ada, b_ada):
  return pl.pallas_call(
      _mod_kernel,
      grid=(DEPTH, N_MOD // MOD_PER_STEP),
      in_specs=[
          pl.BlockSpec((N_COND, D), lambda l, m: (0, 0)),
          pl.BlockSpec((1, D, MOD_PER_STEP * D), lambda l, m: (l, 0, m)),
          pl.BlockSpec((1, MOD_PER_STEP, 1, D), lambda l, m: (l, m, 0, 0)),
      ],
      out_specs=pl.BlockSpec((1, MOD_PER_STEP, N_COND, D), lambda l, m: (l, m, 0, 0)),
      out_shape=jax.ShapeDtypeStruct((DEPTH, N_MOD, N_COND, D), F32),
      compiler_params=_params(2),
      name="modulation",
  )(cond, w_ada, b_ada.reshape(DEPTH, N_MOD, 1, D))


def _swap_middle(x):
  blk = lax.broadcasted_iota(jnp.int32, x.shape, 1) // ROPE_PAIRS
  return jnp.where(blk == 1, pltpu.roll(x, HD - ROPE_PAIRS, 1),
                   jnp.where(blk == 2, pltpu.roll(x, ROPE_PAIRS, 1), x))


def _qkv_kernel(*refs, n_x, slot):
  x_refs, refs = refs[:n_x], refs[n_x:]
  ada_ref, g_ref, w_ref, qkg_ref, cs_ref = refs[:5]
  earlier_kv = refs[5:-6]
  q_ref, k_ref, v_ref, nk_ref, nv_ref, w_scr = refs[-6:]
  i = pl.program_id(0)
  is_ctx = i < _n_ctx_blk(QKV_BM)
  shift, scale = _mod_rows(ada_ref, _cond_row(i, QKV_BM))[:2]
  gain = g_ref[0, 0:1] * (1.0 + scale)
  qk_gains = _swap_middle(qkg_ref[0])
  gq = qk_gains[0:1] * Q_SCALE
  gk = qk_gains[1:2]

  @pl.when(i == 0)
  def _():
    for hd in range(N_HEADS + N_KV):
      cols = slice(hd * HD, (hd + 1) * HD)
      w_scr[:, cols] = _swap_middle(w_ref[0, :, cols]).astype(BF16)
    w_scr[:, Q_W + KV_W:] = w_ref[0, :, Q_W + KV_W:].astype(BF16)
  groups = [slice(s * CTX_T, (s + 1) * CTX_T) for s in range(QKV_BM // CTX_T)]

  def body(x_ref, latent):
    h = [(_rms_scale(x_ref[rows, :]) * gain + shift).astype(BF16) for rows in groups]
    qkv = [jnp.dot(h_s, w_scr[...], preferred_element_type=F32) for h_s in h]
    for s, (rows, qkv_s) in enumerate(zip(groups, qkv)):
      if latent:
        cos = cs_ref[rows, :HD]
        sin = cs_ref[rows, HD:]
        rope = lambda x: x * cos + pltpu.roll(x, HD // 2, 1) * sin
      else:
        rope = lambda x: x
      for hd in range(N_HEADS):
        cols = slice(hd * HD, (hd + 1) * HD)
        q_ref[rows, cols] = rope(_rms_scale(qkv_s[:, cols]) * gq).astype(BF16)
      v = qkv_s[:, Q_W + KV_W:]
      v_ref[rows, :] = v.astype(BF16)
      for hd in range(N_KV):
        cols = slice(hd * HD, (hd + 1) * HD)
        k = rope(_rms_scale(qkv_s[:, Q_W + hd * HD:Q_W + (hd + 1) * HD]) * gk)
        k_ref[rows, cols] = k.astype(BF16)
        if not latent:
          nk_ref[s, slot, pl.ds(hd, CTX_T, stride=N_KV), :] = _swap_middle(k)
          nv_ref[s, slot, pl.ds(hd, CTX_T, stride=N_KV), :] = v[:, cols]

  @pl.when(is_ctx)
  def _():
    for src, dst in zip(earlier_kv, (nk_ref, nv_ref)):
      dst[:, :slot] = src[...]
    body(x_refs[0], latent=False)

  @pl.when(jnp.logical_not(is_ctx))
  def _():
    body(x_refs[-1], latent=True)


def _qkv(xs, ada, gains, w_qkv, qk_gains, rope_cs, new_kv, layer):
  a = layer // 2
  bm = QKV_BM
  new_spec = lambda slots: pl.BlockSpec(
      (bm // CTX_T, slots, CTX_T * N_KV, HD),
      lambda i: (jnp.minimum(i, _n_ctx_blk(bm) - 1), 0, 0, 0))
  new_shape = jax.ShapeDtypeStruct((N_CTX_SEQ, a + 1, CTX_T * N_KV, HD), F32)
  args = list(xs) + [ada, gains, w_qkv, qk_gains, rope_cs]
  in_specs = _token_specs(len(xs) == 2, D, bm) + [
      _layer_spec((N_MOD, N_COND, D), layer),
      _layer_spec((4, D), layer),
      _layer_spec((D, QKV_W), a),
      _layer_spec((2, HD), a),
      pl.BlockSpec((bm, 2 * HD), lambda i: (_lat_seq_pos(i, bm)[1], 0)),
  ]
  if new_kv is not None:
    args += list(new_kv)
    in_specs += [new_spec(a)] * 2
  return pl.pallas_call(
      functools.partial(_qkv_kernel, n_x=len(xs), slot=a),
      grid=(N_TOK // bm,),
      in_specs=in_specs,
      out_specs=(_token_specs(False, Q_W, bm) + _token_specs(False, KV_W, bm) * 2
                 + [new_spec(a + 1)] * 2),
      out_shape=(_token_shapes(False, Q_W, BF16) + _token_shapes(False, KV_W, BF16) * 2
                 + [new_shape] * 2),
      scratch_shapes=[pltpu.VMEM((D, QKV_W), BF16)],
      compiler_params=_params(1),
      name="qkv",
  )(*args)


def _nt_dot(a, b):
  return lax.dot_general(a, b, (((1,), (1,)), ((), ())), preferred_element_type=F32)


def _with_ones(v):
  return jnp.concatenate([v, jnp.ones_like(v)], axis=1)


def _softmax_pv(s, v1):
  p = jnp.exp2(s - jnp.max(s, axis=-1, keepdims=True)).astype(BF16)
  oa = jnp.dot(p, v1, preferred_element_type=F32)
  return oa[:, :HD] / oa[:, HD:]


def _attn_ctx_kernel(q_ref, k_ref, v_ref, o_ref):
  for b in range(CTX_NB):
    rows = slice(b * CTX_T, (b + 1) * CTX_T)
    for h in range(N_KV):
      kv_cols = slice(h * HD, (h + 1) * HD)
      v1 = _with_ones(v_ref[rows, kv_cols])
      for g in range(GQA):
        cols = slice((h * GQA + g) * HD, (h * GQA + g + 1) * HD)
        s = _nt_dot(q_ref[rows, cols], k_ref[rows, kv_cols])
        o_ref[rows, cols] = _softmax_pv(s, v1).astype(BF16)


def _attn_context(q, k, v):
  bm = CTX_NB * CTX_T
  row = lambda i: (i, 0)
  return pl.pallas_call(
      _attn_ctx_kernel,
      grid=(N_CTX_TOK // bm,),
      in_specs=[
          pl.BlockSpec((bm, Q_W), row),
          pl.BlockSpec((bm, KV_W), row),
          pl.BlockSpec((bm, KV_W), row),
      ],
      out_specs=pl.BlockSpec((bm, Q_W), row),
      out_shape=jax.ShapeDtypeStruct((N_TOK, Q_W), BF16),
      compiler_params=_params(1),
      name="attn_context",
  )(q, k, v)


def _attn_lat_kernel(*refs, cast_weights):
  q_ref, k_ref, v_ref, ck_ref, cv_ref, _ = refs[:6]
  kcat, vcat = refs[-2:]
  if cast_weights:
    _cast_ffn_weight_slab(refs[6:8], refs[-4:-2])
    o_ref = refs[8]
  else:
    o_ref = refs[6]

  @pl.when(pl.program_id(2) == 0)
  def _():
    head_rows = pl.ds(pl.program_id(1), PAST, stride=N_KV)
    kcat[:LAT_T] = k_ref[...]
    kcat[LAT_T:] = _swap_middle(ck_ref[0, 0, head_rows, :]).astype(BF16)
    vcat[:LAT_T] = _with_ones(v_ref[...])
    vcat[LAT_T:] = _with_ones(cv_ref[0, 0, head_rows, :].astype(BF16))

  chains = [(g, r, ATT_CHUNK) for g in range(GQA) for r in range(0, ATT_TQ, ATT_CHUNK)]
  halves = lambda g, r, n: [(g, r, n // 2), (g, r + n // 2, n // 2)]
  chains = halves(*chains[0]) + chains[1:-1] + halves(*chains[-1])
  for g, r, n in chains:
    rows, cols = slice(r, r + n), slice(g * HD, (g + 1) * HD)
    s = _nt_dot(q_ref[rows, cols], kcat[...])
    o_ref[rows, cols] = _softmax_pv(s, vcat[...]).astype(BF16)


def _attn_latent(q, k, v, cache_k, cache_v, mix, a, cast_w):
  nq = LAT_T // ATT_TQ
  q0 = N_CTX_TOK // ATT_TQ
  kv0 = N_CTX_TOK // LAT_T
  cache_rows = (N_LAT_SEQ, N_ATTN, PAST * N_KV, HD)
  cache_spec = pl.BlockSpec((1, 1, PAST * N_KV, HD), lambda b, h, j: (b, a, 0, 0))
  in_specs = [
      pl.BlockSpec((ATT_TQ, GQA * HD), lambda b, h, j: (q0 + b * nq + j, h)),
      pl.BlockSpec((LAT_T, HD), lambda b, h, j: (kv0 + b, h)),
      pl.BlockSpec((LAT_T, HD), lambda b, h, j: (kv0 + b, h)),
      cache_spec,
      cache_spec,
      pl.BlockSpec(memory_space=pl.ANY),
  ]
  args = [q, k, v, cache_k.reshape(cache_rows), cache_v.reshape(cache_rows), mix]
  out_specs = [pl.BlockSpec((ATT_TQ, GQA * HD), lambda b, h, j: (q0 + b * nq + j, h))]
  out_shape = [jax.ShapeDtypeStruct((N_TOK, Q_W), BF16)]
  if cast_w is not None:
    cast_in, cast_out, cast_shape = _ffn_weight_cast_specs(
        0, N_LAT_SEQ * N_KV * nq, lambda b, h, j: (b * N_KV + h) * nq + j)
    in_specs += cast_in
    args += list(cast_w)
    out_specs += cast_out
    out_shape += cast_shape
  outs = pl.pallas_call(
      functools.partial(_attn_lat_kernel, cast_weights=cast_w is not None),
      grid=(N_LAT_SEQ, N_KV, nq),
      in_specs=in_specs,
      out_specs=out_specs,
      out_shape=out_shape,
      scratch_shapes=[pltpu.VMEM((LAT_T + PAST, HD), BF16),
                      pltpu.VMEM((LAT_T + PAST, 2 * HD), BF16)],
      input_output_aliases={5: 0},
      compiler_params=_params(3),
      name="attn_latent",
  )(*args)
  return outs[0], outs[1:]


def _edge_window_sums(slab, w):
  rows = slab.shape[0]
  back, fwd, m = slab, slab, 1
  while m < w // 2:
    back = back + pltpu.roll(back, m, 0)
    fwd = fwd + pltpu.roll(fwd, rows - m, 0)
    m *= 2
  return (pltpu.roll(back, 1, 0) + fwd)[POOL_HALO:2 * POOL_HALO]


def _pool_bands():
  t = np.arange(POOL_TILE)[:, None]
  j = np.arange(POOL_TILE)[None, :]
  bands = [((j >= t - w // 2) & (j < t + w - w // 2)) / w for w in POOL_WINDOWS]
  return jnp.asarray(np.stack(bands), dtype=BF16)


def _pool_kernel(x_ref, xprev_ref, xnext_ref, ada_ref, g_ref, band_ref, w_ref, ps_ref, o_ref):
  i = pl.program_id(0)
  is_ctx = i < _n_ctx_blk(POOL_BM)
  shift, scale = _mod_rows(ada_ref, _cond_row(i, POOL_BM))[:2]
  gain = g_ref[0, 0:1] * (1.0 + scale)
  pre = lambda x: _rms_scale(x) * gain + shift
  h = pre(x_ref[...])
  seq_pos = _lat_seq_pos(i, POOL_BM)[1]
  is_first = jnp.logical_or(is_ctx, seq_pos == 0)
  is_last = jnp.logical_or(is_ctx, seq_pos == LAT_T // POOL_BM - 1)
  ctx_edge = jnp.where(is_ctx, 1.0, 0.0)
  first_edge = jnp.where(is_first, 1.0, 0.0)
  last_edge = jnp.where(is_last, 1.0, 0.0)
  h_prev = jnp.where(is_first, 0.0, pre(xprev_ref[...]))
  h_next = jnp.where(is_last, 0.0, pre(xnext_ref[...]))
  h_hi = h.astype(BF16)
  h_lo = (h - h_hi.astype(F32)).astype(BF16)
  n_tiles = POOL_BM // POOL_TILE
  row = lax.broadcasted_iota(jnp.int32, (POOL_HALO, POOL_DG), 0)
  units = [(g, w, j) for g, w in enumerate(POOL_WINDOWS) for j in range(n_tiles)]
  col_of = lambda g: slice(g * POOL_DG, (g + 1) * POOL_DG)
  row_of = lambda j: slice(j * POOL_TILE, (j + 1) * POOL_TILE)

  parts = [jnp.dot(band_ref[g],
                   jnp.concatenate([h_hi[row_of(j), col_of(g)], h_lo[row_of(j), col_of(g)]], axis=1),
                   preferred_element_type=F32) for g, w, j in units]

  pooled = []
  for (g, w, j), part in zip(units, parts):
    r0, cols = j * POOL_TILE, col_of(g)
    hg = h[row_of(j), cols]
    inner = part[:, :POOL_DG] + part[:, POOL_DG:]
    before = (h_prev[:, cols] if j == 0
              else jnp.where(is_ctx, 0.0, h[r0 - POOL_HALO:r0, cols]))
    after = (h_next[:, cols] if j == n_tiles - 1
             else jnp.where(is_ctx, 0.0, h[r0 + POOL_TILE:r0 + POOL_TILE + POOL_HALO, cols]))
    top = _edge_window_sums(jnp.concatenate([before, hg[:2 * POOL_HALO]], axis=0), w)
    bot = _edge_window_sums(jnp.concatenate([hg[-2 * POOL_HALO:], after], axis=0), w)
    starts = first_edge if j == 0 else ctx_edge
    ends = last_edge if j == n_tiles - 1 else ctx_edge
    top_cnt = w - jnp.maximum(w // 2 - row, 0).astype(F32) * starts
    bot_cnt = w - jnp.maximum(row + (w - w // 2) - POOL_HALO, 0).astype(F32) * ends
    mean = jnp.concatenate(
        [top / top_cnt, inner[POOL_HALO:-POOL_HALO], bot / bot_cnt], axis=0)
    pooled.append((mean - hg).astype(BF16))

  for (g, w, j), p in zip(units, pooled):
    og = jnp.dot(p, w_ref[0, g], preferred_element_type=F32)
    o_ref[row_of(j), col_of(g)] = (og * ps_ref[0, :, col_of(g)]).astype(BF16)


def _pool(x, ada, gains, w_pool, pool_scale, layer):
  p = layer // 2
  bm = POOL_BM
  halo_per_blk = bm // POOL_HALO
  n_halo = N_TOK // POOL_HALO
  prev_spec = pl.BlockSpec((POOL_HALO, D), lambda i: (jnp.maximum(i * halo_per_blk - 1, 0), 0))
  next_spec = pl.BlockSpec(
      (POOL_HALO, D), lambda i: (jnp.minimum((i + 1) * halo_per_blk, n_halo - 1), 0))
  return pl.pallas_call(
      _pool_kernel,
      grid=(N_TOK // bm,),
      in_specs=_token_specs(False, D, bm) + [
          prev_spec, next_spec,
          _layer_spec((N_MOD, N_COND, D), layer),
          _layer_spec((4, D), layer),
          pl.BlockSpec((N_POOL_G, POOL_TILE, POOL_TILE), lambda i: (0, 0, 0),
                       pipeline_mode=pl.Buffered(1)),
          _layer_spec((N_POOL_G, POOL_DG, POOL_DG), p),
          _layer_spec((1, D), p),
      ],
      out_specs=_token_specs(False, D, bm)[0],
      out_shape=_token_shapes(False, D, BF16)[0],
      compiler_params=_params(1),
      name="pool",
  )(x, x, x, ada, gains, _pool_bands(), w_pool, pool_scale.reshape(-1, 1, D))


def _ffn_kernel(*refs, n_x, n_out, has_wo, cast_next):
  ahead = not has_wo
  x_refs, refs = refs[:n_x], refs[n_x:]
  if ahead:
    xn_refs, refs = refs[:n_x], refs[n_x:]
    (x1_scr, h_scr), refs = refs[-2:], refs[:-2]
  if cast_next:
    cast_dst, refs = refs[-2:], refs[:-2]
  o_refs, refs = refs[-n_out:], refs[:-n_out]
  if cast_next:
    cast_src, refs = refs[-2:], refs[:-2]
  if has_wo:
    mix_ref, ada_ref, g_ref, wo_ref, wup_ref, wdn_ref = refs
  else:
    mix_ref, mixn_ref, ada_ref, g_ref, wup_ref, wdn_ref = refs
  i = pl.program_id(0)
  nxt = jnp.minimum(i + 1, pl.num_programs(0) - 1)
  n_ctx = _n_ctx_blk(FFN_BM)
  mod = _mod_rows(ada_ref, _cond_row(i, FFN_BM))
  post_gain = mod[5] * g_ref[0, 3:4]
  if cast_next:
    _cast_ffn_weight_slab(cast_src, cast_dst)

  def pre_ffn(x, mix, mod):
    _, _, gate_m, shift_f, scale_f, _ = mod
    if has_wo:
      a = jnp.dot(mix, wo_ref[0], preferred_element_type=F32)
    else:
      a = mix.astype(F32)
    x1 = x + _rms_scale(a) * (gate_m * g_ref[0, 1:2])
    return x1, (_rms_scale(x1) * (g_ref[0, 2:3] * (1.0 + scale_f)) + shift_f).astype(BF16)

  group = FFN_BM // FFN_GROUPS
  groups = [slice(r * group, (r + 1) * group) for r in range(FFN_GROUPS)]
  pre_group = lambda rows: pre_ffn(_token_load(i < n_ctx, x_refs, rows), mix_ref[rows, :], mod)
  if ahead:
    @pl.when(i == 0)
    def _():
      x1_scr[...], h_scr[...] = pre_group(groups[0])

    x1_h = [(x1_scr[...], h_scr[...])] + [pre_group(rows) for rows in groups[1:]]
  else:
    x1_h = [pre_group(rows) for rows in groups]
  x1, h = zip(*x1_h)
  gu, y = [], []
  for r in range(FFN_GROUPS + 1):
    if r < FFN_GROUPS:
      gu.append(jnp.dot(h[r], wup_ref[...], preferred_element_type=F32))
    if ahead and r == FFN_GROUPS - 1:
      x1_h_nxt = pre_ffn(_token_load(nxt < n_ctx, xn_refs), mixn_ref[...],
                         _mod_rows(ada_ref, _cond_row(nxt, FFN_BM)))
    if r > 0:
      act = (_silu(gu[r - 1][:, :D_FF]) * gu[r - 1][:, D_FF:]).astype(BF16)
      y.append(jnp.dot(act, wdn_ref[...], preferred_element_type=F32))
  for rows, x1_r, y_r in zip(groups, x1, y):
    _token_store(i < n_ctx, o_refs, x1_r + _rms_scale(y_r) * post_gain, rows)
  if ahead:
    x1_scr[...], h_scr[...] = x1_h_nxt


def _next_group_specs(split, width, bm, group):
  n_blk, n_ctx, per_blk = N_TOK // bm, _n_ctx_blk(bm), bm // group
  nxt = lambda i: jnp.minimum(i + 1, n_blk - 1)
  if not split:
    return [pl.BlockSpec((group, width), lambda i: (nxt(i) * per_blk, 0))]
  return [
      pl.BlockSpec((group, width), lambda i: (jnp.minimum(nxt(i), n_ctx - 1) * per_blk, 0)),
      pl.BlockSpec((group, width), lambda i: (jnp.maximum(nxt(i) - n_ctx, 0) * per_blk, 0))]


def _ffn(xs, mix, ada, gains, w_o, w_up, w_down, next_w, layer, split_out):
  has_wo = w_o is not None
  cast_next = next_w is not None
  resident = lambda shape: pl.BlockSpec(shape, lambda i: (0, 0), pipeline_mode=pl.Buffered(1))
  bm = FFN_BM
  n_blk = N_TOK // bm
  group = bm // FFN_GROUPS
  split_in = len(xs) == 2
  ahead = not has_wo
  nxt_specs = lambda split: _next_group_specs(split, D, bm, group) if ahead else []
  in_specs = (_token_specs(split_in, D, bm) + nxt_specs(split_in)
              + _token_specs(False, D, bm) + nxt_specs(False) + [
                  _layer_spec((N_MOD, N_COND, D), layer),
                  _layer_spec((4, D), layer),
              ])
  args = list(xs) * (2 if ahead else 1) + [mix] * (2 if ahead else 1) + [ada, gains]
  if has_wo:
    in_specs.append(_layer_spec((Q_W, D), layer // 2))
    args.append(w_o)
  in_specs += [resident((D, 2 * D_FF)), resident((D_FF, D))]
  args += [w_up, w_down]
  n_out = 2 if split_out else 1
  out_specs = _token_specs(split_out, D, bm)
  out_shape = _token_shapes(split_out, D, F32)
  if cast_next:
    cast_in, cast_out, cast_shape = _ffn_weight_cast_specs(layer + 1, n_blk, lambda i: i)
    in_specs += cast_in
    args += list(next_w)
    out_specs += cast_out
    out_shape += cast_shape
  outs = pl.pallas_call(
      functools.partial(_ffn_kernel, n_x=len(xs), n_out=n_out, has_wo=has_wo,
                        cast_next=cast_next),
      grid=(n_blk,),
      in_specs=in_specs,
      out_specs=out_specs,
      out_shape=out_shape,
      scratch_shapes=[pltpu.VMEM((group, D), F32), pltpu.VMEM((group, D), BF16)] if ahead else [],
      compiler_params=_params(1),
      name="ffn_attn" if has_wo else "ffn_pool",
  )(*args)
  return outs[:n_out], outs[n_out:]


def _rope_tables():
  pos = np.arange(LAT_T)
  row = (pos // GRID_W).astype(np.float32)
  col = (pos % GRID_W).astype(np.float32)
  inv = np.float32(ROPE_BASE) ** (-np.arange(ROPE_PAIRS, dtype=np.float32) / np.float32(ROPE_PAIRS))
  ar = row[:, None] * inv
  ac = col[:, None] * inv
  cos = np.concatenate([np.cos(ar), np.cos(ac), np.cos(ar), np.cos(ac)], axis=1)
  sin = np.concatenate([-np.sin(ar), -np.sin(ac), np.sin(ar), np.sin(ac)], axis=1)
  return jnp.asarray(np.concatenate([cos, sin], axis=1), dtype=F32)


def kernel(x_prompt, x_sample, c, cache_k, cache_v, c_ctx, w_ada, b_ada, norm_gains,
           w_qkv, qk_gains, w_o, w_pool, pool_scale, w_up, w_down):
  xs = [x_prompt.reshape(N_CTX_TOK, D), x_sample.reshape(N_LAT_TOK, D)]
  cond = jnp.concatenate(
      [c_ctx[None, :], c, jnp.zeros((N_COND - 1 - N_LAT_SEQ, D), F32)], axis=0)
  ada = _modulation(cond, w_ada, b_ada)
  rope_cs = _rope_tables()
  w_o_b = w_o.astype(BF16)
  w_pool_b = w_pool.astype(BF16)

  new_kv = ffn_w = None
  for layer in range(DEPTH):
    last = layer == DEPTH - 1
    next_w = None if last else (w_up, w_down)
    if layer % 2 == 0:
      q, k, v, *new_kv = _qkv(xs, ada, norm_gains, w_qkv, qk_gains, rope_cs, new_kv, layer)
      mix, cast_w = _attn_latent(q, k, v, cache_k, cache_v, _attn_context(q, k, v), layer // 2,
                                 (w_up, w_down) if ffn_w is None else None)
      ffn_w = ffn_w or cast_w
      xs, ffn_w = _ffn(xs, mix, ada, norm_gains, w_o_b, *ffn_w, next_w, layer, last)
    else:
      mix = _pool(xs[0], ada, norm_gains, w_pool_b, pool_scale, layer)
      xs, ffn_w = _ffn(xs, mix, ada, norm_gains, None, *ffn_w, next_w, layer, last)

  out_kv = (N_CTX_SEQ, N_ATTN, CTX_T, N_KV, HD)
  return (xs[0].reshape(N_CTX_SEQ, CTX_T, D), xs[1].reshape(N_LAT_SEQ, LAT_T, D),
          new_kv[0].reshape(out_kv), new_kv[1].reshape(out_kv))
```
